```python
import math
import jax
import jax.numpy as jnp
from jax import lax
import numpy as np

D_MODEL = 1024
BATCH = 4
SEQ = 4096
DEPTH = 4

CTX_LEN = 256
GRID_W = 64
N_BRANCH = 4
BRANCH_DIM = D_MODEL // 2
POOL_GROUPS = 4
POOL_GROUP_DIM = BRANCH_DIM // POOL_GROUPS
POOL_WINDOWS = (2, 4, 8, 16)
FOUR_GROUPS = 4
FOUR_GROUP_DIM = BRANCH_DIM // FOUR_GROUPS
SSD_HEADDIM = 64
SSD_HEADS = BRANCH_DIM // SSD_HEADDIM
SSD_STATE = 128
SSD_GROUPS = 2
SSD_CHUNK = 128
SSD_XBC = BRANCH_DIM + 2 * SSD_GROUPS * SSD_STATE
GDN_DK = 128
GDN_DV = 128
GDN_HEADS = BRANCH_DIM // GDN_DK
GDN_CHUNK = 64
SHORT_CONV = 5
PEER_HEADS = 8
PEER_NKEYS = 128
PEER_EXPERTS = PEER_NKEYS * PEER_NKEYS
PEER_DQ = 256
PEER_TOPK = 16
PEER_BLOCK = 128
DEEPNORM_ALPHA = (2 * DEPTH) ** 0.25
DEEPNORM_BETA = (8 * DEPTH) ** -0.25
LN_EPS = 1e-6
SPLIT_WIDTHS = (BRANCH_DIM, BRANCH_DIM, SSD_XBC, BRANCH_DIM, 2 * SSD_HEADS, 3 * BRANCH_DIM, BRANCH_DIM, 2 * GDN_HEADS, 2 * GDN_HEADS, N_BRANCH * D_MODEL)
N_IN = sum(SPLIT_WIDTHS)

kernel_name = 'hybrid_pool_ssd_deltanet_fourier_peer_dit'


def _layer_norm(x, g=None, b=None):
    xf = x.astype(jnp.float32)
    xc = xf - jnp.mean(xf, -1, keepdims=True)
    y = xc * lax.rsqrt(jnp.mean(xc * xc, -1, keepdims=True) + LN_EPS)
    if g is not None:
        y = y * g.astype(jnp.float32) + b.astype(jnp.float32)
    return y.astype(x.dtype)


def _modulate(x, shift, scale):
    return _layer_norm(x) * (1.0 + scale[:, None]) + shift[:, None]


def _rms(x):
    xf = x.astype(jnp.float32)
    return xf * lax.rsqrt(jnp.mean(xf * xf, -1, keepdims=True) + LN_EPS)


def _l2norm(x):
    return x * lax.rsqrt(jnp.sum(x * x, -1, keepdims=True) + 1e-6)


def _flip(a):
    return jnp.flip(a, axis=1)


def _split_proj(proj):
    offsets, acc = [], 0
    for w in SPLIT_WIDTHS[:-1]:
        acc += w
        offsets.append(acc)
    return jnp.split(proj, offsets, axis=-1)


def _dw_conv(x, w, b=None):
    k = w.shape[0]
    y = lax.conv_general_dilated(x, w[:, None, :], window_strides=(1,), padding=[(k // 2, k - 1 - k // 2)],
                                 dimension_numbers=('NWC', 'WIO', 'NWC'), feature_group_count=x.shape[-1])
    if b is not None:
        y = y + b
    return y


def _box_sum(x, win, axis):
    n = x.shape[axis]
    lo = win // 2
    hi = win - 1 - lo
    pad = [(0, 0)] * x.ndim
    pad[axis] = (lo + 1, hi)
    cs = jnp.cumsum(jnp.pad(x, pad), axis=axis)
    s = lax.slice_in_dim(cs, win, win + n, axis=axis) - lax.slice_in_dim(cs, 0, n, axis=axis)
    t = jnp.arange(n)
    cnt = (jnp.minimum(t + hi, n - 1) - jnp.maximum(t - lo, 0) + 1).astype(jnp.float32)
    return s, cnt


def _window_mean(u, win, rows):
    bsz, seq, ch = u.shape
    if rows is None:
        s, cnt = _box_sum(u, win, 1)
        return s / cnt[None, :, None]
    g = u.reshape(bsz, rows, GRID_W, ch)
    s, cnt_c = _box_sum(g, win, 2)
    s, cnt_r = _box_sum(s, win, 1)
    return (s / (cnt_r[:, None] * cnt_c[None, :])[None, :, :, None]).reshape(bsz, seq, ch)


def pool_branch(u, w_pool, s_pool, rows):
    bsz, seq, _ = u.shape
    ug = u.astype(jnp.float32).reshape(bsz, seq, POOL_GROUPS, POOL_GROUP_DIM)
    outs = []
    for gi, win in enumerate(POOL_WINDOWS):
        x_g = ug[:, :, gi]
        outs.append((_window_mean(x_g, win, rows) - x_g).astype(u.dtype) @ w_pool[gi])
    return jnp.concatenate(outs, axis=-1) * s_pool


def fourier_branch(u):
    bsz, seq, _ = u.shape
    ug = u.astype(jnp.float32).reshape(bsz, seq, FOUR_GROUPS, FOUR_GROUP_DIM)
    return jnp.fft.fft2(ug, axes=(1, 3), norm='ortho').real.reshape(bsz, seq, BRANCH_DIM).astype(u.dtype)


def ssd_scan(x, dt, a, bm, cm, s0):
    bsz, seq, nh, hp = x.shape
    ns = bm.shape[-1]
    q = SSD_CHUNK
    nc = seq // q
    la = (dt * a).reshape(bsz, nc, q, nh)
    xdt = (x * dt[..., None]).reshape(bsz, nc, q, nh, hp)
    bm = bm.reshape(bsz, nc, q, nh, ns)
    cm = cm.reshape(bsz, nc, q, nh, ns)
    acs = jnp.cumsum(la, axis=2)
    lower = jnp.tril(jnp.ones((q, q), bool))[None, None, :, :, None]
    decay = jnp.exp(jnp.where(lower, acs[:, :, :, None, :] - acs[:, :, None, :, :], -jnp.inf))
    scores = jnp.einsum('bcihn,bcjhn->bcijh', cm, bm) * decay
    y_diag = jnp.einsum('bcijh,bcjhp->bcihp', scores, xdt)
    last = acs[:, :, -1]
    w_state = jnp.exp(last[:, :, None] - acs)
    chunk_states = jnp.einsum('bcjhn,bcjhp->bchpn', bm, xdt * w_state[..., None])

    def step(s, inp):
        st, dl = inp
        return s * jnp.exp(dl)[:, :, None, None] + st, s

    s_fin, s_in = lax.scan(step, s0, (jnp.moveaxis(chunk_states, 1, 0), jnp.moveaxis(last, 1, 0)))
    s_in = jnp.moveaxis(s_in, 0, 1)
    y_off = jnp.einsum('bcihn,bchpn->bcihp', cm, s_in) * jnp.exp(acs)[..., None]
    return (y_diag + y_off).reshape(bsz, seq, nh, hp), s_fin


def gdn_scan(q, k, v, g, beta, s0):
    bsz, seq, nh, _ = q.shape
    dv = v.shape[-1]
    cl = GDN_CHUNK
    nc = seq // cl

    def chunks(t):
        return jnp.swapaxes(t.reshape(bsz, nc, cl, nh, -1), 2, 3)

    q, k, v = chunks(q), chunks(k), chunks(v)
    g = jnp.swapaxes(g.reshape(bsz, nc, cl, nh), 2, 3)
    beta = jnp.swapaxes(beta.reshape(bsz, nc, cl, nh), 2, 3)
    gc = jnp.cumsum(g, axis=-1)
    lower = jnp.tril(jnp.ones((cl, cl), bool))
    strict = jnp.tril(jnp.ones((cl, cl), bool), -1)
    decay = jnp.exp(jnp.where(lower, gc[..., :, None] - gc[..., None, :], -jnp.inf))
    kk = jnp.einsum('bchid,bchjd->bchij', k, k)
    a_mat = jnp.where(strict, kk * decay * beta[..., :, None], 0.0)
    ia = a_mat + jnp.eye(cl, dtype=jnp.float32)
    u = lax.linalg.triangular_solve(ia, v * beta[..., None], left_side=True, lower=True, unit_diagonal=True)
    w = lax.linalg.triangular_solve(ia, k * (beta * jnp.exp(gc))[..., None], left_side=True, lower=True, unit_diagonal=True)
    attn = jnp.einsum('bchid,bchjd->bchij', q, k) * decay
    qg = q * jnp.exp(gc)[..., None]
    glast = gc[..., -1]
    kdec = k * jnp.exp(glast[..., None] - gc)[..., None]

    def step(s, inp):
        u_c, w_c, at_c, qg_c, kd_c, gl_c = inp
        v_new = u_c - jnp.einsum('bhtk,bhkv->bhtv', w_c, s)
        o = jnp.einsum('bhtk,bhkv->bhtv', qg_c, s) + jnp.einsum('bhts,bhsv->bhtv', at_c, v_new)
        s = s * jnp.exp(gl_c)[..., None, None] + jnp.einsum('bhtk,bhtv->bhkv', kd_c, v_new)
        return s, o

    xs = tuple(jnp.moveaxis(t, 1, 0) for t in (u, w, attn, qg, kdec, glast))
    s_fin, o = lax.scan(step, s0, xs)
    o = jnp.swapaxes(jnp.moveaxis(o, 0, 1), 2, 3).reshape(bsz, seq, nh, dv)
    return o, s_fin


def recurrent_branches(parts, lp, init):
    f32 = jnp.float32
    ssd_xbc, ssd_dt, gdn_qkv, gdn_beta, gdn_a = parts[2], parts[4], parts[5], parts[7], parts[8]
    bsz, seq, _ = ssd_xbc.shape
    xbc = jax.nn.silu(_dw_conv(ssd_xbc, lp['ssd_conv_w'], lp['ssd_conv_b'])).astype(f32)
    sx, sb, sc = jnp.split(xbc, [BRANCH_DIM, BRANCH_DIM + SSD_GROUPS * SSD_STATE], axis=-1)
    sx = sx.reshape(bsz, seq, SSD_HEADS, SSD_HEADDIM)
    rep = SSD_HEADS // SSD_GROUPS
    sb = jnp.repeat(sb.reshape(bsz, seq, SSD_GROUPS, SSD_STATE), rep, axis=2)
    sc = jnp.repeat(sc.reshape(bsz, seq, SSD_GROUPS, SSD_STATE), rep, axis=2)
    dt = jax.nn.softplus(ssd_dt.astype(f32).reshape(bsz, seq, 2, SSD_HEADS) + lp['ssd_dt_bias'].astype(f32))
    a = -jnp.exp(lp['ssd_a_log'].astype(f32))
    y_f, s_f = ssd_scan(sx, dt[:, :, 0], a[0], sb, sc, init[0])
    y_b, s_b = ssd_scan(_flip(sx), _flip(dt[:, :, 1]), a[1], _flip(sb), _flip(sc), init[1])
    ssd_y = y_f + _flip(y_b) + lp['ssd_d'].astype(f32)[:, None] * sx
    qkv = jax.nn.silu(_dw_conv(gdn_qkv, lp['gdn_conv_w'])).astype(f32)
    q, k, v = (t.reshape(bsz, seq, GDN_HEADS, -1) for t in jnp.split(qkv, 3, axis=-1))
    q = _l2norm(q) * GDN_DK ** -0.5
    k = _l2norm(k)
    beta = jax.nn.sigmoid(gdn_beta.astype(f32).reshape(bsz, seq, 2, GDN_HEADS))
    g = -jnp.exp(lp['gdn_a_log'].astype(f32)) * jax.nn.softplus(
        gdn_a.astype(f32).reshape(bsz, seq, 2, GDN_HEADS) + lp['gdn_dt_bias'].astype(f32))
    o_f, gs_f = gdn_scan(q, k, v, g[:, :, 0], beta[:, :, 0], init[2])
    o_b, gs_b = gdn_scan(_flip(q), _flip(k), _flip(v), _flip(g[:, :, 1]), _flip(beta[:, :, 1]), init[3])
    gdn_o = o_f + _flip(o_b)
    return ssd_y, gdn_o, (s_f, s_b, gs_f, gs_b)


def token_mixer(h, lp, init, rows):
    bsz, seq, _ = h.shape
    f32 = jnp.float32
    parts = _split_proj(h @ lp['w_in'])
    pool_in, four_in, ssd_z, gdn_z, gate_in = parts[0], parts[1], parts[3], parts[6], parts[9]
    ssd_y, gdn_o, states = recurrent_branches(parts, lp, init)
    pool_out = pool_branch(pool_in, lp['pool_w'], lp['pool_scale'], rows)
    four_out = fourier_branch(four_in)
    ssd_g = ssd_y.reshape(bsz, seq, BRANCH_DIM) * jax.nn.silu(ssd_z.astype(f32))
    ssd_out = (_rms(ssd_g.reshape(bsz, seq, SSD_GROUPS, -1)).reshape(bsz, seq, BRANCH_DIM)
               * lp['ssd_norm_w'].astype(f32)).astype(h.dtype)
    gdn_gate = jax.nn.silu(gdn_z.astype(f32).reshape(bsz, seq, GDN_HEADS, GDN_DV))
    gdn_out = (_rms(gdn_o) * lp['gdn_norm_w'].astype(f32) * gdn_gate).reshape(bsz, seq, BRANCH_DIM).astype(h.dtype)
    gates = jax.nn.sigmoid(gate_in.reshape(bsz, seq, N_BRANCH, D_MODEL))
    branches = (pool_out, four_out, ssd_out, gdn_out)
    merged = sum(gates[:, :, i] * (br @ lp['w_branch'][i]) for i, br in enumerate(branches))
    return merged @ lp['w_out'], states


def context_states(h, lp, init):
    parts = _split_proj(h @ lp['w_in'])
    return recurrent_branches(parts, lp, init)[2]


def peer_ffn(h, wq, sub_keys, u_tab, v_tab):
    bsz, seq, dm = h.shape
    blocks = h.reshape(-1, PEER_BLOCK, dm)

    def one_block(hb):
        t = hb.shape[0]
        q = (hb @ wq).reshape(t, PEER_HEADS, 2, PEER_DQ // 2)
        s = jnp.einsum('thsd,hskd->thsk', q, sub_keys).astype(jnp.float32)
        v1, i1 = lax.top_k(s[:, :, 0], PEER_TOPK)
        v2, i2 = lax.top_k(s[:, :, 1], PEER_TOPK)
        cand = (v1[..., :, None] + v2[..., None, :]).reshape(t, PEER_HEADS, PEER_TOPK * PEER_TOPK)
        cid = (i1[..., :, None] * PEER_NKEYS + i2[..., None, :]).reshape(t, PEER_HEADS, PEER_TOPK * PEER_TOPK)
        best, pos = lax.top_k(cand, PEER_TOPK)
        eid = jnp.take_along_axis(cid, pos, axis=-1)
        gate = jax.nn.softmax(best, axis=-1)
        act = jax.nn.gelu(jnp.einsum('td,thkd->thk', hb, u_tab[eid]).astype(jnp.float32), approximate=False)
        return jnp.einsum('thk,thkd->td', (gate * act).astype(hb.dtype), v_tab[eid])

    return lax.map(one_block, blocks).reshape(bsz, seq, dm)


def setup_inputs(seed: int = 0) -> dict:
    key = jax.random.key(seed)
    keys = jax.random.split(key, 32)
    f32 = jnp.float32
    dm = D_MODEL
    nl = DEPTH

    def nrm(i, shape, scale):
        return jax.random.normal(keys[i], shape, f32) * scale

    def dt_bias(i, shape):
        dt = jnp.exp(jax.random.uniform(keys[i], shape, f32, math.log(1e-3), math.log(1e-1)))
        return dt + jnp.log(-jnp.expm1(-dt))

    def a_log(i, shape):
        return jnp.log(jax.random.uniform(keys[i], shape, f32, 1.0, 16.0))

    return {
        'x': nrm(0, (BATCH, SEQ, dm), 1.0),
        'c': nrm(1, (BATCH, dm), 1.0),
        'ctx': nrm(2, (BATCH, CTX_LEN, dm), 1.0),
        'c_ctx': nrm(3, (dm,), 1.0),
        'w_mod': nrm(4, (nl, dm, 6 * dm), 0.5 * dm ** -0.5),
        'b_mod': nrm(5, (nl, 6 * dm), 0.02),
        'w_in': nrm(6, (nl, dm, N_IN), dm ** -0.5),
        'pool_w': nrm(7, (nl, POOL_GROUPS, POOL_GROUP_DIM, POOL_GROUP_DIM), POOL_GROUP_DIM ** -0.5),
        'pool_scale': 1.0 + nrm(8, (nl, BRANCH_DIM), 0.1),
        'ssd_conv_w': nrm(9, (nl, SHORT_CONV, SSD_XBC), SHORT_CONV ** -0.5),
        'ssd_conv_b': nrm(10, (nl, SSD_XBC), 0.02),
        'ssd_dt_bias': dt_bias(11, (nl, 2, SSD_HEADS)),
        'ssd_a_log': a_log(12, (nl, 2, SSD_HEADS)),
        'ssd_d': 1.0 + nrm(13, (nl, SSD_HEADS), 0.1),
        'ssd_norm_w': 1.0 + nrm(14, (nl, BRANCH_DIM), 0.05),
        'gdn_conv_w': nrm(15, (nl, SHORT_CONV, 3 * BRANCH_DIM), SHORT_CONV ** -0.5),
        'gdn_dt_bias': dt_bias(16, (nl, 2, GDN_HEADS)),
        'gdn_a_log': a_log(17, (nl, 2, GDN_HEADS)),
        'gdn_norm_w': 1.0 + nrm(18, (nl, GDN_DV), 0.05),
        'w_branch': nrm(19, (nl, N_BRANCH, BRANCH_DIM, dm), DEEPNORM_BETA * BRANCH_DIM ** -0.5),
        'w_out': nrm(20, (nl, dm, dm), DEEPNORM_BETA * dm ** -0.5),
        'ln1_g': 1.0 + nrm(21, (nl, dm), 0.05),
        'ln1_b': nrm(22, (nl, dm), 0.02),
        'peer_wq': nrm(23, (nl, dm, PEER_HEADS * PEER_DQ), dm ** -0.5),
        'peer_keys': nrm(24, (nl, PEER_HEADS, 2, PEER_NKEYS, PEER_DQ // 2), (PEER_DQ // 2) ** -0.5),
        'peer_u': nrm(25, (nl, PEER_EXPERTS, dm), dm ** -0.5),
        'peer_v': nrm(26, (nl, PEER_EXPERTS, dm), DEEPNORM_BETA),
        'ln2_g': 1.0 + nrm(27, (nl, dm), 0.05),
        'ln2_b': nrm(28, (nl, dm), 0.02),
    }


def reference(x, c, ctx, c_ctx, w_mod, b_mod, w_in, pool_w, pool_scale, ssd_conv_w, ssd_conv_b, ssd_dt_bias,
              ssd_a_log, ssd_d, ssd_norm_w, gdn_conv_w, gdn_dt_bias, gdn_a_log, gdn_norm_w, w_branch, w_out,
              ln1_g, ln1_b, peer_wq, peer_keys, peer_u, peer_v, ln2_g, ln2_b):
    bsz = x.shape[0]
    rows = x.shape[1] // GRID_W
    f32 = jnp.float32
    zero_states = ((jnp.zeros((bsz, SSD_HEADS, SSD_HEADDIM, SSD_STATE), f32),) * 2
                   + (jnp.zeros((bsz, GDN_HEADS, GDN_DK, GDN_DV), f32),) * 2)
    silu_c = jax.nn.silu(c)
    silu_cc = jax.nn.silu(c_ctx)[None]
    for l in range(DEPTH):
        lp = {
            'w_in': w_in[l], 'pool_w': pool_w[l], 'pool_scale': pool_scale[l],
            'ssd_conv_w': ssd_conv_w[l], 'ssd_conv_b': ssd_conv_b[l], 'ssd_dt_bias': ssd_dt_bias[l],
            'ssd_a_log': ssd_a_log[l], 'ssd_d': ssd_d[l], 'ssd_norm_w': ssd_norm_w[l],
            'gdn_conv_w': gdn_conv_w[l], 'gdn_dt_bias': gdn_dt_bias[l], 'gdn_a_log': gdn_a_log[l],
            'gdn_norm_w': gdn_norm_w[l], 'w_branch': w_branch[l], 'w_out': w_out[l],
        }
        mod_x = jnp.split(silu_c @ w_mod[l] + b_mod[l], 6, axis=-1)
        mod_c = jnp.split(silu_cc @ w_mod[l] + b_mod[l], 6, axis=-1)
        h_c = _modulate(ctx, mod_c[0], mod_c[1])
        if l == DEPTH - 1:
            states = context_states(h_c, lp, zero_states)
        else:
            mix_c, states = token_mixer(h_c, lp, zero_states, None)
            ctx = _layer_norm(DEEPNORM_ALPHA * ctx + mod_c[2][:, None] * mix_c, ln1_g[l], ln1_b[l])
            h_c = _modulate(ctx, mod_c[3], mod_c[4])
            ffn_c = peer_ffn(h_c, peer_wq[l], peer_keys[l], peer_u[l], peer_v[l])
            ctx = _layer_norm(DEEPNORM_ALPHA * ctx + mod_c[5][:, None] * ffn_c, ln2_g[l], ln2_b[l])
        h_x = _modulate(x, mod_x[0], mod_x[1])
        mix_x, _ = token_mixer(h_x, lp, states, rows)
        x = _layer_norm(DEEPNORM_ALPHA * x + mod_x[2][:, None] * mix_x, ln1_g[l], ln1_b[l])
        h_x = _modulate(x, mod_x[3], mod_x[4])
        ffn_x = peer_ffn(h_x, peer_wq[l], peer_keys[l], peer_u[l], peer_v[l])
        x = _layer_norm(DEEPNORM_ALPHA * x + mod_x[5][:, None] * ffn_x, ln2_g[l], ln2_b[l])
    return x
```

```python
import functools
import math

import jax
import jax.numpy as jnp
from jax import lax
from jax.experimental import pallas as pl
from jax.experimental.pallas import tpu as pltpu

F32 = jnp.float32
BF16 = jnp.bfloat16
HIGHEST = lax.Precision.HIGHEST

LN_EPS = 1e-6
GRID_W = 64
BRANCH = 512
POOL_WINDOWS = (2, 4, 8, 16)
SSD_HEADS = 8
SSD_HEADDIM = 64
SSD_STATE = 128
SSD_CHUNK = 128
GDN_HEADS = 4
GDN_DK = 128
GDN_CHUNK = 64
CONV_K = 5
PEER_HEADS = 8
PEER_NKEYS = 128
PEER_TOPK = 16
N_MAIN = 8704
OFF_GATES, OFF_POOL, OFF_FOUR, OFF_XBC, OFF_QKV, OFF_SZ, OFF_GZ = 0, 4096, 4608, 5120, 6144, 7680, 8192
SM_DT, SM_BETA, SM_A = 0, 16, 24
VMEM_LIMIT = 52 * 1024 * 1024
NEG_INF = float("-inf")


def _cp(sem):
    return pltpu.CompilerParams(dimension_semantics=sem, vmem_limit_bytes=VMEM_LIMIT)


def _sigmoid(x):
    return 1.0 / (1.0 + jnp.exp(-x))


def _silu(x):
    return x * _sigmoid(x)


def _softplus(x):
    return jnp.maximum(x, 0.0) + jnp.log(1.0 + jnp.exp(-jnp.abs(x)))


def _ln(x):
    mu = jnp.mean(x, axis=-1, keepdims=True)
    xc = x - mu
    var = jnp.mean(xc * xc, axis=-1, keepdims=True)
    return xc * lax.rsqrt(var + LN_EPS)


def _dot(a, b, **kw):
    return jnp.dot(a, b, preferred_element_type=F32, **kw)


def _dot_nt(a, b, **kw):
    return lax.dot_general(a, b, (((1,), (1,)), ((), ())), preferred_element_type=F32, **kw)


def _gelu(x):
    return 0.5 * x * (1.0 + lax.erf(x * (2.0 ** -0.5)))


def _mod_kernel(c_ref, w_ref, b_ref, o_ref):
    s = _silu(c_ref[...])
    o_ref[0] = _dot(s.astype(BF16), w_ref[0]) + b_ref[0]


def _mod_all(cs, w_mod, b_mod):
    nl, d, n6 = w_mod.shape
    tn = 1536
    return pl.pallas_call(
        _mod_kernel,
        grid=(nl, n6 // tn),
        in_specs=[pl.BlockSpec((8, d), lambda l, j: (0, 0)),
                  pl.BlockSpec((1, d, tn), lambda l, j: (l, 0, j)),
                  pl.BlockSpec((1, 1, tn), lambda l, j: (l, 0, j))],
        out_specs=pl.BlockSpec((1, 8, tn), lambda l, j: (l, 0, j)),
        out_shape=jax.ShapeDtypeStruct((nl, 8, n6), F32),
        compiler_params=_cp(("parallel", "parallel")),
    )(cs, w_mod, b_mod.reshape(nl, 1, n6))


def _bmap(bs):
    return (lambda b, *_: (b, 0, 0)) if bs > 1 else (lambda b, *_: (0, 0, 0))


def _lnmod_mm_kernel(x_ref, sh_ref, sc_ref, w_ref, o_ref, h_scr):
    @pl.when(pl.program_id(2) == 0)
    def _():
        h = _ln(x_ref[0]) * (1.0 + sc_ref[0]) + sh_ref[0]
        h_scr[...] = h.astype(BF16)

    o_ref[0] = _dot(h_scr[...], w_ref[...])


def _lnmod_mm(x, shift, scale, w, tn):
    b, l, d = x.shape
    n = w.shape[1]
    tm = min(512, l)
    return pl.pallas_call(
        _lnmod_mm_kernel,
        grid=(b, l // tm, n // tn),
        in_specs=[pl.BlockSpec((1, tm, d), lambda b_, i, j: (b_, i, 0)),
                  pl.BlockSpec((1, 1, d), _bmap(shift.shape[0])),
                  pl.BlockSpec((1, 1, d), _bmap(scale.shape[0])),
                  pl.BlockSpec((d, tn), lambda b_, i, j: (0, j))],
        out_specs=pl.BlockSpec((1, tm, tn), lambda b_, i, j: (b_, i, j)),
        out_shape=jax.ShapeDtypeStruct((b, l, n), F32),
        scratch_shapes=[pltpu.VMEM((tm, d), BF16)],
        compiler_params=_cp(("parallel", "parallel", "arbitrary")),
    )(x, shift, scale, w)


def _small_kernel(x_ref, sh_ref, sc_ref, w_ref, wt_ref, o_ref, ot_ref):
    h = (_ln(x_ref[0]) * (1.0 + sc_ref[0]) + sh_ref[0]).astype(BF16)
    o_ref[0] = _dot(h, w_ref[...])
    ot_ref[0] = _dot_nt(wt_ref[...], h)


def _small_proj(x, shift, scale, w_sm, w_smt):
    b, l, d = x.shape
    tm = min(256, l)
    return pl.pallas_call(
        _small_kernel,
        grid=(b, l // tm),
        in_specs=[pl.BlockSpec((1, tm, d), lambda b_, i: (b_, i, 0)),
                  pl.BlockSpec((1, 1, d), _bmap(shift.shape[0])),
                  pl.BlockSpec((1, 1, d), _bmap(scale.shape[0])),
                  pl.BlockSpec((d, 128), lambda b_, i: (0, 0)),
                  pl.BlockSpec((32, d), lambda b_, i: (0, 0))],
        out_specs=[pl.BlockSpec((1, tm, 128), lambda b_, i: (b_, i, 0)),
                   pl.BlockSpec((1, 32, tm), lambda b_, i: (b_, 0, i))],
        out_shape=[jax.ShapeDtypeStruct((b, l, 128), F32), jax.ShapeDtypeStruct((b, 32, l), F32)],
        compiler_params=_cp(("parallel", "parallel")),
    )(x, shift, scale, w_sm, w_smt)


def _conv_kernel(x_ref, w_ref, b_ref, o_ref, pad_scr, *, l, rc):
    tc = x_ref.shape[2]
    zeros8 = jnp.zeros((8, tc), F32)
    pad_scr[0:8, :] = zeros8
    pad_scr[l + 8:l + 16, :] = zeros8
    pad_scr[8:l + 8, :] = x_ref[0]
    half = CONV_K // 2
    for r0 in range(0, l, rc):
        acc = b_ref[...] + w_ref[0:1, :] * pad_scr[r0 + 8 - half:r0 + 8 - half + rc, :]
        for k in range(1, CONV_K):
            acc = acc + w_ref[k:k + 1, :] * pad_scr[r0 + 8 - half + k:r0 + 8 - half + k + rc, :]
        o_ref[0, r0:r0 + rc, :] = _silu(acc)


def _conv_silu(proj, col_off, width, w, bias):
    b, l, _ = proj.shape
    tc = 256
    cb0 = col_off // tc
    rc = min(512, l)
    return pl.pallas_call(
        functools.partial(_conv_kernel, l=l, rc=rc),
        grid=(b, width // tc),
        in_specs=[pl.BlockSpec((1, l, tc), lambda b_, j: (b_, 0, cb0 + j)),
                  pl.BlockSpec((8, tc), lambda b_, j: (0, j)),
                  pl.BlockSpec((1, tc), lambda b_, j: (0, j))],
        out_specs=pl.BlockSpec((1, l, tc), lambda b_, j: (b_, 0, j)),
        out_shape=jax.ShapeDtypeStruct((b, l, width), F32),
        scratch_shapes=[pltpu.VMEM((l + 16, tc), F32)],
        compiler_params=_cp(("parallel", "parallel")),
    )(proj, w, bias)


def _tri(n, lower):
    ii = lax.broadcasted_iota(jnp.int32, (n, n), 0)
    jj = lax.broadcasted_iota(jnp.int32, (n, n), 1)
    return (jj <= ii) if lower else (jj >= ii)


def _ssd_kernel(xf_ref, xb_ref, smf_ref, smb_ref, smtf_ref, smtb_ref, par_ref, part_ref, dx_ref, e_ref, s0_ref,
                yf_ref, yb_ref, sfin_ref, s_scr, *, nc):
    c = pl.program_id(1)

    @pl.when(c == 0)
    def _():
        s_scr[...] = s0_ref[0]

    q = SSD_CHUNK
    low = _tri(q, True)
    upp = _tri(q, False)
    lowf = low.astype(F32)
    uppf = upp.astype(F32)
    lane = lax.broadcasted_iota(jnp.int32, (1, 256), 1) // SSD_HEADDIM
    dirs = ((xf_ref, smf_ref, smtf_ref, yf_ref), (xb_ref, smb_ref, smtb_ref, yb_ref))
    for d, (x_ref, sm_ref, smt_ref, y_ref) in enumerate(dirs):
        mask = low if d == 0 else upp
        xbc = x_ref[0]
        xs = xbc[:, 0:512]
        dt = _softplus(sm_ref[0] + par_ref[0:1, :])
        la = dt * (-jnp.exp(par_ref[1:2, :]))
        cum = _dot(lowf if d == 0 else uppf, la, precision=HIGHEST)
        dtt = _softplus(smt_ref[0, 8 * d:8 * d + 8, :] + part_ref[8 * d:8 * d + 8, 0:1])
        lat = dtt * (-jnp.exp(part_ref[8 * d:8 * d + 8, 1:2]))
        cumt = _dot(lat, uppf if d == 0 else lowf, precision=HIGHEST)
        expand = e_ref[d]
        dtx = _dot(dt, expand, precision=HIGHEST)
        cumx = _dot(cum, expand, precision=HIGHEST)
        tot = cumx[q - 1:q, :] if d == 0 else cumx[0:1, :]
        xdt = xs * dtx
        ecum = jnp.exp(cumx)
        xw = (xdt * jnp.exp(tot - cumx)).astype(BF16)
        etot = jnp.exp(tot)
        ys = []
        for g in range(2):
            bg = xbc[:, 512 + 128 * g:640 + 128 * g]
            cg = xbc[:, 768 + 128 * g:896 + 128 * g].astype(BF16)
            sl = slice(256 * g, 256 * g + 256)
            s_old = s_scr[d, :, sl]
            scores = _dot_nt(cg, bg.astype(BF16))
            y = _dot(cg, s_old.astype(BF16)) * ecum[:, sl]
            xg = xdt[:, sl]
            for hh in range(4):
                h = 4 * g + hh
                col = cum[:, 8 * d + h:8 * d + h + 1]
                row = cumt[h:h + 1, :]
                dec = jnp.exp(jnp.where(mask, col - row, NEG_INF))
                xm = jnp.where(lane == hh, xg, 0.0).astype(BF16)
                y = y + _dot((scores * dec).astype(BF16), xm)
            ys.append(y)
            s_scr[d, :, sl] = s_old * etot[:, sl] + _dot(bg.T.astype(BF16), xw[:, sl])
        y = jnp.concatenate(ys, axis=1)
        if d == 0:
            y = y + dx_ref[...] * xs
        y_ref[0] = y

    @pl.when(c == nc - 1)
    def _():
        sfin_ref[0] = s_scr[...]


def _ssd(conv_x, small, small_t, par, par_t, dx, expand, s0):
    b, l, _ = conv_x.shape
    q = SSD_CHUNK
    nc = l // q
    fw = lambda b_, c: (b_, c, 0)
    bw = lambda b_, c: (b_, nc - 1 - c, 0)
    fwt = lambda b_, c: (b_, 0, c)
    bwt = lambda b_, c: (b_, 0, nc - 1 - c)
    full2 = lambda b_, c: (0, 0)
    return pl.pallas_call(
        functools.partial(_ssd_kernel, nc=nc),
        grid=(b, nc),
        in_specs=[pl.BlockSpec((1, q, 1024), fw), pl.BlockSpec((1, q, 1024), bw),
                  pl.BlockSpec((1, q, 128), fw), pl.BlockSpec((1, q, 128), bw),
                  pl.BlockSpec((1, 32, q), fwt), pl.BlockSpec((1, 32, q), bwt),
                  pl.BlockSpec((8, 128), full2), pl.BlockSpec((32, 128), full2),
                  pl.BlockSpec((1, 512), full2),
                  pl.BlockSpec((2, 128, 512), lambda b_, c: (0, 0, 0)),
                  pl.BlockSpec((1, 2, 128, 512), lambda b_, c: (b_, 0, 0, 0))],
        out_specs=[pl.BlockSpec((1, q, 512), fw), pl.BlockSpec((1, q, 512), bw),
                   pl.BlockSpec((1, 2, 128, 512), lambda b_, c: (b_, 0, 0, 0))],
        out_shape=[jax.ShapeDtypeStruct((b, l, 512), F32), jax.ShapeDtypeStruct((b, l, 512), F32),
                   jax.ShapeDtypeStruct((b, 2, 128, 512), F32)],
        scratch_shapes=[pltpu.VMEM((2, 128, 512), F32)],
        compiler_params=_cp(("parallel", "arbitrary")),
    )(conv_x, conv_x, small, small, small_t, small_t, par, par_t, dx, expand, s0)


def _unit_tri_inverse(a, eye):
    t = eye - a
    p = a
    for _ in range(5):
        p = _dot(p, p, precision=HIGHEST)
        t = t + _dot(t, p, precision=HIGHEST)
    return t


def _gdn_kernel(qf_ref, kf_ref, vf_ref, qb_ref, kb_ref, vb_ref, smf_ref, smb_ref, smtf_ref, smtb_ref,
                par_ref, part_ref, s0_ref, of_ref, ob_ref, sfin_ref, s_scr, *, nc):
    c = pl.program_id(1)

    @pl.when(c == 0)
    def _():
        s_scr[...] = s0_ref[0]

    cl = GDN_CHUNK
    low = _tri(cl, True)
    upp = _tri(cl, False)
    lowf = low.astype(F32)
    uppf = upp.astype(F32)
    ii = lax.broadcasted_iota(jnp.int32, (cl, cl), 0)
    jj = lax.broadcasted_iota(jnp.int32, (cl, cl), 1)
    eye = (ii == jj).astype(F32)
    zpad = jnp.zeros((cl, GDN_DK), F32)
    dirs = ((qf_ref, kf_ref, vf_ref, smf_ref, smtf_ref, of_ref), (qb_ref, kb_ref, vb_ref, smb_ref, smtb_ref, ob_ref))
    for d, (q_ref, k_ref, v_ref, sm_ref, smt_ref, o_ref) in enumerate(dirs):
        mask = low if d == 0 else upp
        strict = (jj < ii) if d == 0 else (jj > ii)
        sm = sm_ref[0]
        beta_all = _sigmoid(sm)
        g_all = -jnp.exp(par_ref[2:3, :]) * _softplus(sm + par_ref[3:4, :])
        gt8 = -jnp.exp(part_ref[24:32, 2:3]) * _softplus(smt_ref[0, 24:32, :] + part_ref[24:32, 3:4])
        for sc in ((0, 1) if d == 0 else (1, 0)):
            r0 = cl * sc
            cum = _dot(lowf if d == 0 else uppf, g_all[r0:r0 + cl, :], precision=HIGHEST)
            cumt = _dot(gt8[:, r0:r0 + cl], uppf if d == 0 else lowf, precision=HIGHEST)
            for h in range(GDN_HEADS):
                hs = slice(GDN_DK * h, GDN_DK * h + GDN_DK)
                ln = SM_A + 4 * d + h
                qh = q_ref[0, r0:r0 + cl, hs]
                kh = k_ref[0, r0:r0 + cl, hs]
                vh = v_ref[0, r0:r0 + cl, hs]
                qh = qh * lax.rsqrt(jnp.sum(qh * qh, axis=-1, keepdims=True) + 1e-6) * (GDN_DK ** -0.5)
                kh = kh * lax.rsqrt(jnp.sum(kh * kh, axis=-1, keepdims=True) + 1e-6)
                col = cum[:, ln:ln + 1]
                row = cumt[4 * d + h:4 * d + h + 1, :]
                beta = beta_all[r0:r0 + cl, SM_BETA + 4 * d + h:SM_BETA + 4 * d + h + 1]
                dec = jnp.exp(jnp.where(mask, col - row, NEG_INF))
                khb = kh.astype(BF16)
                qhb = qh.astype(BF16)
                a_mat = jnp.where(strict, _dot_nt(khb, khb) * dec * beta, 0.0)
                t_inv = _unit_tri_inverse(a_mat, eye)
                ecol = jnp.exp(col)
                tot = col[cl - 1:cl, :] if d == 0 else col[0:1, :]
                rhs = jnp.concatenate([vh * beta, kh * (beta * ecol)], axis=1)
                uw = _dot(t_inv, rhs, precision=HIGHEST)
                attn = _dot_nt(qhb, khb) * dec
                kdec = kh * jnp.exp(tot - col)
                s_old = s_scr[d, h]
                wq = jnp.concatenate([uw[:, GDN_DK:], qh * ecol], axis=0).astype(BF16)
                ws_qs = _dot(wq, s_old.astype(BF16))
                v_new = uw[:, :GDN_DK] - ws_qs[0:cl]
                vnb = v_new.astype(BF16)
                o_ref[0, r0:r0 + cl, hs] = ws_qs[cl:] + _dot(attn.astype(BF16), vnb)
                kd_pad = jnp.concatenate([kdec, zpad], axis=0).T.astype(BF16)
                vn_pad = jnp.concatenate([v_new, zpad], axis=0).astype(BF16)
                s_scr[d, h] = s_old * jnp.exp(tot) + _dot(kd_pad, vn_pad)

    @pl.when(c == nc - 1)
    def _():
        sfin_ref[0] = s_scr[...]


def _gdn(conv_g, small, small_t, par, par_t, s0):
    b, l, _ = conv_g.shape
    rows = 2 * GDN_CHUNK
    nc = l // rows
    fwm = lambda j: (lambda b_, c: (b_, c, j))
    bwm = lambda j: (lambda b_, c: (b_, nc - 1 - c, j))
    fw, bw = fwm(0), bwm(0)
    fwt = lambda b_, c: (b_, 0, c)
    bwt = lambda b_, c: (b_, 0, nc - 1 - c)
    full2 = lambda b_, c: (0, 0)
    st = lambda b_, c: (b_, 0, 0, 0, 0)
    return pl.pallas_call(
        functools.partial(_gdn_kernel, nc=nc),
        grid=(b, nc),
        in_specs=[pl.BlockSpec((1, rows, 512), fwm(0)), pl.BlockSpec((1, rows, 512), fwm(1)),
                  pl.BlockSpec((1, rows, 512), fwm(2)),
                  pl.BlockSpec((1, rows, 512), bwm(0)), pl.BlockSpec((1, rows, 512), bwm(1)),
                  pl.BlockSpec((1, rows, 512), bwm(2)),
                  pl.BlockSpec((1, rows, 128), fw), pl.BlockSpec((1, rows, 128), bw),
                  pl.BlockSpec((1, 32, rows), fwt), pl.BlockSpec((1, 32, rows), bwt),
                  pl.BlockSpec((8, 128), full2), pl.BlockSpec((32, 128), full2),
                  pl.BlockSpec((1, 2, GDN_HEADS, 128, 128), st)],
        out_specs=[pl.BlockSpec((1, rows, 512), fw), pl.BlockSpec((1, rows, 512), bw),
                   pl.BlockSpec((1, 2, GDN_HEADS, 128, 128), st)],
        out_shape=[jax.ShapeDtypeStruct((b, l, 512), F32), jax.ShapeDtypeStruct((b, l, 512), F32),
                   jax.ShapeDtypeStruct((b, 2, GDN_HEADS, 128, 128), F32)],
        scratch_shapes=[pltpu.VMEM((2, GDN_HEADS, 128, 128), F32)],
        compiler_params=_cp(("parallel", "arbitrary")),
    )(conv_g, conv_g, conv_g, conv_g, conv_g, conv_g, small, small, small_t, small_t, par, par_t, s0)


def _pool_body(x_ref, w_ref, sc_ref, o_ref, pa, pb, *, l, gw, two_d, win, rc):
    lo = win // 2
    hi = win - 1 - lo
    nrow = l // gw
    shift = int(math.log2(gw))
    pbw = 8 * gw
    z8 = jnp.zeros((8, 128), F32)
    pa[0:8, :] = z8
    pa[l + 8:l + 16, :] = z8
    pa[8:l + 8, :] = x_ref[0]
    if two_d:
        zb = jnp.zeros((pbw, 128), F32)
        pb[0:pbw, :] = zb
        pb[pbw + l:pbw + l + pbw, :] = zb

    def finish(r0, acc, pos):
        col = pos & (gw - 1)
        cnt = (jnp.minimum(col + hi, gw - 1) - jnp.maximum(col - lo, 0) + 1).astype(F32)
        if two_d:
            row = pos >> shift
            cnt = cnt * (jnp.minimum(row + hi, nrow - 1) - jnp.maximum(row - lo, 0) + 1).astype(F32)
        dlt = (acc / cnt - x_ref[0, r0:r0 + rc, :]).astype(BF16)
        o_ref[0, r0:r0 + rc, :] = _dot(dlt, w_ref[0]) * sc_ref[...]

    for r0 in range(0, l, rc):
        pos = lax.broadcasted_iota(jnp.int32, (rc, 128), 0) + r0
        col = pos & (gw - 1)
        acc = pa[8 + r0:8 + r0 + rc, :]
        for j in range(-lo, hi + 1):
            if j == 0:
                continue
            v = pa[8 + r0 + j:8 + r0 + j + rc, :]
            ok = (col + j >= 0) if j < 0 else (col + j < gw)
            acc = acc + jnp.where(ok, v, 0.0)
        if two_d:
            pb[pbw + r0:pbw + r0 + rc, :] = acc
        else:
            finish(r0, acc, pos)
    if two_d:
        for r0 in range(0, l, rc):
            pos = lax.broadcasted_iota(jnp.int32, (rc, 128), 0) + r0
            acc = pb[pbw + r0:pbw + r0 + rc, :]
            for j in range(-lo, hi + 1):
                if j != 0:
                    acc = acc + pb[pbw + r0 + gw * j:pbw + r0 + gw * j + rc, :]
            finish(r0, acc, pos)


def _pool_kernel(x_ref, w_ref, sc_ref, o_ref, pa, pb, **kw):
    g = pl.program_id(1)
    for gi, win in enumerate(POOL_WINDOWS):
        @pl.when(g == gi)
        def _(win=win):
            _pool_body(x_ref, w_ref, sc_ref, o_ref, pa, pb, win=win, **kw)


def _pool(proj, pool_w, pool_scale, two_d):
    b, l, _ = proj.shape
    gw = GRID_W if two_d else l
    assert gw & (gw - 1) == 0 and l % gw == 0
    rc = min(512, l)
    cb0 = OFF_POOL // 128
    pb_rows = l + 16 * gw if two_d else 8
    return pl.pallas_call(
        functools.partial(_pool_kernel, l=l, gw=gw, two_d=two_d, rc=rc),
        grid=(b, len(POOL_WINDOWS)),
        in_specs=[pl.BlockSpec((1, l, 128), lambda b_, g: (b_, 0, cb0 + g)),
                  pl.BlockSpec((1, 128, 128), lambda b_, g: (g, 0, 0)),
                  pl.BlockSpec((1, 128), lambda b_, g: (0, g))],
        out_specs=pl.BlockSpec((1, l, 128), lambda b_, g: (b_, 0, g)),
        out_shape=jax.ShapeDtypeStruct((b, l, BRANCH), F32),
        scratch_shapes=[pltpu.VMEM((l + 16, 128), F32), pltpu.VMEM((pb_rows, 128), F32)],
        compiler_params=_cp(("parallel", "parallel")),
    )(proj, pool_w, pool_scale)


def _four1_kernel(x_ref, cc_ref, ss_ref, xc_ref, xs_ref):
    x = x_ref[0]
    xc_ref[0] = _dot(x, cc_ref[...], precision=HIGHEST).astype(BF16)
    xs_ref[0] = _dot(x, ss_ref[...], precision=HIGHEST).astype(BF16)


def _four2_kernel(ct_ref, st_ref, xc_ref, xs_ref, o_ref):
    o_ref[0] = _dot(ct_ref[...], xc_ref[0]) + _dot(st_ref[...], xs_ref[0])


def _fourier(proj, tabs):
    cc, ss, cl_, sl_ = tabs
    b, l, _ = proj.shape
    tm = min(512, l)
    cb = OFF_FOUR // BRANCH
    xc, xs = pl.pallas_call(
        _four1_kernel,
        grid=(b, l // tm),
        in_specs=[pl.BlockSpec((1, tm, BRANCH), lambda b_, i: (b_, i, cb)),
                  pl.BlockSpec((BRANCH, BRANCH), lambda b_, i: (0, 0)),
                  pl.BlockSpec((BRANCH, BRANCH), lambda b_, i: (0, 0))],
        out_specs=[pl.BlockSpec((1, tm, BRANCH), lambda b_, i: (b_, i, 0))] * 2,
        out_shape=[jax.ShapeDtypeStruct((b, l, BRANCH), BF16)] * 2,
        compiler_params=_cp(("parallel", "parallel")),
    )(proj, cc, ss)
    tr = min(256, l)
    return pl.pallas_call(
        _four2_kernel,
        grid=(b, l // tr),
        in_specs=[pl.BlockSpec((tr, l), lambda b_, i: (i, 0)),
                  pl.BlockSpec((tr, l), lambda b_, i: (i, 0)),
                  pl.BlockSpec((1, l, BRANCH), lambda b_, i: (b_, 0, 0)),
                  pl.BlockSpec((1, l, BRANCH), lambda b_, i: (b_, 0, 0))],
        out_specs=pl.BlockSpec((1, tr, BRANCH), lambda b_, i: (b_, i, 0)),
        out_shape=jax.ShapeDtypeStruct((b, l, BRANCH), F32),
        compiler_params=_cp(("parallel", "parallel")),
    )(cl_, sl_, xc, xs)


def _dft_tables(l):
    c = BRANCH // 4
    k = jnp.arange(c, dtype=jnp.int32)
    ang = (2.0 * math.pi / c) * ((k[:, None] * k[None, :]) % c).astype(F32)
    scale = 1.0 / math.sqrt(l * c)
    eye4 = jnp.eye(4, dtype=F32)
    cc = jnp.kron(eye4, jnp.cos(ang) * scale)
    ss = jnp.kron(eye4, jnp.sin(ang) * scale)
    t = jnp.arange(l, dtype=jnp.int32)
    angl = (2.0 * math.pi / l) * ((t[:, None] * t[None, :]) % l).astype(F32)
    return cc, ss, jnp.cos(angl).astype(BF16), (-jnp.sin(angl)).astype(BF16)


def _merge_kernel(x_ref, g1_ref, lng_ref, lnb_ref, gates_ref, pool_ref, four_ref, yf_ref, yb_ref, sz_ref,
                  of_ref, ob_ref, gz_ref, snw_ref, gnw_ref, wbr_ref, wout_ref, o_ref, *, alpha):
    d = x_ref.shape[2]
    ssd_g = (yf_ref[0] + yb_ref[0]) * _silu(sz_ref[0])
    parts = []
    for g in range(2):
        blk = ssd_g[:, 256 * g:256 * g + 256]
        parts.append(blk * lax.rsqrt(jnp.mean(blk * blk, axis=-1, keepdims=True) + LN_EPS))
    ssd_out = jnp.concatenate(parts, axis=1) * snw_ref[...]
    o = of_ref[0] + ob_ref[0]
    parts = []
    for h in range(GDN_HEADS):
        blk = o[:, 128 * h:128 * h + 128]
        parts.append(blk * lax.rsqrt(jnp.mean(blk * blk, axis=-1, keepdims=True) + LN_EPS))
    gdn_out = jnp.concatenate(parts, axis=1) * gnw_ref[...] * _silu(gz_ref[0])
    merged = None
    for i, br in enumerate((pool_ref[0], four_ref[0], ssd_out, gdn_out)):
        term = _sigmoid(gates_ref[0, :, d * i:d * i + d]) * _dot(br.astype(BF16), wbr_ref[i])
        merged = term if merged is None else merged + term
    mix = _dot(merged.astype(BF16), wout_ref[...])
    y = alpha * x_ref[0] + g1_ref[0] * mix
    o_ref[0] = _ln(y) * lng_ref[...] + lnb_ref[...]


def _merge(x, gate1, ln_g, ln_b, proj, pool_o, four_o, yf, yb, of, ob, snw, gnw, wbr, wout, alpha):
    b, l, d = x.shape
    tm = min(256, l)
    row = lambda b_, i: (b_, i, 0)
    colb = lambda j: (lambda b_, i: (b_, i, j))
    vec = lambda b_, i: (0, 0)
    br = pl.BlockSpec((1, tm, BRANCH), row)
    return pl.pallas_call(
        functools.partial(_merge_kernel, alpha=alpha),
        grid=(b, l // tm),
        in_specs=[pl.BlockSpec((1, tm, d), row),
                  pl.BlockSpec((1, 1, d), _bmap(gate1.shape[0])),
                  pl.BlockSpec((1, d), vec), pl.BlockSpec((1, d), vec),
                  pl.BlockSpec((1, tm, 4 * d), colb(OFF_GATES // (4 * d))),
                  br, br, br, br,
                  pl.BlockSpec((1, tm, BRANCH), colb(OFF_SZ // BRANCH)),
                  br, br,
                  pl.BlockSpec((1, tm, BRANCH), colb(OFF_GZ // BRANCH)),
                  pl.BlockSpec((1, BRANCH), vec), pl.BlockSpec((1, BRANCH), vec),
                  pl.BlockSpec((4, BRANCH, d), lambda b_, i: (0, 0, 0)),
                  pl.BlockSpec((d, d), vec)],
        out_specs=pl.BlockSpec((1, tm, d), row),
        out_shape=jax.ShapeDtypeStruct((b, l, d), F32),
        compiler_params=_cp(("parallel", "parallel")),
    )(x, gate1, ln_g, ln_b, proj, pool_o, four_o, yf, yb, proj, of, ob, proj, snw, gnw, wbr, wout)


def _top_desc(cur, rows, n):
    idx = lax.broadcasted_iota(jnp.int32, cur.shape, 0).astype(F32)
    out = []
    for r in range(n):
        m = jnp.max(cur, axis=0, keepdims=True)
        out.append(m)
        if r + 1 < n:
            first = jnp.min(jnp.where(cur == m, idx, float(rows)), axis=0, keepdims=True)
            cur = jnp.where(idx == first, NEG_INF, cur)
    return out


def _peer_select_kernel(x_ref, sh_ref, sc_ref, wqt_ref, keys_ref, s_ref, e_ref, thr_ref, qt_scr, cand_scr):
    k1 = PEER_TOPK + 1
    h = (_ln(x_ref[0]) * (1.0 + sc_ref[0]) + sh_ref[0]).astype(BF16)
    qt_scr[...] = _dot_nt(wqt_ref[...], h).astype(BF16)
    pairs = [(a, b) for a in range(k1) for b in range(k1) if (a + 1) * (b + 1) <= k1]
    ncand = cand_scr.shape[0]
    t = x_ref.shape[1]
    for r in range(len(pairs), ncand):
        cand_scr[r:r + 1, :] = jnp.full((1, t), NEG_INF, F32)
    for hd in range(PEER_HEADS):
        tops = []
        for s in range(2):
            hs = 2 * hd + s
            st = _dot(keys_ref[hs], qt_scr[128 * hs:128 * hs + 128, :])
            s_ref[0, hs] = st
            tops.append(_top_desc(st, PEER_NKEYS, k1))
        v1, v2 = tops
        for r, (a, b) in enumerate(pairs):
            cand_scr[r:r + 1, :] = v1[a] + v2[b]
        best = _top_desc(cand_scr[...], ncand, k1)
        z = None
        for kk in range(PEER_TOPK):
            term = jnp.exp(best[kk] - best[0])
            z = term if z is None else z + term
        e_ref[0, 2 * hd] = jnp.exp(s_ref[0, 2 * hd] - v1[0])
        e_ref[0, 2 * hd + 1] = jnp.exp(s_ref[0, 2 * hd + 1] - v2[0]) / z
        thr_ref[0, hd:hd + 1, :] = 0.5 * (best[PEER_TOPK - 1] + best[PEER_TOPK])


def _peer_select(x, shift, scale, wqt, keys):
    b, l, d = x.shape
    t = min(256, l)
    nh2 = 2 * PEER_HEADS
    k1 = PEER_TOPK + 1
    npairs = sum(1 for a in range(k1) for b_ in range(k1) if (a + 1) * (b_ + 1) <= k1)
    ncand = -(-npairs // 8) * 8
    big = pl.BlockSpec((1, nh2, PEER_NKEYS, t), lambda b_, i: (b_, 0, 0, i))
    return pl.pallas_call(
        _peer_select_kernel,
        grid=(b, l // t),
        in_specs=[pl.BlockSpec((1, t, d), lambda b_, i: (b_, i, 0)),
                  pl.BlockSpec((1, 1, d), _bmap(shift.shape[0])),
                  pl.BlockSpec((1, 1, d), _bmap(scale.shape[0])),
                  pl.BlockSpec((nh2 * 128, d), lambda b_, i: (0, 0)),
                  pl.BlockSpec((nh2, PEER_NKEYS, 128), lambda b_, i: (0, 0, 0))],
        out_specs=[big, big, pl.BlockSpec((1, PEER_HEADS, t), lambda b_, i: (b_, 0, i))],
        out_shape=[jax.ShapeDtypeStruct((b, nh2, PEER_NKEYS, l), F32),
                   jax.ShapeDtypeStruct((b, nh2, PEER_NKEYS, l), F32),
                   jax.ShapeDtypeStruct((b, PEER_HEADS, l), F32)],
        scratch_shapes=[pltpu.VMEM((nh2 * 128, t), BF16), pltpu.VMEM((ncand, t), F32)],
        compiler_params=_cp(("parallel", "parallel")),
    )(x, shift, scale, wqt, keys)


def _peer_dense_kernel(x_ref, sh_ref, sc_ref, g2_ref, lng_ref, lnb_ref, s_ref, e_ref, thr_ref, u_ref, vt_ref,
                       o_ref, h_scr, acc_scr, *, n_i1, n_eb, alpha):
    eb = pl.program_id(2)

    @pl.when(eb == 0)
    def _():
        h = _ln(x_ref[0]) * (1.0 + sc_ref[0]) + sh_ref[0]
        h_scr[...] = h.astype(BF16)
        acc_scr[...] = jnp.zeros(acc_scr.shape, F32)

    act = _gelu(_dot_nt(u_ref[...], h_scr[...]))
    blocks = []
    for ii in range(n_i1):
        i1 = eb * n_i1 + ii
        w = None
        for hd in range(PEER_HEADS):
            s1 = s_ref[0, 2 * hd, pl.ds(i1, 1), :]
            e1 = e_ref[0, 2 * hd, pl.ds(i1, 1), :]
            sel = (s_ref[0, 2 * hd + 1] + s1) > thr_ref[0, hd:hd + 1, :]
            term = jnp.where(sel, e_ref[0, 2 * hd + 1] * e1, 0.0)
            w = term if w is None else w + term
        blocks.append((w * act[128 * ii:128 * ii + 128, :]).astype(BF16))
    p = jnp.concatenate(blocks, axis=0)
    acc_scr[...] += _dot(vt_ref[...], p)

    @pl.when(eb == n_eb - 1)
    def _():
        y = alpha * x_ref[0] + g2_ref[0] * acc_scr[...].T
        o_ref[0] = _ln(y) * lng_ref[...] + lnb_ref[...]


def _peer_dense(x, shift, scale, gate2, ln_g, ln_b, s_all, e_all, thr, u_tab, v_tab_t, alpha):
    b, l, d = x.shape
    t = min(256, l)
    n_exp = u_tab.shape[0]
    n_i1 = 8
    be = n_i1 * PEER_NKEYS
    n_eb = n_exp // be
    nh2 = 2 * PEER_HEADS
    big = pl.BlockSpec((1, nh2, PEER_NKEYS, t), lambda b_, i, e: (b_, 0, 0, i))
    vec = lambda b_, i, e: (0, 0)
    return pl.pallas_call(
        functools.partial(_peer_dense_kernel, n_i1=n_i1, n_eb=n_eb, alpha=alpha),
        grid=(b, l // t, n_eb),
        in_specs=[pl.BlockSpec((1, t, d), lambda b_, i, e: (b_, i, 0)),
                  pl.BlockSpec((1, 1, d), _bmap(shift.shape[0])),
                  pl.BlockSpec((1, 1, d), _bmap(scale.shape[0])),
                  pl.BlockSpec((1, 1, d), _bmap(gate2.shape[0])),
                  pl.BlockSpec((1, d), vec), pl.BlockSpec((1, d), vec),
                  big, big,
                  pl.BlockSpec((1, PEER_HEADS, t), lambda b_, i, e: (b_, 0, i)),
                  pl.BlockSpec((be, d), lambda b_, i, e: (e, 0)),
                  pl.BlockSpec((d, be), lambda b_, i, e: (0, e))],
        out_specs=pl.BlockSpec((1, t, d), lambda b_, i, e: (b_, i, 0)),
        out_shape=jax.ShapeDtypeStruct((b, l, d), F32),
        scratch_shapes=[pltpu.VMEM((t, d), BF16), pltpu.VMEM((d, t), F32)],
        compiler_params=_cp(("parallel", "parallel", "arbitrary")),
    )(x, shift, scale, gate2, ln_g, ln_b, s_all, e_all, thr, u_tab, v_tab_t)


def _prep_layer_weights(w_in, pool_scale, ssd_conv_w, ssd_conv_b, ssd_dt_bias, ssd_a_log, ssd_d, ssd_norm_w,
                        gdn_conv_w, gdn_dt_bias, gdn_a_log, gdn_norm_w, peer_wq, peer_keys, peer_u, peer_v):
    nl, d, _ = w_in.shape
    seg = lambda a, b_: w_in[:, :, a:b_]
    main = jnp.concatenate([seg(4640, 8736), seg(0, 512), seg(512, 1024), seg(1024, 2048), seg(2576, 4112),
                            seg(2048, 2560), seg(4112, 4624)], axis=-1).astype(BF16)
    small = jnp.concatenate([seg(2560, 2576), seg(4624, 4632), seg(4632, 4640)], axis=-1)
    w_sm = jnp.pad(small, ((0, 0), (0, 0), (0, 96))).astype(BF16)
    w_smt = jnp.swapaxes(small, 1, 2).astype(BF16)
    conv_s = jnp.pad(ssd_conv_w, ((0, 0), (0, 8 - CONV_K), (0, 0)))
    conv_g = jnp.pad(gdn_conv_w, ((0, 0), (0, 8 - CONV_K), (0, 0)))
    zeros = lambda n: jnp.zeros((nl, n), F32)
    bias_row = jnp.concatenate([ssd_dt_bias.reshape(nl, 16), zeros(112)], axis=1)
    alog_row = jnp.concatenate([ssd_a_log.reshape(nl, 16), zeros(112)], axis=1)
    galog_row = jnp.concatenate([zeros(24), gdn_a_log.reshape(nl, 8), zeros(96)], axis=1)
    gbias_row = jnp.concatenate([zeros(24), gdn_dt_bias.reshape(nl, 8), zeros(96)], axis=1)
    par = jnp.stack([bias_row, alog_row, galog_row, gbias_row] + [zeros(128)] * 4, axis=1)
    par_t = jnp.pad(jnp.swapaxes(par[:, 0:4, 0:32], 1, 2), ((0, 0), (0, 0), (0, 124)))
    dx = jnp.repeat(ssd_d, SSD_HEADDIM, axis=1).reshape(nl, 1, BRANCH)
    return dict(
        main=main, w_sm=w_sm, w_smt=w_smt, conv_s=conv_s, conv_sb=ssd_conv_b.reshape(nl, 1, -1),
        conv_g=conv_g, conv_gb=jnp.zeros((nl, 1, 3 * BRANCH), F32), par=par, par_t=par_t, dx=dx,
        pool_scale=pool_scale.reshape(nl, 1, BRANCH), snw=ssd_norm_w.reshape(nl, 1, BRANCH),
        gnw=jnp.tile(gdn_norm_w, (1, GDN_HEADS)).reshape(nl, 1, BRANCH),
        wqt=jnp.swapaxes(peer_wq, 1, 2).astype(BF16),
        keys=peer_keys.reshape(nl, 2 * PEER_HEADS, PEER_NKEYS, -1).astype(BF16),
        u=peer_u.astype(BF16), vt=jnp.swapaxes(peer_v, 1, 2).astype(BF16))


def _expand_table():
    lane = jnp.arange(128)[:, None]
    col = jnp.arange(BRANCH)[None, :] // SSD_HEADDIM
    return jnp.stack([(lane == col), (lane == col + SSD_HEADS)]).astype(F32)


def _scans(x, shift, scale, lw, l_idx, expand, states):
    proj = _lnmod_mm(x, shift, scale, lw['main'][l_idx], 512)
    small, small_t = _small_proj(x, shift, scale, lw['w_sm'][l_idx], lw['w_smt'][l_idx])
    conv_s = _conv_silu(proj, OFF_XBC, 1024, lw['conv_s'][l_idx], lw['conv_sb'][l_idx])
    conv_g = _conv_silu(proj, OFF_QKV, 1536, lw['conv_g'][l_idx], lw['conv_gb'][l_idx])
    yf, yb, s_ssd = _ssd(conv_s, small, small_t, lw['par'][l_idx], lw['par_t'][l_idx], lw['dx'][l_idx], expand,
                         states[0])
    of, ob, s_gdn = _gdn(conv_g, small, small_t, lw['par'][l_idx], lw['par_t'][l_idx], states[1])
    return proj, (yf, yb, of, ob), (s_ssd, s_gdn)


def _mixer(x, mods, lw, l_idx, expand, tabs, states, pool_w, w_branch, w_out, ln_g, ln_b, two_d, alpha):
    proj, (yf, yb, of, ob), new_states = _scans(x, mods[0], mods[1], lw, l_idx, expand, states)
    pool_o = _pool(proj, pool_w, lw['pool_scale'][l_idx], two_d)
    four_o = _fourier(proj, tabs)
    x = _merge(x, mods[2], ln_g, ln_b, proj, pool_o, four_o, yf, yb, of, ob, lw['snw'][l_idx], lw['gnw'][l_idx],
               w_branch, w_out, alpha)
    return x, new_states


def _peer(x, mods, lw, l_idx, ln_g, ln_b, alpha):
    s_all, e_all, thr = _peer_select(x, mods[3], mods[4], lw['wqt'][l_idx], lw['keys'][l_idx])
    return _peer_dense(x, mods[3], mods[4], mods[5], ln_g, ln_b, s_all, e_all, thr, lw['u'][l_idx],
                       lw['vt'][l_idx], alpha)


def kernel(x, c, ctx, c_ctx, w_mod, b_mod, w_in, pool_w, pool_scale, ssd_conv_w, ssd_conv_b, ssd_dt_bias, ssd_a_log, ssd_d, ssd_norm_w, gdn_conv_w, gdn_dt_bias, gdn_a_log, gdn_norm_w, w_branch, w_out, ln1_g, ln1_b, peer_wq, peer_keys, peer_u, peer_v, ln2_g, ln2_b):
    bsz, seq, d = x.shape
    nl = w_in.shape[0]
    alpha = (2 * nl) ** 0.25
    lw = _prep_layer_weights(w_in, pool_scale, ssd_conv_w, ssd_conv_b, ssd_dt_bias, ssd_a_log, ssd_d, ssd_norm_w,
                             gdn_conv_w, gdn_dt_bias, gdn_a_log, gdn_norm_w, peer_wq, peer_keys, peer_u, peer_v)
    pool_wb = pool_w.astype(BF16)
    w_branch_b = w_branch.astype(BF16)
    w_out_b = w_out.astype(BF16)
    expand = _expand_table()
    tabs_x = _dft_tables(seq)
    tabs_c = _dft_tables(ctx.shape[1])
    cs = jnp.concatenate([c, c_ctx[None], jnp.zeros((8 - bsz - 1, d), F32)], axis=0)
    mod_all = _mod_all(cs, w_mod.astype(BF16), b_mod).reshape(nl, 8, 6, 1, d)
    zero_states = (jnp.zeros((bsz, 2, SSD_STATE, BRANCH), F32), jnp.zeros((bsz, 2, GDN_HEADS, GDN_DK, GDN_DK), F32))
    vec = lambda a, l_idx: a[l_idx].reshape(1, d)
    for l_idx in range(nl):
        mods_x = [mod_all[l_idx, :bsz, i] for i in range(6)]
        mods_c = [mod_all[l_idx, bsz:bsz + 1, i] for i in range(6)]
        g1, b1, g2, b2 = vec(ln1_g, l_idx), vec(ln1_b, l_idx), vec(ln2_g, l_idx), vec(ln2_b, l_idx)
        if l_idx == nl - 1:
            _, _, states = _scans(ctx, mods_c[0], mods_c[1], lw, l_idx, expand, zero_states)
        else:
            ctx, states = _mixer(ctx, mods_c, lw, l_idx, expand, tabs_c, zero_states, pool_wb[l_idx],
                                 w_branch_b[l_idx], w_out_b[l_idx], g1, b1, False, alpha)
            ctx = _peer(ctx, mods_c, lw, l_idx, g2, b2, alpha)
        x, _ = _mixer(x, mods_x, lw, l_idx, expand, tabs_x, states, pool_wb[l_idx], w_branch_b[l_idx],
                      w_out_b[l_idx], g1, b1, True, alpha)
        x = _peer(x, mods_x, lw, l_idx, g2, b2, alpha)
    return x
```

```python
import functools
import math

import jax
import jax.numpy as jnp
from jax import lax
from jax.experimental import pallas as pl
from jax.experimental.pallas import tpu as pltpu

F32 = jnp.float32
BF16 = jnp.bfloat16
HIGHEST = lax.Precision.HIGHEST

LN_EPS = 1e-6
GRID_W = 64
BRANCH = 512
POOL_WINDOWS = (2, 4, 8, 16)
SSD_HEADS = 8
SSD_HEADDIM = 64
SSD_STATE = 128
SSD_CHUNK = 128
GDN_HEADS = 4
GDN_DK = 128
GDN_CHUNK = 64
CONV_K = 5
PEER_HEADS = 8
PEER_NKEYS = 128
PEER_TOPK = 16
N_MAIN = 8704
OFF_GATES, OFF_POOL, OFF_FOUR, OFF_XBC, OFF_QKV, OFF_SZ, OFF_GZ = 0, 4096, 4608, 5120, 6144, 7680, 8192
SM_DT, SM_BETA, SM_A = 0, 16, 24
VMEM_LIMIT = 52 * 1024 * 1024
NEG_INF = float("-inf")


def _cp(sem):
    return pltpu.CompilerParams(dimension_semantics=sem, vmem_limit_bytes=VMEM_LIMIT)


def _sigmoid(x):
    return 1.0 / (1.0 + jnp.exp(-x))


def _silu(x):
    return x * _sigmoid(x)


def _softplus(x):
    return jnp.maximum(x, 0.0) + jnp.log(1.0 + jnp.exp(-jnp.abs(x)))


def _ln(x):
    mu = jnp.mean(x, axis=-1, keepdims=True)
    xc = x - mu
    var = jnp.mean(xc * xc, axis=-1, keepdims=True)
    return xc * lax.rsqrt(var + LN_EPS)


def _dot(a, b, **kw):
    return jnp.dot(a, b, preferred_element_type=F32, **kw)


def _dot_nt(a, b, **kw):
    return lax.dot_general(a, b, (((1,), (1,)), ((), ())), preferred_element_type=F32, **kw)


def _dot3(a, b):
    a_hi = a.astype(BF16)
    b_hi = b.astype(BF16)
    a_lo = (a - a_hi.astype(F32)).astype(BF16)
    b_lo = (b - b_hi.astype(F32)).astype(BF16)
    return _dot(a_hi, b_hi) + (_dot(a_hi, b_lo) + _dot(a_lo, b_hi))


def _gelu(x):
    return 0.5 * x * (1.0 + lax.erf(x * (2.0 ** -0.5)))


def _mod_kernel(c_ref, w_ref, b_ref, o_ref):
    s = _silu(c_ref[...])
    o_ref[0] = _dot(s.astype(BF16), w_ref[0]) + b_ref[0]


def _mod_all(cs, w_mod, b_mod):
    nl, d, n6 = w_mod.shape
    tn = 1536
    return pl.pallas_call(
        _mod_kernel,
        grid=(nl, n6 // tn),
        in_specs=[pl.BlockSpec((8, d), lambda l, j: (0, 0)),
                  pl.BlockSpec((1, d, tn), lambda l, j: (l, 0, j)),
                  pl.BlockSpec((1, 1, tn), lambda l, j: (l, 0, j))],
        out_specs=pl.BlockSpec((1, 8, tn), lambda l, j: (l, 0, j)),
        out_shape=jax.ShapeDtypeStruct((nl, 8, n6), F32),
        compiler_params=_cp(("parallel", "parallel")),
        name="mod_vectors",
    )(cs, w_mod, b_mod.reshape(nl, 1, n6))


def _bmap(bs):
    return (lambda b, *_: (b, 0, 0)) if bs > 1 else (lambda b, *_: (0, 0, 0))


def _lnmod_mm_kernel(x_ref, sh_ref, sc_ref, w_ref, o_ref, h_scr):
    @pl.when(pl.program_id(2) == 0)
    def _():
        h = _ln(x_ref[0]) * (1.0 + sc_ref[0]) + sh_ref[0]
        h_scr[...] = h.astype(BF16)

    o_ref[0] = _dot(h_scr[...], w_ref[...])


def _lnmod_mm(x, shift, scale, w, tn):
    b, l, d = x.shape
    n = w.shape[1]
    tm = min(1024, l)
    return pl.pallas_call(
        _lnmod_mm_kernel,
        grid=(b, l // tm, n // tn),
        in_specs=[pl.BlockSpec((1, tm, d), lambda b_, i, j: (b_, i, 0)),
                  pl.BlockSpec((1, 1, d), _bmap(shift.shape[0])),
                  pl.BlockSpec((1, 1, d), _bmap(scale.shape[0])),
                  pl.BlockSpec((d, tn), lambda b_, i, j: (0, j))],
        out_specs=pl.BlockSpec((1, tm, tn), lambda b_, i, j: (b_, i, j)),
        out_shape=jax.ShapeDtypeStruct((b, l, n), F32),
        scratch_shapes=[pltpu.VMEM((tm, d), BF16)],
        compiler_params=_cp(("parallel", "parallel", "arbitrary")),
        name="in_proj",
    )(x, shift, scale, w)


def _small_kernel(x_ref, sh_ref, sc_ref, w_ref, wt_ref, o_ref, ot_ref):
    h = (_ln(x_ref[0]) * (1.0 + sc_ref[0]) + sh_ref[0]).astype(BF16)
    o_ref[0] = _dot(h, w_ref[...])
    ot_ref[0] = _dot_nt(wt_ref[...], h)


def _small_proj(x, shift, scale, w_sm, w_smt):
    b, l, d = x.shape
    tm = min(256, l)
    return pl.pallas_call(
        _small_kernel,
        grid=(b, l // tm),
        in_specs=[pl.BlockSpec((1, tm, d), lambda b_, i: (b_, i, 0)),
                  pl.BlockSpec((1, 1, d), _bmap(shift.shape[0])),
                  pl.BlockSpec((1, 1, d), _bmap(scale.shape[0])),
                  pl.BlockSpec((d, 128), lambda b_, i: (0, 0)),
                  pl.BlockSpec((32, d), lambda b_, i: (0, 0))],
        out_specs=[pl.BlockSpec((1, tm, 128), lambda b_, i: (b_, i, 0)),
                   pl.BlockSpec((1, 32, tm), lambda b_, i: (b_, 0, i))],
        out_shape=[jax.ShapeDtypeStruct((b, l, 128), F32), jax.ShapeDtypeStruct((b, 32, l), F32)],
        compiler_params=_cp(("parallel", "parallel")),
        name="small_proj",
    )(x, shift, scale, w_sm, w_smt)


def _conv_kernel(x_ref, w_ref, b_ref, o_ref, pad_scr, *, l, rc):
    tc = x_ref.shape[2]
    zeros8 = jnp.zeros((8, tc), F32)
    pad_scr[0:8, :] = zeros8
    pad_scr[l + 8:l + 16, :] = zeros8
    pad_scr[8:l + 8, :] = x_ref[0]
    half = CONV_K // 2
    for r0 in range(0, l, rc):
        acc = b_ref[...] + w_ref[0:1, :] * pad_scr[r0 + 8 - half:r0 + 8 - half + rc, :]
        for k in range(1, CONV_K):
            acc = acc + w_ref[k:k + 1, :] * pad_scr[r0 + 8 - half + k:r0 + 8 - half + k + rc, :]
        o_ref[0, r0:r0 + rc, :] = _silu(acc)


def _conv_silu(proj, col_off, width, w, bias):
    b, l, _ = proj.shape
    tc = 256
    cb0 = col_off // tc
    rc = min(512, l)
    return pl.pallas_call(
        functools.partial(_conv_kernel, l=l, rc=rc),
        grid=(b, width // tc),
        in_specs=[pl.BlockSpec((1, l, tc), lambda b_, j: (b_, 0, cb0 + j)),
                  pl.BlockSpec((8, tc), lambda b_, j: (0, j)),
                  pl.BlockSpec((1, tc), lambda b_, j: (0, j))],
        out_specs=pl.BlockSpec((1, l, tc), lambda b_, j: (b_, 0, j)),
        out_shape=jax.ShapeDtypeStruct((b, l, width), F32),
        scratch_shapes=[pltpu.VMEM((l + 16, tc), F32)],
        compiler_params=_cp(("parallel", "parallel")),
        name="conv_silu",
    )(proj, w, bias)


def _tri(n, lower):
    ii = lax.broadcasted_iota(jnp.int32, (n, n), 0)
    jj = lax.broadcasted_iota(jnp.int32, (n, n), 1)
    return (jj <= ii) if lower else (jj >= ii)


def _ssd_kernel(xf_ref, xb_ref, smf_ref, smb_ref, smtf_ref, smtb_ref, par_ref, part_ref, dx_ref, e_ref, s0_ref,
                yf_ref, yb_ref, sfin_ref, s_scr, *, nc):
    c = pl.program_id(1)

    @pl.when(c == 0)
    def _():
        s_scr[...] = s0_ref[0]

    q = SSD_CHUNK
    low = _tri(q, True)
    upp = _tri(q, False)
    lowf = low.astype(F32)
    uppf = upp.astype(F32)
    lane = lax.broadcasted_iota(jnp.int32, (1, 256), 1) // SSD_HEADDIM
    dirs = ((xf_ref, smf_ref, smtf_ref, yf_ref), (xb_ref, smb_ref, smtb_ref, yb_ref))
    for d, (x_ref, sm_ref, smt_ref, y_ref) in enumerate(dirs):
        mask = low if d == 0 else upp
        xbc = x_ref[0]
        xs = xbc[:, 0:512]
        dt = _softplus(sm_ref[0] + par_ref[0:1, :])
        la = dt * (-jnp.exp(par_ref[1:2, :]))
        cum = _dot(lowf if d == 0 else uppf, la, precision=HIGHEST)
        dtt = _softplus(smt_ref[0, 8 * d:8 * d + 8, :] + part_ref[8 * d:8 * d + 8, 0:1])
        lat = dtt * (-jnp.exp(part_ref[8 * d:8 * d + 8, 1:2]))
        cumt = _dot(lat, uppf if d == 0 else lowf, precision=HIGHEST)
        expand = e_ref[d]
        dtx = _dot(dt, expand, precision=HIGHEST)
        cumx = _dot(cum, expand, precision=HIGHEST)
        tot = cumx[q - 1:q, :] if d == 0 else cumx[0:1, :]
        xdt = xs * dtx
        ecum = jnp.exp(cumx)
        xw = (xdt * jnp.exp(tot - cumx)).astype(BF16)
        etot = jnp.exp(tot)
        ys = []
        for g in range(2):
            bg = xbc[:, 512 + 128 * g:640 + 128 * g]
            cg = xbc[:, 768 + 128 * g:896 + 128 * g].astype(BF16)
            sl = slice(256 * g, 256 * g + 256)
            s_old = s_scr[d, :, sl]
            scores = _dot_nt(cg, bg.astype(BF16))
            y = _dot(cg, s_old.astype(BF16)) * ecum[:, sl]
            xg = xdt[:, sl]
            for hh in range(4):
                h = 4 * g + hh
                col = cum[:, 8 * d + h:8 * d + h + 1]
                row = cumt[h:h + 1, :]
                dec = jnp.exp(jnp.where(mask, col - row, NEG_INF))
                xm = jnp.where(lane == hh, xg, 0.0).astype(BF16)
                y = y + _dot((scores * dec).astype(BF16), xm)
            ys.append(y)
            s_scr[d, :, sl] = s_old * etot[:, sl] + _dot(bg.T.astype(BF16), xw[:, sl])
        y = jnp.concatenate(ys, axis=1)
        if d == 0:
            y = y + dx_ref[...] * xs
        y_ref[0] = y

    @pl.when(c == nc - 1)
    def _():
        sfin_ref[0] = s_scr[...]


def _ssd(conv_x, small, small_t, par, par_t, dx, expand, s0):
    b, l, _ = conv_x.shape
    q = SSD_CHUNK
    nc = l // q
    fw = lambda b_, c: (b_, c, 0)
    bw = lambda b_, c: (b_, nc - 1 - c, 0)
    fwt = lambda b_, c: (b_, 0, c)
    bwt = lambda b_, c: (b_, 0, nc - 1 - c)
    full2 = lambda b_, c: (0, 0)
    return pl.pallas_call(
        functools.partial(_ssd_kernel, nc=nc),
        grid=(b, nc),
        in_specs=[pl.BlockSpec((1, q, 1024), fw), pl.BlockSpec((1, q, 1024), bw),
                  pl.BlockSpec((1, q, 128), fw), pl.BlockSpec((1, q, 128), bw),
                  pl.BlockSpec((1, 32, q), fwt), pl.BlockSpec((1, 32, q), bwt),
                  pl.BlockSpec((8, 128), full2), pl.BlockSpec((32, 128), full2),
                  pl.BlockSpec((1, 512), full2),
                  pl.BlockSpec((2, 128, 512), lambda b_, c: (0, 0, 0)),
                  pl.BlockSpec((1, 2, 128, 512), lambda b_, c: (b_, 0, 0, 0))],
        out_specs=[pl.BlockSpec((1, q, 512), fw), pl.BlockSpec((1, q, 512), bw),
                   pl.BlockSpec((1, 2, 128, 512), lambda b_, c: (b_, 0, 0, 0))],
        out_shape=[jax.ShapeDtypeStruct((b, l, 512), F32), jax.ShapeDtypeStruct((b, l, 512), F32),
                   jax.ShapeDtypeStruct((b, 2, 128, 512), F32)],
        scratch_shapes=[pltpu.VMEM((2, 128, 512), F32)],
        compiler_params=_cp(("parallel", "arbitrary")),
        name="ssd_scan",
    )(conv_x, conv_x, small, small, small_t, small_t, par, par_t, dx, expand, s0)


def _unit_tri_inverse(a, eye):
    t = eye - a
    p = a
    for _ in range(5):
        pb = p.astype(BF16)
        p = _dot(pb, pb)
        t = t + _dot(t.astype(BF16), p.astype(BF16))
    return t


def _gdn_kernel(qf_ref, kf_ref, vf_ref, qb_ref, kb_ref, vb_ref, smf_ref, smb_ref, smtf_ref, smtb_ref,
                par_ref, part_ref, s0_ref, of_ref, ob_ref, sfin_ref, s_scr, *, nc):
    c = pl.program_id(1)

    @pl.when(c == 0)
    def _():
        s_scr[...] = s0_ref[0]

    cl = GDN_CHUNK
    nh = GDN_HEADS
    pk = nh * cl
    rows = 2 * cl
    ii = lax.broadcasted_iota(jnp.int32, (pk, pk), 0)
    jj = lax.broadcasted_iota(jnp.int32, (pk, pk), 1)
    same = (ii // cl) == (jj // cl)
    eye = (ii == jj).astype(F32)
    ri = lax.broadcasted_iota(jnp.int32, (rows, rows), 0)
    rj = lax.broadcasted_iota(jnp.int32, (rows, rows), 1)
    rsame = (ri // cl) == (rj // cl)
    zpad = jnp.zeros((cl, GDN_DK), F32)
    dirs = ((qf_ref, kf_ref, vf_ref, smf_ref, smtf_ref, of_ref), (qb_ref, kb_ref, vb_ref, smb_ref, smtb_ref, ob_ref))
    for d, (q_ref, k_ref, v_ref, sm_ref, smt_ref, o_ref) in enumerate(dirs):
        incl = same & ((jj <= ii) if d == 0 else (jj >= ii))
        strict = same & ((jj < ii) if d == 0 else (jj > ii))
        tri = (rsame & ((rj <= ri) if d == 0 else (rj >= ri))).astype(F32)
        tri_t = (rsame & ((ri <= rj) if d == 0 else (ri >= rj))).astype(F32)
        sm = sm_ref[0]
        beta_all = _sigmoid(sm)
        g_all = -jnp.exp(par_ref[2:3, :]) * _softplus(sm + par_ref[3:4, :])
        gt8 = -jnp.exp(part_ref[24:32, 2:3]) * _softplus(smt_ref[0, 24:32, :] + part_ref[24:32, 3:4])
        cum_all = _dot(tri, g_all, precision=HIGHEST)
        cumt_all = _dot(gt8, tri_t, precision=HIGHEST)
        for sc in ((0, 1) if d == 0 else (1, 0)):
            r0 = cl * sc
            rs = slice(r0, r0 + cl)
            heads = range(nh)
            stack = lambda ref: jnp.concatenate([ref[0, rs, GDN_DK * h:GDN_DK * h + GDN_DK] for h in heads], axis=0)
            qp, kp, vp = stack(q_ref), stack(k_ref), stack(v_ref)
            qp = qp * lax.rsqrt(jnp.sum(qp * qp, axis=-1, keepdims=True) + 1e-6) * (GDN_DK ** -0.5)
            kp = kp * lax.rsqrt(jnp.sum(kp * kp, axis=-1, keepdims=True) + 1e-6)
            col = jnp.concatenate([cum_all[rs, SM_A + 4 * d + h:SM_A + 4 * d + h + 1] for h in heads], axis=0)
            row = jnp.concatenate([cumt_all[4 * d + h:4 * d + h + 1, rs] for h in heads], axis=1)
            beta = jnp.concatenate([beta_all[rs, SM_BETA + 4 * d + h:SM_BETA + 4 * d + h + 1] for h in heads], axis=0)
            edge = cl - 1 if d == 0 else 0
            tot = jnp.concatenate([jnp.broadcast_to(col[cl * h + edge:cl * h + edge + 1, :], (cl, 1)) for h in heads],
                                  axis=0)
            dec = jnp.exp(jnp.where(incl, col - row, NEG_INF))
            kb = kp.astype(BF16)
            qb = qp.astype(BF16)
            a_mat = jnp.where(strict, _dot_nt(kb, kb) * dec * beta, 0.0)
            t_inv = _unit_tri_inverse(a_mat, eye)
            ecol = jnp.exp(col)
            uw = _dot3(t_inv, jnp.concatenate([vp * beta, kp * (beta * ecol)], axis=1))
            attn = (_dot_nt(qb, kb) * dec).astype(BF16)
            qg = qp * ecol
            kdec = kp * jnp.exp(tot - col)
            etot = jnp.exp(tot)
            vnew, qs = [], []
            for h in heads:
                hr = slice(cl * h, cl * h + cl)
                wq = jnp.concatenate([uw[hr, GDN_DK:], qg[hr]], axis=0).astype(BF16)
                ws_qs = _dot(wq, s_scr[d, h].astype(BF16))
                vnew.append(uw[hr, :GDN_DK] - ws_qs[0:cl])
                qs.append(ws_qs[cl:])
            vn = jnp.concatenate(vnew, axis=0)
            out = jnp.concatenate(qs, axis=0) + _dot(attn, vn.astype(BF16))
            for h in heads:
                hr = slice(cl * h, cl * h + cl)
                o_ref[0, rs, GDN_DK * h:GDN_DK * h + GDN_DK] = out[hr]
                kd_t = jnp.concatenate([kdec[hr], zpad], axis=0).T.astype(BF16)
                vn_pad = jnp.concatenate([vnew[h], zpad], axis=0).astype(BF16)
                s_scr[d, h] = s_scr[d, h] * etot[cl * h:cl * h + 1, :] + _dot(kd_t, vn_pad)

    @pl.when(c == nc - 1)
    def _():
        sfin_ref[0] = s_scr[...]


def _gdn(conv_g, small, small_t, par, par_t, s0):
    b, l, _ = conv_g.shape
    rows = 2 * GDN_CHUNK
    nc = l // rows
    fwm = lambda j: (lambda b_, c: (b_, c, j))
    bwm = lambda j: (lambda b_, c: (b_, nc - 1 - c, j))
    fw, bw = fwm(0), bwm(0)
    fwt = lambda b_, c: (b_, 0, c)
    bwt = lambda b_, c: (b_, 0, nc - 1 - c)
    full2 = lambda b_, c: (0, 0)
    st = lambda b_, c: (b_, 0, 0, 0, 0)
    return pl.pallas_call(
        functools.partial(_gdn_kernel, nc=nc),
        grid=(b, nc),
        in_specs=[pl.BlockSpec((1, rows, 512), fwm(0)), pl.BlockSpec((1, rows, 512), fwm(1)),
                  pl.BlockSpec((1, rows, 512), fwm(2)),
                  pl.BlockSpec((1, rows, 512), bwm(0)), pl.BlockSpec((1, rows, 512), bwm(1)),
                  pl.BlockSpec((1, rows, 512), bwm(2)),
                  pl.BlockSpec((1, rows, 128), fw), pl.BlockSpec((1, rows, 128), bw),
                  pl.BlockSpec((1, 32, rows), fwt), pl.BlockSpec((1, 32, rows), bwt),
                  pl.BlockSpec((8, 128), full2), pl.BlockSpec((32, 128), full2),
                  pl.BlockSpec((1, 2, GDN_HEADS, 128, 128), st)],
        out_specs=[pl.BlockSpec((1, rows, 512), fw), pl.BlockSpec((1, rows, 512), bw),
                   pl.BlockSpec((1, 2, GDN_HEADS, 128, 128), st)],
        out_shape=[jax.ShapeDtypeStruct((b, l, 512), F32), jax.ShapeDtypeStruct((b, l, 512), F32),
                   jax.ShapeDtypeStruct((b, 2, GDN_HEADS, 128, 128), F32)],
        scratch_shapes=[pltpu.VMEM((2, GDN_HEADS, 128, 128), F32)],
        compiler_params=_cp(("parallel", "arbitrary")),
        name="gdn_scan",
    )(conv_g, conv_g, conv_g, conv_g, conv_g, conv_g, small, small, small_t, small_t, par, par_t, s0)


def _pool_body(x_ref, w_ref, sc_ref, o_ref, pa, pb, *, l, gw, two_d, win, rc):
    lo = win // 2
    hi = win - 1 - lo
    nrow = l // gw
    shift = int(math.log2(gw))
    pbw = 8 * gw
    z8 = jnp.zeros((8, 128), F32)
    pa[0:8, :] = z8
    pa[l + 8:l + 16, :] = z8
    pa[8:l + 8, :] = x_ref[0]
    if two_d:
        zb = jnp.zeros((pbw, 128), F32)
        pb[0:pbw, :] = zb
        pb[pbw + l:pbw + l + pbw, :] = zb

    def finish(r0, acc, pos):
        col = pos & (gw - 1)
        cnt = (jnp.minimum(col + hi, gw - 1) - jnp.maximum(col - lo, 0) + 1).astype(F32)
        if two_d:
            row = pos >> shift
            cnt = cnt * (jnp.minimum(row + hi, nrow - 1) - jnp.maximum(row - lo, 0) + 1).astype(F32)
        dlt = (acc / cnt - x_ref[0, r0:r0 + rc, :]).astype(BF16)
        o_ref[0, r0:r0 + rc, :] = _dot(dlt, w_ref[0]) * sc_ref[...]

    for r0 in range(0, l, rc):
        pos = lax.broadcasted_iota(jnp.int32, (rc, 128), 0) + r0
        col = pos & (gw - 1)
        acc = pa[8 + r0:8 + r0 + rc, :]
        for j in range(-lo, hi + 1):
            if j == 0:
                continue
            v = pa[8 + r0 + j:8 + r0 + j + rc, :]
            ok = (col + j >= 0) if j < 0 else (col + j < gw)
            acc = acc + jnp.where(ok, v, 0.0)
        if two_d:
            pb[pbw + r0:pbw + r0 + rc, :] = acc
        else:
            finish(r0, acc, pos)
    if two_d:
        for r0 in range(0, l, rc):
            pos = lax.broadcasted_iota(jnp.int32, (rc, 128), 0) + r0
            acc = pb[pbw + r0:pbw + r0 + rc, :]
            for j in range(-lo, hi + 1):
                if j != 0:
                    acc = acc + pb[pbw + r0 + gw * j:pbw + r0 + gw * j + rc, :]
            finish(r0, acc, pos)


def _pool_kernel(x_ref, w_ref, sc_ref, o_ref, pa, pb, **kw):
    g = pl.program_id(1)
    for gi, win in enumerate(POOL_WINDOWS):
        @pl.when(g == gi)
        def _(win=win):
            _pool_body(x_ref, w_ref, sc_ref, o_ref, pa, pb, win=win, **kw)


def _pool(proj, pool_w, pool_scale, two_d):
    b, l, _ = proj.shape
    gw = GRID_W if two_d else l
    assert gw & (gw - 1) == 0 and l % gw == 0
    rc = min(512, l)
    cb0 = OFF_POOL // 128
    pb_rows = l + 16 * gw if two_d else 8
    return pl.pallas_call(
        functools.partial(_pool_kernel, l=l, gw=gw, two_d=two_d, rc=rc),
        grid=(b, len(POOL_WINDOWS)),
        in_specs=[pl.BlockSpec((1, l, 128), lambda b_, g: (b_, 0, cb0 + g)),
                  pl.BlockSpec((1, 128, 128), lambda b_, g: (g, 0, 0)),
                  pl.BlockSpec((1, 128), lambda b_, g: (0, g))],
        out_specs=pl.BlockSpec((1, l, 128), lambda b_, g: (b_, 0, g)),
        out_shape=jax.ShapeDtypeStruct((b, l, BRANCH), F32),
        scratch_shapes=[pltpu.VMEM((l + 16, 128), F32), pltpu.VMEM((pb_rows, 128), F32)],
        compiler_params=_cp(("parallel", "parallel")),
        name="pool_branch",
    )(proj, pool_w, pool_scale)


def _four1_kernel(x_ref, cc_ref, ss_ref, xc_ref, xs_ref):
    x = x_ref[0]
    xc_ref[0] = _dot(x, cc_ref[...], precision=HIGHEST).astype(BF16)
    xs_ref[0] = _dot(x, ss_ref[...], precision=HIGHEST).astype(BF16)


def _four2_kernel(ct_ref, st_ref, xc_ref, xs_ref, o_ref):
    o_ref[0] = _dot(ct_ref[...], xc_ref[0]) + _dot(st_ref[...], xs_ref[0])


def _fourier(proj, tabs):
    cc, ss, cl_, sl_ = tabs
    b, l, _ = proj.shape
    tm = min(512, l)
    cb = OFF_FOUR // BRANCH
    xc, xs = pl.pallas_call(
        _four1_kernel,
        grid=(b, l // tm),
        in_specs=[pl.BlockSpec((1, tm, BRANCH), lambda b_, i: (b_, i, cb)),
                  pl.BlockSpec((BRANCH, BRANCH), lambda b_, i: (0, 0)),
                  pl.BlockSpec((BRANCH, BRANCH), lambda b_, i: (0, 0))],
        out_specs=[pl.BlockSpec((1, tm, BRANCH), lambda b_, i: (b_, i, 0))] * 2,
        out_shape=[jax.ShapeDtypeStruct((b, l, BRANCH), BF16)] * 2,
        compiler_params=_cp(("parallel", "parallel")),
        name="dft_channels",
    )(proj, cc, ss)
    tr = min(256, l)
    return pl.pallas_call(
        _four2_kernel,
        grid=(b, l // tr),
        in_specs=[pl.BlockSpec((tr, l), lambda b_, i: (i, 0)),
                  pl.BlockSpec((tr, l), lambda b_, i: (i, 0)),
                  pl.BlockSpec((1, l, BRANCH), lambda b_, i: (b_, 0, 0)),
                  pl.BlockSpec((1, l, BRANCH), lambda b_, i: (b_, 0, 0))],
        out_specs=pl.BlockSpec((1, tr, BRANCH), lambda b_, i: (b_, i, 0)),
        out_shape=jax.ShapeDtypeStruct((b, l, BRANCH), F32),
        compiler_params=_cp(("parallel", "parallel")),
        name="dft_positions",
    )(cl_, sl_, xc, xs)


def _dft_tables(l):
    c = BRANCH // 4
    k = jnp.arange(c, dtype=jnp.int32)
    ang = (2.0 * math.pi / c) * ((k[:, None] * k[None, :]) % c).astype(F32)
    scale = 1.0 / math.sqrt(l * c)
    eye4 = jnp.eye(4, dtype=F32)
    cc = jnp.kron(eye4, jnp.cos(ang) * scale)
    ss = jnp.kron(eye4, jnp.sin(ang) * scale)
    t = jnp.arange(l, dtype=jnp.int32)
    angl = (2.0 * math.pi / l) * ((t[:, None] * t[None, :]) % l).astype(F32)
    return cc, ss, jnp.cos(angl).astype(BF16), (-jnp.sin(angl)).astype(BF16)


def _merge_kernel(x_ref, g1_ref, lng_ref, lnb_ref, gates_ref, pool_ref, four_ref, yf_ref, yb_ref, sz_ref,
                  of_ref, ob_ref, gz_ref, snw_ref, gnw_ref, wbr_ref, wout_ref, o_ref, *, alpha):
    d = x_ref.shape[2]
    ssd_g = (yf_ref[0] + yb_ref[0]) * _silu(sz_ref[0])
    parts = []
    for g in range(2):
        blk = ssd_g[:, 256 * g:256 * g + 256]
        parts.append(blk * lax.rsqrt(jnp.mean(blk * blk, axis=-1, keepdims=True) + LN_EPS))
    ssd_out = jnp.concatenate(parts, axis=1) * snw_ref[...]
    o = of_ref[0] + ob_ref[0]
    parts = []
    for h in range(GDN_HEADS):
        blk = o[:, 128 * h:128 * h + 128]
        parts.append(blk * lax.rsqrt(jnp.mean(blk * blk, axis=-1, keepdims=True) + LN_EPS))
    gdn_out = jnp.concatenate(parts, axis=1) * gnw_ref[...] * _silu(gz_ref[0])
    merged = None
    for i, br in enumerate((pool_ref[0], four_ref[0], ssd_out, gdn_out)):
        term = _sigmoid(gates_ref[0, :, d * i:d * i + d]) * _dot(br.astype(BF16), wbr_ref[i])
        merged = term if merged is None else merged + term
    mix = _dot(merged.astype(BF16), wout_ref[...])
    y = alpha * x_ref[0] + g1_ref[0] * mix
    o_ref[0] = _ln(y) * lng_ref[...] + lnb_ref[...]


def _merge(x, gate1, ln_g, ln_b, proj, pool_o, four_o, yf, yb, of, ob, snw, gnw, wbr, wout, alpha):
    b, l, d = x.shape
    tm = min(256, l)
    row = lambda b_, i: (b_, i, 0)
    colb = lambda j: (lambda b_, i: (b_, i, j))
    vec = lambda b_, i: (0, 0)
    br = pl.BlockSpec((1, tm, BRANCH), row)
    return pl.pallas_call(
        functools.partial(_merge_kernel, alpha=alpha),
        grid=(b, l // tm),
        in_specs=[pl.BlockSpec((1, tm, d), row),
                  pl.BlockSpec((1, 1, d), _bmap(gate1.shape[0])),
                  pl.BlockSpec((1, d), vec), pl.BlockSpec((1, d), vec),
                  pl.BlockSpec((1, tm, 4 * d), colb(OFF_GATES // (4 * d))),
                  br, br, br, br,
                  pl.BlockSpec((1, tm, BRANCH), colb(OFF_SZ // BRANCH)),
                  br, br,
                  pl.BlockSpec((1, tm, BRANCH), colb(OFF_GZ // BRANCH)),
                  pl.BlockSpec((1, BRANCH), vec), pl.BlockSpec((1, BRANCH), vec),
                  pl.BlockSpec((4, BRANCH, d), lambda b_, i: (0, 0, 0)),
                  pl.BlockSpec((d, d), vec)],
        out_specs=pl.BlockSpec((1, tm, d), row),
        out_shape=jax.ShapeDtypeStruct((b, l, d), F32),
        compiler_params=_cp(("parallel", "parallel")),
        name="branch_merge",
    )(x, gate1, ln_g, ln_b, proj, pool_o, four_o, yf, yb, proj, of, ob, proj, snw, gnw, wbr, wout)


def _sort_network(n):
    pairs = []

    def merge(lo, hi, r):
        step = r * 2
        if step < hi - lo:
            merge(lo, hi, step)
            merge(lo + r, hi, step)
            pairs.extend((i, i + r) for i in range(lo + r, hi - r, step))
        else:
            pairs.append((lo, lo + r))

    def sort(lo, hi):
        if hi - lo >= 1:
            mid = lo + (hi - lo) // 2
            sort(lo, mid)
            sort(mid + 1, hi)
            merge(lo, hi, 1)

    sort(0, n - 1)
    return pairs


def _top_desc(x, n):
    nrow = x.shape[0] // 8
    lst = [x[8 * r:8 * r + 8, :] for r in range(nrow)]
    npad = 1 << (nrow - 1).bit_length()
    lst += [jnp.full(lst[0].shape, NEG_INF, F32)] * (npad - nrow)
    for i, j in _sort_network(npad):
        lst[i], lst[j] = jnp.maximum(lst[i], lst[j]), jnp.minimum(lst[i], lst[j])
    lst = lst[:nrow]
    sub = lax.broadcasted_iota(jnp.int32, lst[0].shape, 0).astype(F32)
    out = []
    for r in range(n):
        m = jnp.max(lst[0], axis=0, keepdims=True)
        out.append(m)
        left = n - r - 1
        if left:
            win = jnp.min(jnp.where(lst[0] == m, sub, 8.0), axis=0, keepdims=True)
            winner = sub == win
            for j in range(min(nrow, left)):
                nxt = lst[j + 1] if j + 1 < nrow else NEG_INF
                lst[j] = jnp.where(winner, nxt, lst[j])
    return out


def _peer_kernel(x_ref, sh_ref, sc_ref, g2_ref, lng_ref, lnb_ref, wqt_ref, keys_ref, u_ref, vt_ref, o_ref,
                 h_scr, qt_scr, th_scr, e1_scr, s2_scr, e2_scr, cand_scr, act_scr, p_scr, acc_scr,
                 *, n_pair, n_eb, alpha):
    eb = pl.program_id(2)
    t = x_ref.shape[1]
    ntc = t // 128
    k1 = PEER_TOPK + 1
    pairs = [(a, b) for a in range(k1) for b in range(k1) if (a + 1) * (b + 1) <= k1]

    @pl.when(eb == 0)
    def _():
        h = (_ln(x_ref[0]) * (1.0 + sc_ref[0]) + sh_ref[0]).astype(BF16)
        h_scr[...] = h
        qt_scr[...] = _dot_nt(wqt_ref[...], h).astype(BF16)
        acc_scr[...] = jnp.zeros(acc_scr.shape, F32)
        for r in range(len(pairs), cand_scr.shape[0]):
            cand_scr[r:r + 1, :] = jnp.full((1, 128), NEG_INF, F32)

        def head_body(hd, carry):
            for tc in range(ntc):
                ts = slice(128 * tc, 128 * tc + 128)
                sv, tops = [], []
                for s in range(2):
                    hs = 2 * hd + s
                    q_rows = qt_scr[pl.ds(pl.multiple_of(hs * 128, 128), 128), ts]
                    st = _dot(keys_ref[hs], q_rows)
                    sv.append(st)
                    tops.append(_top_desc(st, k1))
                v1, v2 = tops
                for r, (a, b) in enumerate(pairs):
                    cand_scr[r:r + 1, :] = v1[a] + v2[b]
                best = _top_desc(cand_scr[...], k1)
                z = None
                for kk in range(PEER_TOPK):
                    term = jnp.exp(best[kk] - best[0])
                    z = term if z is None else z + term
                thr = 0.5 * (best[PEER_TOPK - 1] + best[PEER_TOPK])
                th_scr[hd, tc] = thr - sv[0]
                e1_scr[hd, tc] = jnp.exp(sv[0] - v1[0])
                s2_scr[hd, tc] = sv[1]
                e2_scr[hd, tc] = jnp.exp(sv[1] - v2[0]) / z
            return carry

        lax.fori_loop(0, PEER_HEADS, head_body, 0)

    pr = 2 * PEER_NKEYS
    for k in range(n_pair):
        act = _gelu(_dot_nt(u_ref[pr * k:pr * k + pr, :], h_scr[...]))
        for tc in range(ntc):
            act_scr[k % 2, tc] = act[:, 128 * tc:128 * tc + 128]
        if k:
            acc_scr[...] += _dot(vt_ref[k - 1], p_scr[(k - 1) % 2])
        for j in range(2):
            i1 = eb * (2 * n_pair) + 2 * k + j
            js = slice(PEER_NKEYS * j, PEER_NKEYS * j + PEER_NKEYS)
            for tc in range(ntc):
                w = None
                for hd in range(PEER_HEADS):
                    th = th_scr[hd, tc, pl.ds(i1, 1), :]
                    e1 = e1_scr[hd, tc, pl.ds(i1, 1), :]
                    term = jnp.where(s2_scr[hd, tc] > th, e2_scr[hd, tc] * e1, 0.0)
                    w = term if w is None else w + term
                p_scr[k % 2, js, 128 * tc:128 * tc + 128] = (w * act_scr[k % 2, tc, js, :]).astype(BF16)
    acc_scr[...] += _dot(vt_ref[n_pair - 1], p_scr[(n_pair - 1) % 2])

    @pl.when(eb == n_eb - 1)
    def _():
        y = alpha * x_ref[0] + g2_ref[0] * acc_scr[...].T
        o_ref[0] = _ln(y) * lng_ref[...] + lnb_ref[...]


def _peer_ffn(x, shift, scale, gate2, ln_g, ln_b, wqt, keys, u_tab, v_tab_t, alpha):
    b, l, d = x.shape
    t = min(512, l)
    n_exp = u_tab.shape[0]
    n_pair = 4
    pr = 2 * PEER_NKEYS
    be = n_pair * pr
    n_eb = n_exp // be
    nh2 = 2 * PEER_HEADS
    k1 = PEER_TOPK + 1
    npairs = sum(1 for a in range(k1) for b_ in range(k1) if (a + 1) * (b_ + 1) <= k1)
    ncand = -(-npairs // 8) * 8
    vec = lambda b_, i, e: (0, 0)
    per_head = pltpu.VMEM((PEER_HEADS, t // 128, PEER_NKEYS, 128), F32)
    return pl.pallas_call(
        functools.partial(_peer_kernel, n_pair=n_pair, n_eb=n_eb, alpha=alpha),
        grid=(b, l // t, n_eb),
        in_specs=[pl.BlockSpec((1, t, d), lambda b_, i, e: (b_, i, 0)),
                  pl.BlockSpec((1, 1, d), _bmap(shift.shape[0])),
                  pl.BlockSpec((1, 1, d), _bmap(scale.shape[0])),
                  pl.BlockSpec((1, 1, d), _bmap(gate2.shape[0])),
                  pl.BlockSpec((1, d), vec), pl.BlockSpec((1, d), vec),
                  pl.BlockSpec((nh2 * 128, d), vec),
                  pl.BlockSpec((nh2, PEER_NKEYS, 128), lambda b_, i, e: (0, 0, 0)),
                  pl.BlockSpec((be, d), lambda b_, i, e: (e, 0)),
                  pl.BlockSpec((n_pair, d, pr), lambda b_, i, e: (e, 0, 0))],
        out_specs=pl.BlockSpec((1, t, d), lambda b_, i, e: (b_, i, 0)),
        out_shape=jax.ShapeDtypeStruct((b, l, d), F32),
        scratch_shapes=[pltpu.VMEM((t, d), BF16), pltpu.VMEM((nh2 * 128, t), BF16),
                        per_head, per_head, per_head, per_head,
                        pltpu.VMEM((ncand, 128), F32), pltpu.VMEM((2, t // 128, pr, 128), F32),
                        pltpu.VMEM((2, pr, t), BF16),
                        pltpu.VMEM((d, t), F32)],
        compiler_params=_cp(("parallel", "parallel", "arbitrary")),
        name="peer_ffn",
    )(x, shift, scale, gate2, ln_g, ln_b, wqt, keys, u_tab, v_tab_t)


def _prep_layer_weights(w_in, pool_scale, ssd_conv_w, ssd_conv_b, ssd_dt_bias, ssd_a_log, ssd_d, ssd_norm_w,
                        gdn_conv_w, gdn_dt_bias, gdn_a_log, gdn_norm_w, peer_wq, peer_keys, peer_u, peer_v):
    nl, d, _ = w_in.shape
    seg = lambda a, b_: w_in[:, :, a:b_]
    main = jnp.concatenate([seg(4640, 8736), seg(0, 512), seg(512, 1024), seg(1024, 2048), seg(2576, 4112),
                            seg(2048, 2560), seg(4112, 4624)], axis=-1).astype(BF16)
    small = jnp.concatenate([seg(2560, 2576), seg(4624, 4632), seg(4632, 4640)], axis=-1)
    w_sm = jnp.pad(small, ((0, 0), (0, 0), (0, 96))).astype(BF16)
    w_smt = jnp.swapaxes(small, 1, 2).astype(BF16)
    conv_s = jnp.pad(ssd_conv_w, ((0, 0), (0, 8 - CONV_K), (0, 0)))
    conv_g = jnp.pad(gdn_conv_w, ((0, 0), (0, 8 - CONV_K), (0, 0)))
    zeros = lambda n: jnp.zeros((nl, n), F32)
    bias_row = jnp.concatenate([ssd_dt_bias.reshape(nl, 16), zeros(112)], axis=1)
    alog_row = jnp.concatenate([ssd_a_log.reshape(nl, 16), zeros(112)], axis=1)
    galog_row = jnp.concatenate([zeros(24), gdn_a_log.reshape(nl, 8), zeros(96)], axis=1)
    gbias_row = jnp.concatenate([zeros(24), gdn_dt_bias.reshape(nl, 8), zeros(96)], axis=1)
    par = jnp.stack([bias_row, alog_row, galog_row, gbias_row] + [zeros(128)] * 4, axis=1)
    par_t = jnp.pad(jnp.swapaxes(par[:, 0:4, 0:32], 1, 2), ((0, 0), (0, 0), (0, 124)))
    dx = jnp.repeat(ssd_d, SSD_HEADDIM, axis=1).reshape(nl, 1, BRANCH)
    return dict(
        main=main, w_sm=w_sm, w_smt=w_smt, conv_s=conv_s, conv_sb=ssd_conv_b.reshape(nl, 1, -1),
        conv_g=conv_g, conv_gb=jnp.zeros((nl, 1, 3 * BRANCH), F32), par=par, par_t=par_t, dx=dx,
        pool_scale=pool_scale.reshape(nl, 1, BRANCH), snw=ssd_norm_w.reshape(nl, 1, BRANCH),
        gnw=jnp.tile(gdn_norm_w, (1, GDN_HEADS)).reshape(nl, 1, BRANCH),
        wqt=jnp.swapaxes(peer_wq, 1, 2).astype(BF16),
        keys=peer_keys.reshape(nl, 2 * PEER_HEADS, PEER_NKEYS, -1).astype(BF16),
        u=peer_u.astype(BF16),
        vt=jnp.swapaxes(peer_v.reshape(nl, -1, 2 * PEER_NKEYS, d), 2, 3).astype(BF16))


def _expand_table():
    lane = jnp.arange(128)[:, None]
    col = jnp.arange(BRANCH)[None, :] // SSD_HEADDIM
    return jnp.stack([(lane == col), (lane == col + SSD_HEADS)]).astype(F32)


def _scans(x, shift, scale, lw, l_idx, expand, states):
    proj = _lnmod_mm(x, shift, scale, lw['main'][l_idx], 512)
    small, small_t = _small_proj(x, shift, scale, lw['w_sm'][l_idx], lw['w_smt'][l_idx])
    conv_s = _conv_silu(proj, OFF_XBC, 1024, lw['conv_s'][l_idx], lw['conv_sb'][l_idx])
    conv_g = _conv_silu(proj, OFF_QKV, 1536, lw['conv_g'][l_idx], lw['conv_gb'][l_idx])
    yf, yb, s_ssd = _ssd(conv_s, small, small_t, lw['par'][l_idx], lw['par_t'][l_idx], lw['dx'][l_idx], expand,
                         states[0])
    of, ob, s_gdn = _gdn(conv_g, small, small_t, lw['par'][l_idx], lw['par_t'][l_idx], states[1])
    return proj, (yf, yb, of, ob), (s_ssd, s_gdn)


def _mixer(x, mods, lw, l_idx, expand, tabs, states, pool_w, w_branch, w_out, ln_g, ln_b, two_d, alpha):
    proj, (yf, yb, of, ob), new_states = _scans(x, mods[0], mods[1], lw, l_idx, expand, states)
    pool_o = _pool(proj, pool_w, lw['pool_scale'][l_idx], two_d)
    four_o = _fourier(proj, tabs)
    x = _merge(x, mods[2], ln_g, ln_b, proj, pool_o, four_o, yf, yb, of, ob, lw['snw'][l_idx], lw['gnw'][l_idx],
               w_branch, w_out, alpha)
    return x, new_states


def _peer(x, mods, lw, l_idx, ln_g, ln_b, alpha):
    return _peer_ffn(x, mods[3], mods[4], mods[5], ln_g, ln_b, lw['wqt'][l_idx], lw['keys'][l_idx], lw['u'][l_idx],
                     lw['vt'][l_idx], alpha)


def kernel(x, c, ctx, c_ctx, w_mod, b_mod, w_in, pool_w, pool_scale, ssd_conv_w, ssd_conv_b, ssd_dt_bias, ssd_a_log, ssd_d, ssd_norm_w, gdn_conv_w, gdn_dt_bias, gdn_a_log, gdn_norm_w, w_branch, w_out, ln1_g, ln1_b, peer_wq, peer_keys, peer_u, peer_v, ln2_g, ln2_b):
    bsz, seq, d = x.shape
    nl = w_in.shape[0]
    alpha = (2 * nl) ** 0.25
    lw = _prep_layer_weights(w_in, pool_scale, ssd_conv_w, ssd_conv_b, ssd_dt_bias, ssd_a_log, ssd_d, ssd_norm_w,
                             gdn_conv_w, gdn_dt_bias, gdn_a_log, gdn_norm_w, peer_wq, peer_keys, peer_u, peer_v)
    pool_wb = pool_w.astype(BF16)
    w_branch_b = w_branch.astype(BF16)
    w_out_b = w_out.astype(BF16)
    expand = _expand_table()
    tabs_x = _dft_tables(seq)
    tabs_c = _dft_tables(ctx.shape[1])
    cs = jnp.concatenate([c, c_ctx[None], jnp.zeros((8 - bsz - 1, d), F32)], axis=0)
    mod_all = _mod_all(cs, w_mod.astype(BF16), b_mod).reshape(nl, 8, 6, 1, d)
    zero_states = (jnp.zeros((bsz, 2, SSD_STATE, BRANCH), F32), jnp.zeros((bsz, 2, GDN_HEADS, GDN_DK, GDN_DK), F32))
    vec = lambda a, l_idx: a[l_idx].reshape(1, d)
    for l_idx in range(nl):
        mods_x = [mod_all[l_idx, :bsz, i] for i in range(6)]
        mods_c = [mod_all[l_idx, bsz:bsz + 1, i] for i in range(6)]
        g1, b1, g2, b2 = vec(ln1_g, l_idx), vec(ln1_b, l_idx), vec(ln2_g, l_idx), vec(ln2_b, l_idx)
        if l_idx == nl - 1:
            _, _, states = _scans(ctx, mods_c[0], mods_c[1], lw, l_idx, expand, zero_states)
        else:
            ctx, states = _mixer(ctx, mods_c, lw, l_idx, expand, tabs_c, zero_states, pool_wb[l_idx],
                                 w_branch_b[l_idx], w_out_b[l_idx], g1, b1, False, alpha)
            ctx = _peer(ctx, mods_c, lw, l_idx, g2, b2, alpha)
        x, _ = _mixer(x, mods_x, lw, l_idx, expand, tabs_x, states, pool_wb[l_idx], w_branch_b[l_idx],
                      w_out_b[l_idx], g1, b1, True, alpha)
        x = _peer(x, mods_x, lw, l_idx, g2, b2, alpha)
    return x
```

```python
import functools
import math

import jax
import jax.numpy as jnp
from jax import lax
from jax.experimental import pallas as pl
from jax.experimental.pallas import tpu as pltpu

F32 = jnp.float32
BF16 = jnp.bfloat16
HIGHEST = lax.Precision.HIGHEST

LN_EPS = 1e-6
GRID_W = 64
BRANCH = 512
POOL_WINDOWS = (2, 4, 8, 16)
SSD_HEADS = 8
SSD_HEADDIM = 64
SSD_STATE = 128
SSD_CHUNK = 128
GDN_HEADS = 4
GDN_DK = 128
GDN_CHUNK = 64
CONV_K = 5
PEER_HEADS = 8
PEER_NKEYS = 128
PEER_TOPK = 16
N_MAIN = 8704
OFF_GATES, OFF_POOL, OFF_FOUR, OFF_XBC, OFF_QKV, OFF_SZ, OFF_GZ = 0, 4096, 4608, 5120, 6144, 7680, 8192
SM_DT, SM_BETA, SM_A = 0, 16, 24
VMEM_LIMIT = 52 * 1024 * 1024
NEG_INF = float("-inf")
LOG2E = 1.4426950408889634


def _cp(sem):
    return pltpu.CompilerParams(dimension_semantics=sem, vmem_limit_bytes=VMEM_LIMIT)


def _sigmoid(x):
    return 1.0 / (1.0 + jnp.exp(-x))


def _silu(x):
    return x * _sigmoid(x)


def _softplus(x):
    return jnp.maximum(x, 0.0) + jnp.log(1.0 + jnp.exp(-jnp.abs(x)))


def _ln(x):
    mu = jnp.mean(x, axis=-1, keepdims=True)
    xc = x - mu
    var = jnp.mean(xc * xc, axis=-1, keepdims=True)
    return xc * lax.rsqrt(var + LN_EPS)


def _dot(a, b, **kw):
    return jnp.dot(a, b, preferred_element_type=F32, **kw)


def _dot_nt(a, b, **kw):
    return lax.dot_general(a, b, (((1,), (1,)), ((), ())), preferred_element_type=F32, **kw)


def _dot3(a, b):
    a_hi = a.astype(BF16)
    b_hi = b.astype(BF16)
    a_lo = (a - a_hi.astype(F32)).astype(BF16)
    b_lo = (b - b_hi.astype(F32)).astype(BF16)
    return _dot(a_hi, b_hi) + (_dot(a_hi, b_lo) + _dot(a_lo, b_hi))


def _gelu(x):
    return 0.5 * x * (1.0 + lax.erf(x * (2.0 ** -0.5)))


def _mod_kernel(c_ref, w_ref, b_ref, o_ref):
    s = _silu(c_ref[...])
    o_ref[0] = _dot(s.astype(BF16), w_ref[0]) + b_ref[0]


def _mod_all(cs, w_mod, b_mod):
    nl, d, n6 = w_mod.shape
    tn = 1536
    return pl.pallas_call(
        _mod_kernel,
        grid=(nl, n6 // tn),
        in_specs=[pl.BlockSpec((8, d), lambda l, j: (0, 0)),
                  pl.BlockSpec((1, d, tn), lambda l, j: (l, 0, j)),
                  pl.BlockSpec((1, 1, tn), lambda l, j: (l, 0, j))],
        out_specs=pl.BlockSpec((1, 8, tn), lambda l, j: (l, 0, j)),
        out_shape=jax.ShapeDtypeStruct((nl, 8, n6), F32),
        compiler_params=_cp(("parallel", "parallel")),
        name="mod_vectors",
    )(cs, w_mod, b_mod.reshape(nl, 1, n6))


def _bmap(bs):
    return (lambda b, *_: (b, 0, 0)) if bs > 1 else (lambda b, *_: (0, 0, 0))


def _lnmod_mm_kernel(x_ref, sh_ref, sc_ref, w_ref, o_ref, h_scr):
    @pl.when(pl.program_id(2) == 0)
    def _():
        h = _ln(x_ref[0]) * (1.0 + sc_ref[0]) + sh_ref[0]
        h_scr[...] = h.astype(BF16)

    o_ref[0] = _dot(h_scr[...], w_ref[...])


def _lnmod_mm(x, shift, scale, w, tn):
    b, l, d = x.shape
    n = w.shape[1]
    tm = min(1024, l)
    return pl.pallas_call(
        _lnmod_mm_kernel,
        grid=(b, l // tm, n // tn),
        in_specs=[pl.BlockSpec((1, tm, d), lambda b_, i, j: (b_, i, 0)),
                  pl.BlockSpec((1, 1, d), _bmap(shift.shape[0])),
                  pl.BlockSpec((1, 1, d), _bmap(scale.shape[0])),
                  pl.BlockSpec((d, tn), lambda b_, i, j: (0, j))],
        out_specs=pl.BlockSpec((1, tm, tn), lambda b_, i, j: (b_, i, j)),
        out_shape=jax.ShapeDtypeStruct((b, l, n), F32),
        scratch_shapes=[pltpu.VMEM((tm, d), BF16)],
        compiler_params=_cp(("parallel", "parallel", "arbitrary")),
        name="in_proj",
    )(x, shift, scale, w)


def _small_kernel(x_ref, sh_ref, sc_ref, w_ref, wt_ref, o_ref, ot_ref):
    h = (_ln(x_ref[0]) * (1.0 + sc_ref[0]) + sh_ref[0]).astype(BF16)
    o_ref[0] = _dot(h, w_ref[...])
    ot_ref[0] = _dot_nt(wt_ref[...], h)


def _small_proj(x, shift, scale, w_sm, w_smt):
    b, l, d = x.shape
    tm = min(256, l)
    return pl.pallas_call(
        _small_kernel,
        grid=(b, l // tm),
        in_specs=[pl.BlockSpec((1, tm, d), lambda b_, i: (b_, i, 0)),
                  pl.BlockSpec((1, 1, d), _bmap(shift.shape[0])),
                  pl.BlockSpec((1, 1, d), _bmap(scale.shape[0])),
                  pl.BlockSpec((d, 128), lambda b_, i: (0, 0)),
                  pl.BlockSpec((32, d), lambda b_, i: (0, 0))],
        out_specs=[pl.BlockSpec((1, tm, 128), lambda b_, i: (b_, i, 0)),
                   pl.BlockSpec((1, 32, tm), lambda b_, i: (b_, 0, i))],
        out_shape=[jax.ShapeDtypeStruct((b, l, 128), F32), jax.ShapeDtypeStruct((b, 32, l), F32)],
        compiler_params=_cp(("parallel", "parallel")),
        name="small_proj",
    )(x, shift, scale, w_sm, w_smt)


def _conv_kernel(x_ref, w_ref, b_ref, o_ref, pad_scr, *, l, rc):
    tc = x_ref.shape[2]
    zeros8 = jnp.zeros((8, tc), F32)
    pad_scr[0:8, :] = zeros8
    pad_scr[l + 8:l + 16, :] = zeros8
    pad_scr[8:l + 8, :] = x_ref[0]
    half = CONV_K // 2
    for r0 in range(0, l, rc):
        acc = b_ref[...] + w_ref[0:1, :] * pad_scr[r0 + 8 - half:r0 + 8 - half + rc, :]
        for k in range(1, CONV_K):
            acc = acc + w_ref[k:k + 1, :] * pad_scr[r0 + 8 - half + k:r0 + 8 - half + k + rc, :]
        o_ref[0, r0:r0 + rc, :] = _silu(acc)


def _conv_silu(proj, col_off, width, w, bias):
    b, l, _ = proj.shape
    tc = 256
    cb0 = col_off // tc
    rc = min(512, l)
    return pl.pallas_call(
        functools.partial(_conv_kernel, l=l, rc=rc),
        grid=(b, width // tc),
        in_specs=[pl.BlockSpec((1, l, tc), lambda b_, j: (b_, 0, cb0 + j)),
                  pl.BlockSpec((8, tc), lambda b_, j: (0, j)),
                  pl.BlockSpec((1, tc), lambda b_, j: (0, j))],
        out_specs=pl.BlockSpec((1, l, tc), lambda b_, j: (b_, 0, j)),
        out_shape=jax.ShapeDtypeStruct((b, l, width), F32),
        scratch_shapes=[pltpu.VMEM((l + 16, tc), F32)],
        compiler_params=_cp(("parallel", "parallel")),
        name="conv_silu",
    )(proj, w, bias)


def _tri(n, lower):
    ii = lax.broadcasted_iota(jnp.int32, (n, n), 0)
    jj = lax.broadcasted_iota(jnp.int32, (n, n), 1)
    return (jj <= ii) if lower else (jj >= ii)


def _ssd_kernel(xf_ref, xb_ref, smf_ref, smb_ref, smtf_ref, smtb_ref, par_ref, part_ref, dx_ref, e_ref, s0_ref,
                yf_ref, yb_ref, sfin_ref, s_scr, *, nc):
    c = pl.program_id(1)

    @pl.when(c == 0)
    def _():
        s_scr[...] = s0_ref[0]

    q = SSD_CHUNK
    low = _tri(q, True)
    upp = _tri(q, False)
    lowf = low.astype(F32)
    uppf = upp.astype(F32)
    lane = lax.broadcasted_iota(jnp.int32, (1, 256), 1) // SSD_HEADDIM
    dirs = ((xf_ref, smf_ref, smtf_ref, yf_ref), (xb_ref, smb_ref, smtb_ref, yb_ref))
    for d, (x_ref, sm_ref, smt_ref, y_ref) in enumerate(dirs):
        mask = low if d == 0 else upp
        xbc = x_ref[0]
        xs = xbc[:, 0:512]
        dt = _softplus(sm_ref[0] + par_ref[0:1, :])
        la = dt * (-jnp.exp(par_ref[1:2, :]))
        cum = _dot(lowf if d == 0 else uppf, la, precision=HIGHEST)
        dtt = _softplus(smt_ref[0, 8 * d:8 * d + 8, :] + part_ref[8 * d:8 * d + 8, 0:1])
        lat = dtt * (-jnp.exp(part_ref[8 * d:8 * d + 8, 1:2]))
        cumt = _dot(lat, uppf if d == 0 else lowf, precision=HIGHEST)
        expand = e_ref[d]
        dtx = _dot(dt, expand, precision=HIGHEST)
        cumx = _dot(cum, expand, precision=HIGHEST)
        tot = cumx[q - 1:q, :] if d == 0 else cumx[0:1, :]
        xdt = xs * dtx
        ecum = jnp.exp(cumx)
        xw = (xdt * jnp.exp(tot - cumx)).astype(BF16)
        etot = jnp.exp(tot)
        ys = []
        for g in range(2):
            bg = xbc[:, 512 + 128 * g:640 + 128 * g]
            cg = xbc[:, 768 + 128 * g:896 + 128 * g].astype(BF16)
            sl = slice(256 * g, 256 * g + 256)
            s_old = s_scr[d, :, sl]
            scores = _dot_nt(cg, bg.astype(BF16))
            y = _dot(cg, s_old.astype(BF16)) * ecum[:, sl]
            xg = xdt[:, sl]
            for hh in range(4):
                h = 4 * g + hh
                col = cum[:, 8 * d + h:8 * d + h + 1]
                row = cumt[h:h + 1, :]
                dec = jnp.exp(jnp.where(mask, col - row, NEG_INF))
                xm = jnp.where(lane == hh, xg, 0.0).astype(BF16)
                y = y + _dot((scores * dec).astype(BF16), xm)
            ys.append(y)
            s_scr[d, :, sl] = s_old * etot[:, sl] + _dot(bg.T.astype(BF16), xw[:, sl])
        y = jnp.concatenate(ys, axis=1)
        if d == 0:
            y = y + dx_ref[...] * xs
        y_ref[0] = y

    @pl.when(c == nc - 1)
    def _():
        sfin_ref[0] = s_scr[...]


def _ssd(conv_x, small, small_t, par, par_t, dx, expand, s0):
    b, l, _ = conv_x.shape
    q = SSD_CHUNK
    nc = l // q
    fw = lambda b_, c: (b_, c, 0)
    bw = lambda b_, c: (b_, nc - 1 - c, 0)
    fwt = lambda b_, c: (b_, 0, c)
    bwt = lambda b_, c: (b_, 0, nc - 1 - c)
    full2 = lambda b_, c: (0, 0)
    return pl.pallas_call(
        functools.partial(_ssd_kernel, nc=nc),
        grid=(b, nc),
        in_specs=[pl.BlockSpec((1, q, 1024), fw), pl.BlockSpec((1, q, 1024), bw),
                  pl.BlockSpec((1, q, 128), fw), pl.BlockSpec((1, q, 128), bw),
                  pl.BlockSpec((1, 32, q), fwt), pl.BlockSpec((1, 32, q), bwt),
                  pl.BlockSpec((8, 128), full2), pl.BlockSpec((32, 128), full2),
                  pl.BlockSpec((1, 512), full2),
                  pl.BlockSpec((2, 128, 512), lambda b_, c: (0, 0, 0)),
                  pl.BlockSpec((1, 2, 128, 512), lambda b_, c: (b_, 0, 0, 0))],
        out_specs=[pl.BlockSpec((1, q, 512), fw), pl.BlockSpec((1, q, 512), bw),
                   pl.BlockSpec((1, 2, 128, 512), lambda b_, c: (b_, 0, 0, 0))],
        out_shape=[jax.ShapeDtypeStruct((b, l, 512), F32), jax.ShapeDtypeStruct((b, l, 512), F32),
                   jax.ShapeDtypeStruct((b, 2, 128, 512), F32)],
        scratch_shapes=[pltpu.VMEM((2, 128, 512), F32)],
        compiler_params=_cp(("parallel", "arbitrary")),
        name="ssd_scan",
    )(conv_x, conv_x, small, small, small_t, small_t, par, par_t, dx, expand, s0)


def _gdn_kernel(qf_ref, kf_ref, vf_ref, qb_ref, kb_ref, vb_ref, smf_ref, smb_ref, smtf_ref, smtb_ref,
                par_ref, part_ref, s0_ref, of_ref, ob_ref, sfin_ref, s_scr, *, nc):
    c = pl.program_id(1)

    @pl.when(c == 0)
    def _():
        s_scr[...] = s0_ref[0]

    cl = GDN_CHUNK
    nh = GDN_HEADS
    pk = nh * cl
    rows = 2 * cl
    ii = lax.broadcasted_iota(jnp.int32, (pk, pk), 0)
    jj = lax.broadcasted_iota(jnp.int32, (pk, pk), 1)
    same = (ii // cl) == (jj // cl)
    eye = (ii == jj).astype(F32)
    ri = lax.broadcasted_iota(jnp.int32, (rows, rows), 0)
    rj = lax.broadcasted_iota(jnp.int32, (rows, rows), 1)
    rsame = (ri // cl) == (rj // cl)
    zpad = jnp.zeros((cl, GDN_DK), F32)
    dirs = ((qf_ref, kf_ref, vf_ref, smf_ref, smtf_ref, of_ref), (qb_ref, kb_ref, vb_ref, smb_ref, smtb_ref, ob_ref))
    heads = range(nh)
    probs = []
    for d, (q_ref, k_ref, v_ref, sm_ref, smt_ref, o_ref) in enumerate(dirs):
        incl = same & ((jj <= ii) if d == 0 else (jj >= ii))
        strict = same & ((jj < ii) if d == 0 else (jj > ii))
        tri = (rsame & ((rj <= ri) if d == 0 else (rj >= ri))).astype(F32)
        tri_t = (rsame & ((ri <= rj) if d == 0 else (ri >= rj))).astype(F32)
        sm = sm_ref[0]
        beta_all = _sigmoid(sm)
        g_all = -jnp.exp(par_ref[2:3, :]) * _softplus(sm + par_ref[3:4, :])
        gt8 = -jnp.exp(part_ref[24:32, 2:3]) * _softplus(smt_ref[0, 24:32, :] + part_ref[24:32, 3:4])
        cum_all = _dot(tri, g_all, precision=HIGHEST)
        cumt_all = _dot(gt8, tri_t, precision=HIGHEST)
        for sc in ((0, 1) if d == 0 else (1, 0)):
            rs = slice(cl * sc, cl * sc + cl)
            stack = lambda ref: jnp.concatenate([ref[0, rs, GDN_DK * h:GDN_DK * h + GDN_DK] for h in heads], axis=0)
            qp, kp, vp = stack(q_ref), stack(k_ref), stack(v_ref)
            qp = qp * lax.rsqrt(jnp.sum(qp * qp, axis=-1, keepdims=True) + 1e-6) * (GDN_DK ** -0.5)
            kp = kp * lax.rsqrt(jnp.sum(kp * kp, axis=-1, keepdims=True) + 1e-6)
            col = jnp.concatenate([cum_all[rs, SM_A + 4 * d + h:SM_A + 4 * d + h + 1] for h in heads], axis=0)
            row = jnp.concatenate([cumt_all[4 * d + h:4 * d + h + 1, rs] for h in heads], axis=1)
            beta = jnp.concatenate([beta_all[rs, SM_BETA + 4 * d + h:SM_BETA + 4 * d + h + 1] for h in heads], axis=0)
            edge = cl - 1 if d == 0 else 0
            tot = jnp.concatenate([jnp.broadcast_to(col[cl * h + edge:cl * h + edge + 1, :], (cl, 1)) for h in heads],
                                  axis=0)
            dec = jnp.exp(jnp.where(incl, col - row, NEG_INF))
            kb = kp.astype(BF16)
            a_mat = jnp.where(strict, _dot_nt(kb, kb) * dec * beta, 0.0)
            ecol = jnp.exp(col)
            probs.append(dict(
                d=d, rs=rs, o_ref=o_ref, t=eye - a_mat, p=a_mat,
                rhs=jnp.concatenate([vp * beta, kp * (beta * ecol)], axis=1),
                attn=(_dot_nt(qp.astype(BF16), kb) * dec).astype(BF16),
                qg=qp * ecol, kdec=kp * jnp.exp(tot - col), etot=jnp.exp(tot)))
    for _ in range(5):
        for pb in probs:
            pbf = pb['p'].astype(BF16)
            pb['p'] = _dot(pbf, pbf)
        for pb in probs:
            pb['t'] = pb['t'] + _dot(pb['t'].astype(BF16), pb['p'].astype(BF16))
    for pb in probs:
        pb['uw'] = _dot3(pb['t'], pb['rhs'])
    for step in range(2):
        for pb in (probs[step], probs[2 + step]):
            d, rs, uw = pb['d'], pb['rs'], pb['uw']
            vnew, qs = [], []
            for h in heads:
                hr = slice(cl * h, cl * h + cl)
                wq = jnp.concatenate([uw[hr, GDN_DK:], pb['qg'][hr]], axis=0).astype(BF16)
                ws_qs = _dot(wq, s_scr[d, h].astype(BF16))
                vnew.append(uw[hr, :GDN_DK] - ws_qs[0:cl])
                qs.append(ws_qs[cl:])
            vn = jnp.concatenate(vnew, axis=0)
            out = jnp.concatenate(qs, axis=0) + _dot(pb['attn'], vn.astype(BF16))
            for h in heads:
                hr = slice(cl * h, cl * h + cl)
                pb['o_ref'][0, rs, GDN_DK * h:GDN_DK * h + GDN_DK] = out[hr]
                kd_t = jnp.concatenate([pb['kdec'][hr], zpad], axis=0).T.astype(BF16)
                vn_pad = jnp.concatenate([vnew[h], zpad], axis=0).astype(BF16)
                s_scr[d, h] = s_scr[d, h] * pb['etot'][cl * h:cl * h + 1, :] + _dot(kd_t, vn_pad)

    @pl.when(c == nc - 1)
    def _():
        sfin_ref[0] = s_scr[...]


def _gdn(conv_g, small, small_t, par, par_t, s0):
    b, l, _ = conv_g.shape
    rows = 2 * GDN_CHUNK
    nc = l // rows
    fwm = lambda j: (lambda b_, c: (b_, c, j))
    bwm = lambda j: (lambda b_, c: (b_, nc - 1 - c, j))
    fw, bw = fwm(0), bwm(0)
    fwt = lambda b_, c: (b_, 0, c)
    bwt = lambda b_, c: (b_, 0, nc - 1 - c)
    full2 = lambda b_, c: (0, 0)
    st = lambda b_, c: (b_, 0, 0, 0, 0)
    return pl.pallas_call(
        functools.partial(_gdn_kernel, nc=nc),
        grid=(b, nc),
        in_specs=[pl.BlockSpec((1, rows, 512), fwm(0)), pl.BlockSpec((1, rows, 512), fwm(1)),
                  pl.BlockSpec((1, rows, 512), fwm(2)),
                  pl.BlockSpec((1, rows, 512), bwm(0)), pl.BlockSpec((1, rows, 512), bwm(1)),
                  pl.BlockSpec((1, rows, 512), bwm(2)),
                  pl.BlockSpec((1, rows, 128), fw), pl.BlockSpec((1, rows, 128), bw),
                  pl.BlockSpec((1, 32, rows), fwt), pl.BlockSpec((1, 32, rows), bwt),
                  pl.BlockSpec((8, 128), full2), pl.BlockSpec((32, 128), full2),
                  pl.BlockSpec((1, 2, GDN_HEADS, 128, 128), st)],
        out_specs=[pl.BlockSpec((1, rows, 512), fw), pl.BlockSpec((1, rows, 512), bw),
                   pl.BlockSpec((1, 2, GDN_HEADS, 128, 128), st)],
        out_shape=[jax.ShapeDtypeStruct((b, l, 512), F32), jax.ShapeDtypeStruct((b, l, 512), F32),
                   jax.ShapeDtypeStruct((b, 2, GDN_HEADS, 128, 128), F32)],
        scratch_shapes=[pltpu.VMEM((2, GDN_HEADS, 128, 128), F32)],
        compiler_params=_cp(("parallel", "arbitrary")),
        name="gdn_scan",
    )(conv_g, conv_g, conv_g, conv_g, conv_g, conv_g, small, small, small_t, small_t, par, par_t, s0)


def _pool_body(x_ref, w_ref, sc_ref, o_ref, pa, pb, *, l, gw, two_d, win, rc):
    lo = win // 2
    hi = win - 1 - lo
    nrow = l // gw
    shift = int(math.log2(gw))
    pbw = 8 * gw
    z8 = jnp.zeros((8, 128), F32)
    pa[0:8, :] = z8
    pa[l + 8:l + 16, :] = z8
    pa[8:l + 8, :] = x_ref[0]
    if two_d:
        zb = jnp.zeros((pbw, 128), F32)
        pb[0:pbw, :] = zb
        pb[pbw + l:pbw + l + pbw, :] = zb

    def finish(r0, acc, pos):
        col = pos & (gw - 1)
        cnt = (jnp.minimum(col + hi, gw - 1) - jnp.maximum(col - lo, 0) + 1).astype(F32)
        if two_d:
            row = pos >> shift
            cnt = cnt * (jnp.minimum(row + hi, nrow - 1) - jnp.maximum(row - lo, 0) + 1).astype(F32)
        dlt = (acc / cnt - x_ref[0, r0:r0 + rc, :]).astype(BF16)
        o_ref[0, r0:r0 + rc, :] = _dot(dlt, w_ref[0]) * sc_ref[...]

    for r0 in range(0, l, rc):
        pos = lax.broadcasted_iota(jnp.int32, (rc, 128), 0) + r0
        col = pos & (gw - 1)
        acc = pa[8 + r0:8 + r0 + rc, :]
        for j in range(-lo, hi + 1):
            if j == 0:
                continue
            v = pa[8 + r0 + j:8 + r0 + j + rc, :]
            ok = (col + j >= 0) if j < 0 else (col + j < gw)
            acc = acc + jnp.where(ok, v, 0.0)
        if two_d:
            pb[pbw + r0:pbw + r0 + rc, :] = acc
        else:
            finish(r0, acc, pos)
    if two_d:
        for r0 in range(0, l, rc):
            pos = lax.broadcasted_iota(jnp.int32, (rc, 128), 0) + r0
            acc = pb[pbw + r0:pbw + r0 + rc, :]
            for j in range(-lo, hi + 1):
                if j != 0:
                    acc = acc + pb[pbw + r0 + gw * j:pbw + r0 + gw * j + rc, :]
            finish(r0, acc, pos)


def _pool_kernel(x_ref, w_ref, sc_ref, o_ref, pa, pb, **kw):
    g = pl.program_id(1)
    for gi, win in enumerate(POOL_WINDOWS):
        @pl.when(g == gi)
        def _(win=win):
            _pool_body(x_ref, w_ref, sc_ref, o_ref, pa, pb, win=win, **kw)


def _pool(proj, pool_w, pool_scale, two_d):
    b, l, _ = proj.shape
    gw = GRID_W if two_d else l
    assert gw & (gw - 1) == 0 and l % gw == 0
    rc = min(512, l)
    cb0 = OFF_POOL // 128
    pb_rows = l + 16 * gw if two_d else 8
    return pl.pallas_call(
        functools.partial(_pool_kernel, l=l, gw=gw, two_d=two_d, rc=rc),
        grid=(b, len(POOL_WINDOWS)),
        in_specs=[pl.BlockSpec((1, l, 128), lambda b_, g: (b_, 0, cb0 + g)),
                  pl.BlockSpec((1, 128, 128), lambda b_, g: (g, 0, 0)),
                  pl.BlockSpec((1, 128), lambda b_, g: (0, g))],
        out_specs=pl.BlockSpec((1, l, 128), lambda b_, g: (b_, 0, g)),
        out_shape=jax.ShapeDtypeStruct((b, l, BRANCH), F32),
        scratch_shapes=[pltpu.VMEM((l + 16, 128), F32), pltpu.VMEM((pb_rows, 128), F32)],
        compiler_params=_cp(("parallel", "parallel")),
        name="pool_branch",
    )(proj, pool_w, pool_scale)


def _four1_kernel(x_ref, cc_ref, ss_ref, xc_ref, xs_ref):
    x = x_ref[0]
    xc_ref[0] = _dot(x, cc_ref[...], precision=HIGHEST).astype(BF16)
    xs_ref[0] = _dot(x, ss_ref[...], precision=HIGHEST).astype(BF16)


def _four2_kernel(ct_ref, st_ref, xc_ref, xs_ref, o_ref):
    o_ref[0] = _dot(ct_ref[...], xc_ref[0]) + _dot(st_ref[...], xs_ref[0])


def _fourier(proj, tabs):
    cc, ss, cl_, sl_ = tabs
    b, l, _ = proj.shape
    tm = min(512, l)
    cb = OFF_FOUR // BRANCH
    xc, xs = pl.pallas_call(
        _four1_kernel,
        grid=(b, l // tm),
        in_specs=[pl.BlockSpec((1, tm, BRANCH), lambda b_, i: (b_, i, cb)),
                  pl.BlockSpec((BRANCH, BRANCH), lambda b_, i: (0, 0)),
                  pl.BlockSpec((BRANCH, BRANCH), lambda b_, i: (0, 0))],
        out_specs=[pl.BlockSpec((1, tm, BRANCH), lambda b_, i: (b_, i, 0))] * 2,
        out_shape=[jax.ShapeDtypeStruct((b, l, BRANCH), BF16)] * 2,
        compiler_params=_cp(("parallel", "parallel")),
        name="dft_channels",
    )(proj, cc, ss)
    tr = min(256, l)
    return pl.pallas_call(
        _four2_kernel,
        grid=(b, l // tr),
        in_specs=[pl.BlockSpec((tr, l), lambda b_, i: (i, 0)),
                  pl.BlockSpec((tr, l), lambda b_, i: (i, 0)),
                  pl.BlockSpec((1, l, BRANCH), lambda b_, i: (b_, 0, 0)),
                  pl.BlockSpec((1, l, BRANCH), lambda b_, i: (b_, 0, 0))],
        out_specs=pl.BlockSpec((1, tr, BRANCH), lambda b_, i: (b_, i, 0)),
        out_shape=jax.ShapeDtypeStruct((b, l, BRANCH), F32),
        compiler_params=_cp(("parallel", "parallel")),
        name="dft_positions",
    )(cl_, sl_, xc, xs)


def _dft_tables(l):
    c = BRANCH // 4
    k = jnp.arange(c, dtype=jnp.int32)
    ang = (2.0 * math.pi / c) * ((k[:, None] * k[None, :]) % c).astype(F32)
    scale = 1.0 / math.sqrt(l * c)
    eye4 = jnp.eye(4, dtype=F32)
    cc = jnp.kron(eye4, jnp.cos(ang) * scale)
    ss = jnp.kron(eye4, jnp.sin(ang) * scale)
    t = jnp.arange(l, dtype=jnp.int32)
    angl = (2.0 * math.pi / l) * ((t[:, None] * t[None, :]) % l).astype(F32)
    return cc, ss, jnp.cos(angl).astype(BF16), (-jnp.sin(angl)).astype(BF16)


def _merge_kernel(x_ref, g1_ref, lng_ref, lnb_ref, gates_ref, pool_ref, four_ref, yf_ref, yb_ref, sz_ref,
                  of_ref, ob_ref, gz_ref, snw_ref, gnw_ref, wbr_ref, wout_ref, o_ref, *, alpha):
    d = x_ref.shape[2]
    ssd_g = (yf_ref[0] + yb_ref[0]) * _silu(sz_ref[0])
    parts = []
    for g in range(2):
        blk = ssd_g[:, 256 * g:256 * g + 256]
        parts.append(blk * lax.rsqrt(jnp.mean(blk * blk, axis=-1, keepdims=True) + LN_EPS))
    ssd_out = jnp.concatenate(parts, axis=1) * snw_ref[...]
    o = of_ref[0] + ob_ref[0]
    parts = []
    for h in range(GDN_HEADS):
        blk = o[:, 128 * h:128 * h + 128]
        parts.append(blk * lax.rsqrt(jnp.mean(blk * blk, axis=-1, keepdims=True) + LN_EPS))
    gdn_out = jnp.concatenate(parts, axis=1) * gnw_ref[...] * _silu(gz_ref[0])
    merged = None
    for i, br in enumerate((pool_ref[0], four_ref[0], ssd_out, gdn_out)):
        term = _sigmoid(gates_ref[0, :, d * i:d * i + d]) * _dot(br.astype(BF16), wbr_ref[i])
        merged = term if merged is None else merged + term
    mix = _dot(merged.astype(BF16), wout_ref[...])
    y = alpha * x_ref[0] + g1_ref[0] * mix
    o_ref[0] = _ln(y) * lng_ref[...] + lnb_ref[...]


def _merge(x, gate1, ln_g, ln_b, proj, pool_o, four_o, yf, yb, of, ob, snw, gnw, wbr, wout, alpha):
    b, l, d = x.shape
    tm = min(256, l)
    row = lambda b_, i: (b_, i, 0)
    colb = lambda j: (lambda b_, i: (b_, i, j))
    vec = lambda b_, i: (0, 0)
    br = pl.BlockSpec((1, tm, BRANCH), row)
    return pl.pallas_call(
        functools.partial(_merge_kernel, alpha=alpha),
        grid=(b, l // tm),
        in_specs=[pl.BlockSpec((1, tm, d), row),
                  pl.BlockSpec((1, 1, d), _bmap(gate1.shape[0])),
                  pl.BlockSpec((1, d), vec), pl.BlockSpec((1, d), vec),
                  pl.BlockSpec((1, tm, 4 * d), colb(OFF_GATES // (4 * d))),
                  br, br, br, br,
                  pl.BlockSpec((1, tm, BRANCH), colb(OFF_SZ // BRANCH)),
                  br, br,
                  pl.BlockSpec((1, tm, BRANCH), colb(OFF_GZ // BRANCH)),
                  pl.BlockSpec((1, BRANCH), vec), pl.BlockSpec((1, BRANCH), vec),
                  pl.BlockSpec((4, BRANCH, d), lambda b_, i: (0, 0, 0)),
                  pl.BlockSpec((d, d), vec)],
        out_specs=pl.BlockSpec((1, tm, d), row),
        out_shape=jax.ShapeDtypeStruct((b, l, d), F32),
        compiler_params=_cp(("parallel", "parallel")),
        name="branch_merge",
    )(x, gate1, ln_g, ln_b, proj, pool_o, four_o, yf, yb, proj, of, ob, proj, snw, gnw, wbr, wout)


def _sort_network(n):
    pairs = []

    def merge(lo, hi, r):
        step = r * 2
        if step < hi - lo:
            merge(lo, hi, step)
            merge(lo + r, hi, step)
            pairs.extend((i, i + r) for i in range(lo + r, hi - r, step))
        else:
            pairs.append((lo, lo + r))

    def sort(lo, hi):
        if hi - lo >= 1:
            mid = lo + (hi - lo) // 2
            sort(lo, mid)
            sort(mid + 1, hi)
            merge(lo, hi, 1)

    sort(0, n - 1)
    return pairs


def _top_desc(x, n):
    nrow = x.shape[0] // 8
    lst = [x[8 * r:8 * r + 8, :] for r in range(nrow)]
    npad = 1 << (nrow - 1).bit_length()
    lst += [jnp.full(lst[0].shape, NEG_INF, F32)] * (npad - nrow)
    for i, j in _sort_network(npad):
        lst[i], lst[j] = jnp.maximum(lst[i], lst[j]), jnp.minimum(lst[i], lst[j])
    lst = lst[:nrow]
    sub = lax.broadcasted_iota(jnp.int32, lst[0].shape, 0).astype(F32)
    out = []
    for r in range(n):
        m = jnp.max(lst[0], axis=0, keepdims=True)
        out.append(m)
        left = n - r - 1
        if left:
            win = jnp.min(jnp.where(lst[0] == m, sub, 8.0), axis=0, keepdims=True)
            winner = sub == win
            for j in range(min(nrow, left)):
                nxt = lst[j + 1] if j + 1 < nrow else NEG_INF
                lst[j] = jnp.where(winner, nxt, lst[j])
    return out


def _peer_kernel(x_ref, sh_ref, sc_ref, g2_ref, lng_ref, lnb_ref, wqt_ref, keys_ref, u_ref, vt_ref, o_ref,
                 h_scr, qt_scr, th_scr, a1_scr, s2_scr, cand_scr, act_scr, p_scr, acc_scr,
                 *, n_pair, n_eb, alpha):
    eb = pl.program_id(2)
    t = x_ref.shape[1]
    ntc = t // 128
    k1 = PEER_TOPK + 1
    pairs = [(a, b) for a in range(k1) for b in range(k1) if (a + 1) * (b + 1) <= k1]

    @pl.when(eb == 0)
    def _():
        h = (_ln(x_ref[0]) * (1.0 + sc_ref[0]) + sh_ref[0]).astype(BF16)
        h_scr[...] = h
        qt_scr[...] = _dot_nt(wqt_ref[...], h).astype(BF16)
        acc_scr[...] = jnp.zeros(acc_scr.shape, F32)
        for r in range(len(pairs), cand_scr.shape[0]):
            cand_scr[r:r + 1, :] = jnp.full((1, 128), NEG_INF, F32)

        def head_body(hd, carry):
            for tc in range(ntc):
                ts = slice(128 * tc, 128 * tc + 128)
                sv, tops = [], []
                for s in range(2):
                    hs = 2 * hd + s
                    q_rows = qt_scr[pl.ds(pl.multiple_of(hs * 128, 128), 128), ts]
                    st = _dot(keys_ref[hs], q_rows)
                    sv.append(st)
                    tops.append(_top_desc(st, k1))
                v1, v2 = tops
                for r, (a, b) in enumerate(pairs):
                    cand_scr[r:r + 1, :] = v1[a] + v2[b]
                best = _top_desc(cand_scr[...], k1)
                z = None
                for kk in range(PEER_TOPK):
                    term = jnp.exp(best[kk] - best[0])
                    z = term if z is None else z + term
                thr = 0.5 * (best[PEER_TOPK - 1] + best[PEER_TOPK])
                th_scr[hd, tc] = (thr - sv[0]) * LOG2E
                a1_scr[hd, tc] = (sv[0] - best[0]) * LOG2E - jnp.log(z) * LOG2E
                s2_scr[hd, tc] = sv[1] * LOG2E
            return carry

        lax.fori_loop(0, PEER_HEADS, head_body, 0)

    pr = 2 * PEER_NKEYS
    for k in range(n_pair):
        act = _gelu(_dot_nt(u_ref[pr * k:pr * k + pr, :], h_scr[...]))
        for tc in range(ntc):
            act_scr[k % 2, tc] = act[:, 128 * tc:128 * tc + 128]
        if k:
            acc_scr[...] += _dot(vt_ref[k - 1], p_scr[(k - 1) % 2])
        i1 = eb * (2 * n_pair) + 2 * k
        for j in range(2):
            js = slice(PEER_NKEYS * j, PEER_NKEYS * j + PEER_NKEYS)
            for tc in range(ntc):
                w = None
                for hd in range(PEER_HEADS):
                    th = th_scr[hd, tc, pl.ds(i1 + j, 1), :]
                    a1 = a1_scr[hd, tc, pl.ds(i1 + j, 1), :]
                    s2 = s2_scr[hd, tc]
                    term = jnp.where(s2 > th, jnp.exp2(s2 + a1), 0.0)
                    w = term if w is None else w + term
                p_scr[k % 2, js, 128 * tc:128 * tc + 128] = (w * act_scr[k % 2, tc, js, :]).astype(BF16)
    acc_scr[...] += _dot(vt_ref[n_pair - 1], p_scr[(n_pair - 1) % 2])

    @pl.when(eb == n_eb - 1)
    def _():
        y = alpha * x_ref[0] + g2_ref[0] * acc_scr[...].T
        o_ref[0] = _ln(y) * lng_ref[...] + lnb_ref[...]


def _peer_ffn(x, shift, scale, gate2, ln_g, ln_b, wqt, keys, u_tab, v_tab_t, alpha):
    b, l, d = x.shape
    t = min(512, l)
    n_exp = u_tab.shape[0]
    n_pair = 4
    pr = 2 * PEER_NKEYS
    be = n_pair * pr
    n_eb = n_exp // be
    nh2 = 2 * PEER_HEADS
    k1 = PEER_TOPK + 1
    npairs = sum(1 for a in range(k1) for b_ in range(k1) if (a + 1) * (b_ + 1) <= k1)
    ncand = -(-npairs // 8) * 8
    vec = lambda b_, i, e: (0, 0)
    per_head = pltpu.VMEM((PEER_HEADS, t // 128, PEER_NKEYS, 128), F32)
    return pl.pallas_call(
        functools.partial(_peer_kernel, n_pair=n_pair, n_eb=n_eb, alpha=alpha),
        grid=(b, l // t, n_eb),
        in_specs=[pl.BlockSpec((1, t, d), lambda b_, i, e: (b_, i, 0)),
                  pl.BlockSpec((1, 1, d), _bmap(shift.shape[0])),
                  pl.BlockSpec((1, 1, d), _bmap(scale.shape[0])),
                  pl.BlockSpec((1, 1, d), _bmap(gate2.shape[0])),
                  pl.BlockSpec((1, d), vec), pl.BlockSpec((1, d), vec),
                  pl.BlockSpec((nh2 * 128, d), vec),
                  pl.BlockSpec((nh2, PEER_NKEYS, 128), lambda b_, i, e: (0, 0, 0)),
                  pl.BlockSpec((be, d), lambda b_, i, e: (e, 0)),
                  pl.BlockSpec((n_pair, d, pr), lambda b_, i, e: (e, 0, 0))],
        out_specs=pl.BlockSpec((1, t, d), lambda b_, i, e: (b_, i, 0)),
        out_shape=jax.ShapeDtypeStruct((b, l, d), F32),
        scratch_shapes=[pltpu.VMEM((t, d), BF16), pltpu.VMEM((nh2 * 128, t), BF16),
                        per_head, per_head, per_head,
                        pltpu.VMEM((ncand, 128), F32), pltpu.VMEM((2, t // 128, pr, 128), F32),
                        pltpu.VMEM((2, pr, t), BF16),
                        pltpu.VMEM((d, t), F32)],
        compiler_params=_cp(("parallel", "parallel", "arbitrary")),
        name="peer_ffn",
    )(x, shift, scale, gate2, ln_g, ln_b, wqt, keys, u_tab, v_tab_t)


def _prep_layer_weights(w_in, pool_scale, ssd_conv_w, ssd_conv_b, ssd_dt_bias, ssd_a_log, ssd_d, ssd_norm_w,
                        gdn_conv_w, gdn_dt_bias, gdn_a_log, gdn_norm_w, peer_wq, peer_keys, peer_u, peer_v):
    nl, d, _ = w_in.shape
    seg = lambda a, b_: w_in[:, :, a:b_]
    main = jnp.concatenate([seg(4640, 8736), seg(0, 512), seg(512, 1024), seg(1024, 2048), seg(2576, 4112),
                            seg(2048, 2560), seg(4112, 4624)], axis=-1).astype(BF16)
    small = jnp.concatenate([seg(2560, 2576), seg(4624, 4632), seg(4632, 4640)], axis=-1)
    w_sm = jnp.pad(small, ((0, 0), (0, 0), (0, 96))).astype(BF16)
    w_smt = jnp.swapaxes(small, 1, 2).astype(BF16)
    conv_s = jnp.pad(ssd_conv_w, ((0, 0), (0, 8 - CONV_K), (0, 0)))
    conv_g = jnp.pad(gdn_conv_w, ((0, 0), (0, 8 - CONV_K), (0, 0)))
    zeros = lambda n: jnp.zeros((nl, n), F32)
    bias_row = jnp.concatenate([ssd_dt_bias.reshape(nl, 16), zeros(112)], axis=1)
    alog_row = jnp.concatenate([ssd_a_log.reshape(nl, 16), zeros(112)], axis=1)
    galog_row = jnp.concatenate([zeros(24), gdn_a_log.reshape(nl, 8), zeros(96)], axis=1)
    gbias_row = jnp.concatenate([zeros(24), gdn_dt_bias.reshape(nl, 8), zeros(96)], axis=1)
    par = jnp.stack([bias_row, alog_row, galog_row, gbias_row] + [zeros(128)] * 4, axis=1)
    par_t = jnp.pad(jnp.swapaxes(par[:, 0:4, 0:32], 1, 2), ((0, 0), (0, 0), (0, 124)))
    dx = jnp.repeat(ssd_d, SSD_HEADDIM, axis=1).reshape(nl, 1, BRANCH)
    return dict(
        main=main, w_sm=w_sm, w_smt=w_smt, conv_s=conv_s, conv_sb=ssd_conv_b.reshape(nl, 1, -1),
        conv_g=conv_g, conv_gb=jnp.zeros((nl, 1, 3 * BRANCH), F32), par=par, par_t=par_t, dx=dx,
        pool_scale=pool_scale.reshape(nl, 1, BRANCH), snw=ssd_norm_w.reshape(nl, 1, BRANCH),
        gnw=jnp.tile(gdn_norm_w, (1, GDN_HEADS)).reshape(nl, 1, BRANCH),
        wqt=jnp.swapaxes(peer_wq, 1, 2).astype(BF16),
        keys=peer_keys.reshape(nl, 2 * PEER_HEADS, PEER_NKEYS, -1).astype(BF16),
        u=peer_u.astype(BF16),
        vt=jnp.swapaxes(peer_v.reshape(nl, -1, 2 * PEER_NKEYS, d), 2, 3).astype(BF16))


def _expand_table():
    lane = jnp.arange(128)[:, None]
    col = jnp.arange(BRANCH)[None, :] // SSD_HEADDIM
    return jnp.stack([(lane == col), (lane == col + SSD_HEADS)]).astype(F32)


def _scans(x, shift, scale, lw, l_idx, expand, states):
    proj = _lnmod_mm(x, shift, scale, lw['main'][l_idx], 512)
    small, small_t = _small_proj(x, shift, scale, lw['w_sm'][l_idx], lw['w_smt'][l_idx])
    conv_s = _conv_silu(proj, OFF_XBC, 1024, lw['conv_s'][l_idx], lw['conv_sb'][l_idx])
    conv_g = _conv_silu(proj, OFF_QKV, 1536, lw['conv_g'][l_idx], lw['conv_gb'][l_idx])
    yf, yb, s_ssd = _ssd(conv_s, small, small_t, lw['par'][l_idx], lw['par_t'][l_idx], lw['dx'][l_idx], expand,
                         states[0])
    of, ob, s_gdn = _gdn(conv_g, small, small_t, lw['par'][l_idx], lw['par_t'][l_idx], states[1])
    return proj, (yf, yb, of, ob), (s_ssd, s_gdn)


def _mixer(x, mods, lw, l_idx, expand, tabs, states, pool_w, w_branch, w_out, ln_g, ln_b, two_d, alpha):
    proj, (yf, yb, of, ob), new_states = _scans(x, mods[0], mods[1], lw, l_idx, expand, states)
    pool_o = _pool(proj, pool_w, lw['pool_scale'][l_idx], two_d)
    four_o = _fourier(proj, tabs)
    x = _merge(x, mods[2], ln_g, ln_b, proj, pool_o, four_o, yf, yb, of, ob, lw['snw'][l_idx], lw['gnw'][l_idx],
               w_branch, w_out, alpha)
    return x, new_states


def _peer(x, mods, lw, l_idx, ln_g, ln_b, alpha):
    return _peer_ffn(x, mods[3], mods[4], mods[5], ln_g, ln_b, lw['wqt'][l_idx], lw['keys'][l_idx], lw['u'][l_idx],
                     lw['vt'][l_idx], alpha)


def kernel(x, c, ctx, c_ctx, w_mod, b_mod, w_in, pool_w, pool_scale, ssd_conv_w, ssd_conv_b, ssd_dt_bias, ssd_a_log, ssd_d, ssd_norm_w, gdn_conv_w, gdn_dt_bias, gdn_a_log, gdn_norm_w, w_branch, w_out, ln1_g, ln1_b, peer_wq, peer_keys, peer_u, peer_v, ln2_g, ln2_b):
    bsz, seq, d = x.shape
    nl = w_in.shape[0]
    alpha = (2 * nl) ** 0.25
    lw = _prep_layer_weights(w_in, pool_scale, ssd_conv_w, ssd_conv_b, ssd_dt_bias, ssd_a_log, ssd_d, ssd_norm_w,
                             gdn_conv_w, gdn_dt_bias, gdn_a_log, gdn_norm_w, peer_wq, peer_keys, peer_u, peer_v)
    pool_wb = pool_w.astype(BF16)
    w_branch_b = w_branch.astype(BF16)
    w_out_b = w_out.astype(BF16)
    expand = _expand_table()
    tabs_x = _dft_tables(seq)
    tabs_c = _dft_tables(ctx.shape[1])
    cs = jnp.concatenate([c, c_ctx[None], jnp.zeros((8 - bsz - 1, d), F32)], axis=0)
    mod_all = _mod_all(cs, w_mod.astype(BF16), b_mod).reshape(nl, 8, 6, 1, d)
    zero_states = (jnp.zeros((bsz, 2, SSD_STATE, BRANCH), F32), jnp.zeros((bsz, 2, GDN_HEADS, GDN_DK, GDN_DK), F32))
    vec = lambda a, l_idx: a[l_idx].reshape(1, d)
    for l_idx in range(nl):
        mods_x = [mod_all[l_idx, :bsz, i] for i in range(6)]
        mods_c = [mod_all[l_idx, bsz:bsz + 1, i] for i in range(6)]
        g1, b1, g2, b2 = vec(ln1_g, l_idx), vec(ln1_b, l_idx), vec(ln2_g, l_idx), vec(ln2_b, l_idx)
        if l_idx == nl - 1:
            _, _, states = _scans(ctx, mods_c[0], mods_c[1], lw, l_idx, expand, zero_states)
        else:
            ctx, states = _mixer(ctx, mods_c, lw, l_idx, expand, tabs_c, zero_states, pool_wb[l_idx],
                                 w_branch_b[l_idx], w_out_b[l_idx], g1, b1, False, alpha)
            ctx = _peer(ctx, mods_c, lw, l_idx, g2, b2, alpha)
        x, _ = _mixer(x, mods_x, lw, l_idx, expand, tabs_x, states, pool_wb[l_idx], w_branch_b[l_idx],
                      w_out_b[l_idx], g1, b1, True, alpha)
        x = _peer(x, mods_x, lw, l_idx, g2, b2, alpha)
    return x
```

```python
import functools
import math

import jax
import jax.numpy as jnp
from jax import lax
from jax.experimental import pallas as pl
from jax.experimental.pallas import tpu as pltpu

F32 = jnp.float32
BF16 = jnp.bfloat16
HIGHEST = lax.Precision.HIGHEST

LN_EPS = 1e-6
GRID_W = 64
BRANCH = 512
POOL_WINDOWS = (2, 4, 8, 16)
SSD_HEADS = 8
SSD_HEADDIM = 64
SSD_STATE = 128
SSD_CHUNK = 128
GDN_HEADS = 4
GDN_DK = 128
GDN_CHUNK = 64
CONV_K = 5
PEER_HEADS = 8
PEER_NKEYS = 128
PEER_TOPK = 16
N_MAIN = 8704
OFF_GATES, OFF_POOL, OFF_FOUR, OFF_XBC, OFF_QKV, OFF_SZ, OFF_GZ = 0, 4096, 4608, 5120, 6144, 7680, 8192
SM_DT, SM_BETA, SM_A = 0, 16, 24
VMEM_LIMIT = 52 * 1024 * 1024
NEG_INF = float("-inf")
LOG2E = 1.4426950408889634


def _cp(sem):
    return pltpu.CompilerParams(dimension_semantics=sem, vmem_limit_bytes=VMEM_LIMIT)


def _sigmoid(x):
    return 1.0 / (1.0 + jnp.exp(-x))


def _silu(x):
    return x * _sigmoid(x)


def _softplus(x):
    return jnp.maximum(x, 0.0) + jnp.log(1.0 + jnp.exp(-jnp.abs(x)))


def _ln(x):
    mu = jnp.mean(x, axis=-1, keepdims=True)
    xc = x - mu
    var = jnp.mean(xc * xc, axis=-1, keepdims=True)
    return xc * lax.rsqrt(var + LN_EPS)


def _dot(a, b, **kw):
    return jnp.dot(a, b, preferred_element_type=F32, **kw)


def _dot_nt(a, b, **kw):
    return lax.dot_general(a, b, (((1,), (1,)), ((), ())), preferred_element_type=F32, **kw)


def _dot3(a, b):
    a_hi = a.astype(BF16)
    b_hi = b.astype(BF16)
    a_lo = (a - a_hi.astype(F32)).astype(BF16)
    b_lo = (b - b_hi.astype(F32)).astype(BF16)
    return _dot(a_hi, b_hi) + (_dot(a_hi, b_lo) + _dot(a_lo, b_hi))


def _dot_exact_rhs(a, b):
    bb = b.astype(BF16)
    a1 = a.astype(BF16)
    r1 = a - a1.astype(F32)
    a2 = r1.astype(BF16)
    a3 = (r1 - a2.astype(F32)).astype(BF16)
    return _dot(a1, bb) + (_dot(a2, bb) + _dot(a3, bb))


def _gelu(x):
    return 0.5 * x * (1.0 + lax.erf(x * (2.0 ** -0.5)))


def _mod_kernel(c_ref, w_ref, b_ref, o_ref):
    s = _silu(c_ref[...])
    o_ref[0] = _dot(s.astype(BF16), w_ref[0]) + b_ref[0]


def _mod_all(cs, w_mod, b_mod):
    nl, d, n6 = w_mod.shape
    tn = 1536
    return pl.pallas_call(
        _mod_kernel,
        grid=(nl, n6 // tn),
        in_specs=[pl.BlockSpec((8, d), lambda l, j: (0, 0)),
                  pl.BlockSpec((1, d, tn), lambda l, j: (l, 0, j)),
                  pl.BlockSpec((1, 1, tn), lambda l, j: (l, 0, j))],
        out_specs=pl.BlockSpec((1, 8, tn), lambda l, j: (l, 0, j)),
        out_shape=jax.ShapeDtypeStruct((nl, 8, n6), F32),
        compiler_params=_cp(("parallel", "parallel")),
        name="mod_vectors",
    )(cs, w_mod, b_mod.reshape(nl, 1, n6))


def _bmap(bs):
    return (lambda b, *_: (b, 0, 0)) if bs > 1 else (lambda b, *_: (0, 0, 0))


def _lnmod_mm_kernel(x_ref, sh_ref, sc_ref, w_ref, o_ref, h_scr):
    @pl.when(pl.program_id(2) == 0)
    def _():
        h = _ln(x_ref[0]) * (1.0 + sc_ref[0]) + sh_ref[0]
        h_scr[...] = h.astype(BF16)

    o_ref[0] = _dot(h_scr[...], w_ref[...])


def _lnmod_mm(x, shift, scale, w, tn):
    b, l, d = x.shape
    n = w.shape[1]
    tm = min(1024, l)
    return pl.pallas_call(
        _lnmod_mm_kernel,
        grid=(b, l // tm, n // tn),
        in_specs=[pl.BlockSpec((1, tm, d), lambda b_, i, j: (b_, i, 0)),
                  pl.BlockSpec((1, 1, d), _bmap(shift.shape[0])),
                  pl.BlockSpec((1, 1, d), _bmap(scale.shape[0])),
                  pl.BlockSpec((d, tn), lambda b_, i, j: (0, j))],
        out_specs=pl.BlockSpec((1, tm, tn), lambda b_, i, j: (b_, i, j)),
        out_shape=jax.ShapeDtypeStruct((b, l, n), F32),
        scratch_shapes=[pltpu.VMEM((tm, d), BF16)],
        compiler_params=_cp(("parallel", "parallel", "arbitrary")),
        name="in_proj",
    )(x, shift, scale, w)


def _small_kernel(x_ref, sh_ref, sc_ref, w_ref, wt_ref, o_ref, ot_ref):
    h = (_ln(x_ref[0]) * (1.0 + sc_ref[0]) + sh_ref[0]).astype(BF16)
    o_ref[0] = _dot(h, w_ref[...])
    ot_ref[0] = _dot_nt(wt_ref[...], h)


def _small_proj(x, shift, scale, w_sm, w_smt):
    b, l, d = x.shape
    tm = min(256, l)
    return pl.pallas_call(
        _small_kernel,
        grid=(b, l // tm),
        in_specs=[pl.BlockSpec((1, tm, d), lambda b_, i: (b_, i, 0)),
                  pl.BlockSpec((1, 1, d), _bmap(shift.shape[0])),
                  pl.BlockSpec((1, 1, d), _bmap(scale.shape[0])),
                  pl.BlockSpec((d, 128), lambda b_, i: (0, 0)),
                  pl.BlockSpec((32, d), lambda b_, i: (0, 0))],
        out_specs=[pl.BlockSpec((1, tm, 128), lambda b_, i: (b_, i, 0)),
                   pl.BlockSpec((1, 32, tm), lambda b_, i: (b_, 0, i))],
        out_shape=[jax.ShapeDtypeStruct((b, l, 128), F32), jax.ShapeDtypeStruct((b, 32, l), F32)],
        compiler_params=_cp(("parallel", "parallel")),
        name="small_proj",
    )(x, shift, scale, w_sm, w_smt)


def _conv_kernel(x_ref, w_ref, b_ref, o_ref, pad_scr, *, l, rc):
    tc = x_ref.shape[2]
    zeros8 = jnp.zeros((8, tc), F32)
    pad_scr[0:8, :] = zeros8
    pad_scr[l + 8:l + 16, :] = zeros8
    pad_scr[8:l + 8, :] = x_ref[0]
    half = CONV_K // 2
    for r0 in range(0, l, rc):
        acc = b_ref[...] + w_ref[0:1, :] * pad_scr[r0 + 8 - half:r0 + 8 - half + rc, :]
        for k in range(1, CONV_K):
            acc = acc + w_ref[k:k + 1, :] * pad_scr[r0 + 8 - half + k:r0 + 8 - half + k + rc, :]
        o_ref[0, r0:r0 + rc, :] = _silu(acc)


def _conv_silu(proj, col_off, width, w, bias):
    b, l, _ = proj.shape
    tc = 256
    cb0 = col_off // tc
    rc = min(512, l)
    return pl.pallas_call(
        functools.partial(_conv_kernel, l=l, rc=rc),
        grid=(b, width // tc),
        in_specs=[pl.BlockSpec((1, l, tc), lambda b_, j: (b_, 0, cb0 + j)),
                  pl.BlockSpec((8, tc), lambda b_, j: (0, j)),
                  pl.BlockSpec((1, tc), lambda b_, j: (0, j))],
        out_specs=pl.BlockSpec((1, l, tc), lambda b_, j: (b_, 0, j)),
        out_shape=jax.ShapeDtypeStruct((b, l, width), F32),
        scratch_shapes=[pltpu.VMEM((l + 16, tc), F32)],
        compiler_params=_cp(("parallel", "parallel")),
        name="conv_silu",
    )(proj, w, bias)


def _tri(n, lower):
    ii = lax.broadcasted_iota(jnp.int32, (n, n), 0)
    jj = lax.broadcasted_iota(jnp.int32, (n, n), 1)
    return (jj <= ii) if lower else (jj >= ii)


def _ssd_kernel(xf_ref, xb_ref, smf_ref, smb_ref, smtf_ref, smtb_ref, par_ref, part_ref, dx_ref, e_ref, s0_ref,
                yf_ref, yb_ref, sfin_ref, s_scr, *, nc):
    c = pl.program_id(1)

    @pl.when(c == 0)
    def _():
        s_scr[...] = s0_ref[0]

    q = SSD_CHUNK
    low = _tri(q, True)
    upp = _tri(q, False)
    lowf = low.astype(F32)
    uppf = upp.astype(F32)
    lane = lax.broadcasted_iota(jnp.int32, (1, 256), 1) // SSD_HEADDIM
    dirs = ((xf_ref, smf_ref, smtf_ref, yf_ref), (xb_ref, smb_ref, smtb_ref, yb_ref))
    for d, (x_ref, sm_ref, smt_ref, y_ref) in enumerate(dirs):
        mask = low if d == 0 else upp
        xbc = x_ref[0]
        xs = xbc[:, 0:512]
        dt = _softplus(sm_ref[0] + par_ref[0:1, :])
        la = dt * (-jnp.exp(par_ref[1:2, :]))
        cum = _dot(lowf if d == 0 else uppf, la, precision=HIGHEST)
        dtt = _softplus(smt_ref[0, 8 * d:8 * d + 8, :] + part_ref[8 * d:8 * d + 8, 0:1])
        lat = dtt * (-jnp.exp(part_ref[8 * d:8 * d + 8, 1:2]))
        cumt = _dot(lat, uppf if d == 0 else lowf, precision=HIGHEST)
        expand = e_ref[d]
        dtx = _dot_exact_rhs(dt, expand)
        cumx = _dot_exact_rhs(cum, expand)
        tot = cumx[q - 1:q, :] if d == 0 else cumx[0:1, :]
        xdt = xs * dtx
        ecum = jnp.exp(cumx)
        xw = (xdt * jnp.exp(tot - cumx)).astype(BF16)
        etot = jnp.exp(tot)
        ys = []
        for g in range(2):
            bg = xbc[:, 512 + 128 * g:640 + 128 * g]
            cg = xbc[:, 768 + 128 * g:896 + 128 * g].astype(BF16)
            sl = slice(256 * g, 256 * g + 256)
            s_old = s_scr[d, :, sl]
            scores = _dot_nt(cg, bg.astype(BF16))
            y = _dot(cg, s_old.astype(BF16)) * ecum[:, sl]
            xg = xdt[:, sl]
            for hh in range(4):
                h = 4 * g + hh
                col = cum[:, 8 * d + h:8 * d + h + 1]
                row = cumt[h:h + 1, :]
                dec = jnp.exp(jnp.where(mask, col - row, NEG_INF))
                xm = jnp.where(lane == hh, xg, 0.0).astype(BF16)
                y = y + _dot((scores * dec).astype(BF16), xm)
            ys.append(y)
            s_scr[d, :, sl] = s_old * etot[:, sl] + _dot(bg.T.astype(BF16), xw[:, sl])
        y = jnp.concatenate(ys, axis=1)
        if d == 0:
            y = y + dx_ref[...] * xs
        y_ref[0] = y

    @pl.when(c == nc - 1)
    def _():
        sfin_ref[0] = s_scr[...]


def _ssd(conv_x, small, small_t, par, par_t, dx, expand, s0):
    b, l, _ = conv_x.shape
    q = SSD_CHUNK
    nc = l // q
    fw = lambda b_, c: (b_, c, 0)
    bw = lambda b_, c: (b_, nc - 1 - c, 0)
    fwt = lambda b_, c: (b_, 0, c)
    bwt = lambda b_, c: (b_, 0, nc - 1 - c)
    full2 = lambda b_, c: (0, 0)
    return pl.pallas_call(
        functools.partial(_ssd_kernel, nc=nc),
        grid=(b, nc),
        in_specs=[pl.BlockSpec((1, q, 1024), fw), pl.BlockSpec((1, q, 1024), bw),
                  pl.BlockSpec((1, q, 128), fw), pl.BlockSpec((1, q, 128), bw),
                  pl.BlockSpec((1, 32, q), fwt), pl.BlockSpec((1, 32, q), bwt),
                  pl.BlockSpec((8, 128), full2), pl.BlockSpec((32, 128), full2),
                  pl.BlockSpec((1, 512), full2),
                  pl.BlockSpec((2, 128, 512), lambda b_, c: (0, 0, 0)),
                  pl.BlockSpec((1, 2, 128, 512), lambda b_, c: (b_, 0, 0, 0))],
        out_specs=[pl.BlockSpec((1, q, 512), fw), pl.BlockSpec((1, q, 512), bw),
                   pl.BlockSpec((1, 2, 128, 512), lambda b_, c: (b_, 0, 0, 0))],
        out_shape=[jax.ShapeDtypeStruct((b, l, 512), F32), jax.ShapeDtypeStruct((b, l, 512), F32),
                   jax.ShapeDtypeStruct((b, 2, 128, 512), F32)],
        scratch_shapes=[pltpu.VMEM((2, 128, 512), F32)],
        compiler_params=_cp(("parallel", "arbitrary")),
        name="ssd_scan",
    )(conv_x, conv_x, small, small, small_t, small_t, par, par_t, dx, expand, s0)


def _gdn_kernel(qf_ref, kf_ref, vf_ref, qb_ref, kb_ref, vb_ref, smf_ref, smb_ref, smtf_ref, smtb_ref,
                par_ref, part_ref, s0_ref, of_ref, ob_ref, sfin_ref, s_scr, *, nc):
    c = pl.program_id(1)

    @pl.when(c == 0)
    def _():
        s_scr[...] = s0_ref[0]

    cl = GDN_CHUNK
    nh = GDN_HEADS
    pk = nh * cl
    rows = 2 * cl
    ii = lax.broadcasted_iota(jnp.int32, (pk, pk), 0)
    jj = lax.broadcasted_iota(jnp.int32, (pk, pk), 1)
    same = (ii // cl) == (jj // cl)
    eye = (ii == jj).astype(F32)
    ri = lax.broadcasted_iota(jnp.int32, (rows, rows), 0)
    rj = lax.broadcasted_iota(jnp.int32, (rows, rows), 1)
    rsame = (ri // cl) == (rj // cl)
    zpad = jnp.zeros((cl, GDN_DK), F32)
    dirs = ((qf_ref, kf_ref, vf_ref, smf_ref, smtf_ref, of_ref), (qb_ref, kb_ref, vb_ref, smb_ref, smtb_ref, ob_ref))
    heads = range(nh)
    probs = []
    for d, (q_ref, k_ref, v_ref, sm_ref, smt_ref, o_ref) in enumerate(dirs):
        incl = same & ((jj <= ii) if d == 0 else (jj >= ii))
        strict = same & ((jj < ii) if d == 0 else (jj > ii))
        tri = (rsame & ((rj <= ri) if d == 0 else (rj >= ri))).astype(F32)
        tri_t = (rsame & ((ri <= rj) if d == 0 else (ri >= rj))).astype(F32)
        sm = sm_ref[0]
        beta_all = _sigmoid(sm)
        g_all = -jnp.exp(par_ref[2:3, :]) * _softplus(sm + par_ref[3:4, :])
        gt8 = -jnp.exp(part_ref[24:32, 2:3]) * _softplus(smt_ref[0, 24:32, :] + part_ref[24:32, 3:4])
        cum_all = _dot(tri, g_all, precision=HIGHEST)
        cumt_all = _dot(gt8, tri_t, precision=HIGHEST)
        for sc in ((0, 1) if d == 0 else (1, 0)):
            rs = slice(cl * sc, cl * sc + cl)
            stack = lambda ref: jnp.concatenate([ref[0, rs, GDN_DK * h:GDN_DK * h + GDN_DK] for h in heads], axis=0)
            qp, kp, vp = stack(q_ref), stack(k_ref), stack(v_ref)
            qp = qp * lax.rsqrt(jnp.sum(qp * qp, axis=-1, keepdims=True) + 1e-6) * (GDN_DK ** -0.5)
            kp = kp * lax.rsqrt(jnp.sum(kp * kp, axis=-1, keepdims=True) + 1e-6)
            col = jnp.concatenate([cum_all[rs, SM_A + 4 * d + h:SM_A + 4 * d + h + 1] for h in heads], axis=0)
            row = jnp.concatenate([cumt_all[4 * d + h:4 * d + h + 1, rs] for h in heads], axis=1)
            beta = jnp.concatenate([beta_all[rs, SM_BETA + 4 * d + h:SM_BETA + 4 * d + h + 1] for h in heads], axis=0)
            edge = cl - 1 if d == 0 else 0
            tot = jnp.concatenate([jnp.broadcast_to(col[cl * h + edge:cl * h + edge + 1, :], (cl, 1)) for h in heads],
                                  axis=0)
            dec = jnp.exp(jnp.where(incl, col - row, NEG_INF))
            kb = kp.astype(BF16)
            a_mat = jnp.where(strict, _dot_nt(kb, kb) * dec * beta, 0.0)
            ecol = jnp.exp(col)
            probs.append(dict(
                d=d, rs=rs, o_ref=o_ref, t=eye - a_mat, p=a_mat,
                rhs=jnp.concatenate([vp * beta, kp * (beta * ecol)], axis=1),
                attn=(_dot_nt(qp.astype(BF16), kb) * dec).astype(BF16),
                qg=qp * ecol, kdec=kp * jnp.exp(tot - col), etot=jnp.exp(tot)))
    for _ in range(5):
        for pb in probs:
            pbf = pb['p'].astype(BF16)
            pb['p'] = _dot(pbf, pbf)
        for pb in probs:
            pb['t'] = pb['t'] + _dot(pb['t'].astype(BF16), pb['p'].astype(BF16))
    for pb in probs:
        pb['uw'] = _dot3(pb['t'], pb['rhs'])
    for step in range(2):
        for pb in (probs[step], probs[2 + step]):
            d, rs, uw = pb['d'], pb['rs'], pb['uw']
            vnew, qs = [], []
            for h in heads:
                hr = slice(cl * h, cl * h + cl)
                wq = jnp.concatenate([uw[hr, GDN_DK:], pb['qg'][hr]], axis=0).astype(BF16)
                ws_qs = _dot(wq, s_scr[d, h].astype(BF16))
                vnew.append(uw[hr, :GDN_DK] - ws_qs[0:cl])
                qs.append(ws_qs[cl:])
            vn = jnp.concatenate(vnew, axis=0)
            out = jnp.concatenate(qs, axis=0) + _dot(pb['attn'], vn.astype(BF16))
            for h in heads:
                hr = slice(cl * h, cl * h + cl)
                pb['o_ref'][0, rs, GDN_DK * h:GDN_DK * h + GDN_DK] = out[hr]
                kd_t = jnp.concatenate([pb['kdec'][hr], zpad], axis=0).T.astype(BF16)
                vn_pad = jnp.concatenate([vnew[h], zpad], axis=0).astype(BF16)
                s_scr[d, h] = s_scr[d, h] * pb['etot'][cl * h:cl * h + 1, :] + _dot(kd_t, vn_pad)

    @pl.when(c == nc - 1)
    def _():
        sfin_ref[0] = s_scr[...]


def _gdn(conv_g, small, small_t, par, par_t, s0):
    b, l, _ = conv_g.shape
    rows = 2 * GDN_CHUNK
    nc = l // rows
    fwm = lambda j: (lambda b_, c: (b_, c, j))
    bwm = lambda j: (lambda b_, c: (b_, nc - 1 - c, j))
    fw, bw = fwm(0), bwm(0)
    fwt = lambda b_, c: (b_, 0, c)
    bwt = lambda b_, c: (b_, 0, nc - 1 - c)
    full2 = lambda b_, c: (0, 0)
    st = lambda b_, c: (b_, 0, 0, 0, 0)
    return pl.pallas_call(
        functools.partial(_gdn_kernel, nc=nc),
        grid=(b, nc),
        in_specs=[pl.BlockSpec((1, rows, 512), fwm(0)), pl.BlockSpec((1, rows, 512), fwm(1)),
                  pl.BlockSpec((1, rows, 512), fwm(2)),
                  pl.BlockSpec((1, rows, 512), bwm(0)), pl.BlockSpec((1, rows, 512), bwm(1)),
                  pl.BlockSpec((1, rows, 512), bwm(2)),
                  pl.BlockSpec((1, rows, 128), fw), pl.BlockSpec((1, rows, 128), bw),
                  pl.BlockSpec((1, 32, rows), fwt), pl.BlockSpec((1, 32, rows), bwt),
                  pl.BlockSpec((8, 128), full2), pl.BlockSpec((32, 128), full2),
                  pl.BlockSpec((1, 2, GDN_HEADS, 128, 128), st)],
        out_specs=[pl.BlockSpec((1, rows, 512), fw), pl.BlockSpec((1, rows, 512), bw),
                   pl.BlockSpec((1, 2, GDN_HEADS, 128, 128), st)],
        out_shape=[jax.ShapeDtypeStruct((b, l, 512), F32), jax.ShapeDtypeStruct((b, l, 512), F32),
                   jax.ShapeDtypeStruct((b, 2, GDN_HEADS, 128, 128), F32)],
        scratch_shapes=[pltpu.VMEM((2, GDN_HEADS, 128, 128), F32)],
        compiler_params=_cp(("parallel", "arbitrary")),
        name="gdn_scan",
    )(conv_g, conv_g, conv_g, conv_g, conv_g, conv_g, small, small, small_t, small_t, par, par_t, s0)


def _pool_body(x_ref, w_ref, sc_ref, o_ref, pa, pb, *, l, gw, two_d, win, rc):
    lo = win // 2
    hi = win - 1 - lo
    nrow = l // gw
    shift = int(math.log2(gw))
    pbw = 8 * gw
    z8 = jnp.zeros((8, 128), F32)
    pa[0:8, :] = z8
    pa[l + 8:l + 16, :] = z8
    pa[8:l + 8, :] = x_ref[0]
    if two_d:
        zb = jnp.zeros((pbw, 128), F32)
        pb[0:pbw, :] = zb
        pb[pbw + l:pbw + l + pbw, :] = zb

    def finish(r0, acc, pos):
        col = pos & (gw - 1)
        cnt = (jnp.minimum(col + hi, gw - 1) - jnp.maximum(col - lo, 0) + 1).astype(F32)
        if two_d:
            row = pos >> shift
            cnt = cnt * (jnp.minimum(row + hi, nrow - 1) - jnp.maximum(row - lo, 0) + 1).astype(F32)
        dlt = (acc / cnt - x_ref[0, r0:r0 + rc, :]).astype(BF16)
        o_ref[0, r0:r0 + rc, :] = _dot(dlt, w_ref[0]) * sc_ref[...]

    for r0 in range(0, l, rc):
        pos = lax.broadcasted_iota(jnp.int32, (rc, 128), 0) + r0
        col = pos & (gw - 1)
        acc = pa[8 + r0:8 + r0 + rc, :]
        for j in range(-lo, hi + 1):
            if j == 0:
                continue
            v = pa[8 + r0 + j:8 + r0 + j + rc, :]
            ok = (col + j >= 0) if j < 0 else (col + j < gw)
            acc = acc + jnp.where(ok, v, 0.0)
        if two_d:
            pb[pbw + r0:pbw + r0 + rc, :] = acc
        else:
            finish(r0, acc, pos)
    if two_d:
        for r0 in range(0, l, rc):
            pos = lax.broadcasted_iota(jnp.int32, (rc, 128), 0) + r0
            acc = pb[pbw + r0:pbw + r0 + rc, :]
            for j in range(-lo, hi + 1):
                if j != 0:
                    acc = acc + pb[pbw + r0 + gw * j:pbw + r0 + gw * j + rc, :]
            finish(r0, acc, pos)


def _pool_kernel(x_ref, w_ref, sc_ref, o_ref, pa, pb, **kw):
    g = pl.program_id(1)
    for gi, win in enumerate(POOL_WINDOWS):
        @pl.when(g == gi)
        def _(win=win):
            _pool_body(x_ref, w_ref, sc_ref, o_ref, pa, pb, win=win, **kw)


def _pool(proj, pool_w, pool_scale, two_d):
    b, l, _ = proj.shape
    gw = GRID_W if two_d else l
    assert gw & (gw - 1) == 0 and l % gw == 0
    rc = min(512, l)
    cb0 = OFF_POOL // 128
    pb_rows = l + 16 * gw if two_d else 8
    return pl.pallas_call(
        functools.partial(_pool_kernel, l=l, gw=gw, two_d=two_d, rc=rc),
        grid=(b, len(POOL_WINDOWS)),
        in_specs=[pl.BlockSpec((1, l, 128), lambda b_, g: (b_, 0, cb0 + g)),
                  pl.BlockSpec((1, 128, 128), lambda b_, g: (g, 0, 0)),
                  pl.BlockSpec((1, 128), lambda b_, g: (0, g))],
        out_specs=pl.BlockSpec((1, l, 128), lambda b_, g: (b_, 0, g)),
        out_shape=jax.ShapeDtypeStruct((b, l, BRANCH), F32),
        scratch_shapes=[pltpu.VMEM((l + 16, 128), F32), pltpu.VMEM((pb_rows, 128), F32)],
        compiler_params=_cp(("parallel", "parallel")),
        name="pool_branch",
    )(proj, pool_w, pool_scale)


def _four1_kernel(x_ref, cc_ref, ss_ref, xc_ref, xs_ref):
    x = x_ref[0]
    xc_ref[0] = _dot3(x, cc_ref[...]).astype(BF16)
    xs_ref[0] = _dot3(x, ss_ref[...]).astype(BF16)


def _four2_kernel(ct_ref, st_ref, xc_ref, xs_ref, o_ref):
    o_ref[0] = _dot(ct_ref[...], xc_ref[0]) + _dot(st_ref[...], xs_ref[0])


def _fourier(proj, tabs):
    cc, ss, cl_, sl_ = tabs
    b, l, _ = proj.shape
    tm = min(512, l)
    cb = OFF_FOUR // BRANCH
    xc, xs = pl.pallas_call(
        _four1_kernel,
        grid=(b, l // tm),
        in_specs=[pl.BlockSpec((1, tm, BRANCH), lambda b_, i: (b_, i, cb)),
                  pl.BlockSpec((BRANCH, BRANCH), lambda b_, i: (0, 0)),
                  pl.BlockSpec((BRANCH, BRANCH), lambda b_, i: (0, 0))],
        out_specs=[pl.BlockSpec((1, tm, BRANCH), lambda b_, i: (b_, i, 0))] * 2,
        out_shape=[jax.ShapeDtypeStruct((b, l, BRANCH), BF16)] * 2,
        compiler_params=_cp(("parallel", "parallel")),
        name="dft_channels",
    )(proj, cc, ss)
    tr = min(256, l)
    return pl.pallas_call(
        _four2_kernel,
        grid=(b, l // tr),
        in_specs=[pl.BlockSpec((tr, l), lambda b_, i: (i, 0)),
                  pl.BlockSpec((tr, l), lambda b_, i: (i, 0)),
                  pl.BlockSpec((1, l, BRANCH), lambda b_, i: (b_, 0, 0)),
                  pl.BlockSpec((1, l, BRANCH), lambda b_, i: (b_, 0, 0))],
        out_specs=pl.BlockSpec((1, tr, BRANCH), lambda b_, i: (b_, i, 0)),
        out_shape=jax.ShapeDtypeStruct((b, l, BRANCH), F32),
        compiler_params=_cp(("parallel", "parallel")),
        name="dft_positions",
    )(cl_, sl_, xc, xs)


def _dft_tables(l):
    c = BRANCH // 4
    k = jnp.arange(c, dtype=jnp.int32)
    ang = (2.0 * math.pi / c) * ((k[:, None] * k[None, :]) % c).astype(F32)
    scale = 1.0 / math.sqrt(l * c)
    eye4 = jnp.eye(4, dtype=F32)
    cc = jnp.kron(eye4, jnp.cos(ang) * scale)
    ss = jnp.kron(eye4, jnp.sin(ang) * scale)
    t = jnp.arange(l, dtype=jnp.int32)
    angl = (2.0 * math.pi / l) * ((t[:, None] * t[None, :]) % l).astype(F32)
    return cc, ss, jnp.cos(angl).astype(BF16), (-jnp.sin(angl)).astype(BF16)


def _merge_kernel(x_ref, g1_ref, lng_ref, lnb_ref, gates_ref, pool_ref, four_ref, yf_ref, yb_ref, sz_ref,
                  of_ref, ob_ref, gz_ref, snw_ref, gnw_ref, wbr_ref, wout_ref, o_ref, *, alpha):
    d = x_ref.shape[2]
    ssd_g = (yf_ref[0] + yb_ref[0]) * _silu(sz_ref[0])
    parts = []
    for g in range(2):
        blk = ssd_g[:, 256 * g:256 * g + 256]
        parts.append(blk * lax.rsqrt(jnp.mean(blk * blk, axis=-1, keepdims=True) + LN_EPS))
    ssd_out = jnp.concatenate(parts, axis=1) * snw_ref[...]
    o = of_ref[0] + ob_ref[0]
    parts = []
    for h in range(GDN_HEADS):
        blk = o[:, 128 * h:128 * h + 128]
        parts.append(blk * lax.rsqrt(jnp.mean(blk * blk, axis=-1, keepdims=True) + LN_EPS))
    gdn_out = jnp.concatenate(parts, axis=1) * gnw_ref[...] * _silu(gz_ref[0])
    merged = None
    for i, br in enumerate((pool_ref[0], four_ref[0], ssd_out, gdn_out)):
        term = _sigmoid(gates_ref[0, :, d * i:d * i + d]) * _dot(br.astype(BF16), wbr_ref[i])
        merged = term if merged is None else merged + term
    mix = _dot(merged.astype(BF16), wout_ref[...])
    y = alpha * x_ref[0] + g1_ref[0] * mix
    o_ref[0] = _ln(y) * lng_ref[...] + lnb_ref[...]


def _merge(x, gate1, ln_g, ln_b, proj, pool_o, four_o, yf, yb, of, ob, snw, gnw, wbr, wout, alpha):
    b, l, d = x.shape
    tm = min(256, l)
    row = lambda b_, i: (b_, i, 0)
    colb = lambda j: (lambda b_, i: (b_, i, j))
    vec = lambda b_, i: (0, 0)
    br = pl.BlockSpec((1, tm, BRANCH), row)
    return pl.pallas_call(
        functools.partial(_merge_kernel, alpha=alpha),
        grid=(b, l // tm),
        in_specs=[pl.BlockSpec((1, tm, d), row),
                  pl.BlockSpec((1, 1, d), _bmap(gate1.shape[0])),
                  pl.BlockSpec((1, d), vec), pl.BlockSpec((1, d), vec),
                  pl.BlockSpec((1, tm, 4 * d), colb(OFF_GATES // (4 * d))),
                  br, br, br, br,
                  pl.BlockSpec((1, tm, BRANCH), colb(OFF_SZ // BRANCH)),
                  br, br,
                  pl.BlockSpec((1, tm, BRANCH), colb(OFF_GZ // BRANCH)),
                  pl.BlockSpec((1, BRANCH), vec), pl.BlockSpec((1, BRANCH), vec),
                  pl.BlockSpec((4, BRANCH, d), lambda b_, i: (0, 0, 0)),
                  pl.BlockSpec((d, d), vec)],
        out_specs=pl.BlockSpec((1, tm, d), row),
        out_shape=jax.ShapeDtypeStruct((b, l, d), F32),
        compiler_params=_cp(("parallel", "parallel")),
        name="branch_merge",
    )(x, gate1, ln_g, ln_b, proj, pool_o, four_o, yf, yb, proj, of, ob, proj, snw, gnw, wbr, wout)


def _sort_network(n):
    pairs = []

    def merge(lo, hi, r):
        step = r * 2
        if step < hi - lo:
            merge(lo, hi, step)
            merge(lo + r, hi, step)
            pairs.extend((i, i + r) for i in range(lo + r, hi - r, step))
        else:
            pairs.append((lo, lo + r))

    def sort(lo, hi):
        if hi - lo >= 1:
            mid = lo + (hi - lo) // 2
            sort(lo, mid)
            sort(mid + 1, hi)
            merge(lo, hi, 1)

    sort(0, n - 1)
    return pairs


def _top_desc(x, n):
    nrow = x.shape[0] // 8
    lst = [x[8 * r:8 * r + 8, :] for r in range(nrow)]
    npad = 1 << (nrow - 1).bit_length()
    lst += [jnp.full(lst[0].shape, NEG_INF, F32)] * (npad - nrow)
    for i, j in _sort_network(npad):
        lst[i], lst[j] = jnp.maximum(lst[i], lst[j]), jnp.minimum(lst[i], lst[j])
    lst = lst[:nrow]
    sub = lax.broadcasted_iota(jnp.int32, lst[0].shape, 0).astype(F32)
    out = []
    for r in range(n):
        m = jnp.max(lst[0], axis=0, keepdims=True)
        out.append(m)
        left = n - r - 1
        if left:
            win = jnp.min(jnp.where(lst[0] == m, sub, 8.0), axis=0, keepdims=True)
            winner = sub == win
            for j in range(min(nrow, left)):
                nxt = lst[j + 1] if j + 1 < nrow else NEG_INF
                lst[j] = jnp.where(winner, nxt, lst[j])
    return out


def _peer_kernel(x_ref, sh_ref, sc_ref, g2_ref, lng_ref, lnb_ref, wqt_ref, keys_ref, u_ref, vt_ref, o_ref,
                 h_scr, qt_scr, th_scr, a1_scr, s2_scr, cand_scr, act_scr, p_scr, acc_scr,
                 *, n_pair, n_eb, alpha):
    eb = pl.program_id(2)
    t = x_ref.shape[1]
    ntc = t // 128
    k1 = PEER_TOPK + 1
    pairs = [(a, b) for a in range(k1) for b in range(k1) if (a + 1) * (b + 1) <= k1]

    @pl.when(eb == 0)
    def _():
        h = (_ln(x_ref[0]) * (1.0 + sc_ref[0]) + sh_ref[0]).T.astype(BF16)
        h_scr[...] = h
        qt_scr[...] = _dot(wqt_ref[...], h).astype(BF16)
        acc_scr[...] = jnp.zeros(acc_scr.shape, F32)
        for r in range(len(pairs), cand_scr.shape[0]):
            cand_scr[r:r + 1, :] = jnp.full((1, 128), NEG_INF, F32)

        def head_body(hd, carry):
            for tc in range(ntc):
                ts = slice(128 * tc, 128 * tc + 128)
                sv, tops = [], []
                for s in range(2):
                    hs = 2 * hd + s
                    q_rows = qt_scr[pl.ds(pl.multiple_of(hs * 128, 128), 128), ts]
                    st = _dot(keys_ref[hs], q_rows)
                    sv.append(st)
                    tops.append(_top_desc(st, k1))
                v1, v2 = tops
                for r, (a, b) in enumerate(pairs):
                    cand_scr[r:r + 1, :] = v1[a] + v2[b]
                best = _top_desc(cand_scr[...], k1)
                z = None
                for kk in range(PEER_TOPK):
                    term = jnp.exp(best[kk] - best[0])
                    z = term if z is None else z + term
                thr = 0.5 * (best[PEER_TOPK - 1] + best[PEER_TOPK])
                th_scr[hd, tc] = (thr - sv[0]) * LOG2E
                a1_scr[hd, tc] = (sv[0] - best[0]) * LOG2E - jnp.log(z) * LOG2E
                s2_scr[hd, tc] = sv[1] * LOG2E
            return carry

        lax.fori_loop(0, PEER_HEADS, head_body, 0)

    pr = 2 * PEER_NKEYS
    act = _gelu(_dot(u_ref[...], h_scr[...]))
    for tc in range(ntc):
        act_scr[tc] = act[:, 128 * tc:128 * tc + 128]
    for k in range(n_pair):
        if k:
            acc_scr[...] += _dot(vt_ref[k - 1], p_scr[(k - 1) % 2])
        i1 = eb * (2 * n_pair) + 2 * k
        for j in range(2):
            js = slice(PEER_NKEYS * j, PEER_NKEYS * j + PEER_NKEYS)
            for tc in range(ntc):
                w = None
                for hd in range(PEER_HEADS):
                    th = th_scr[hd, tc, pl.ds(i1 + j, 1), :]
                    a1 = a1_scr[hd, tc, pl.ds(i1 + j, 1), :]
                    s2 = s2_scr[hd, tc]
                    term = jnp.where(s2 > th, jnp.exp2(s2 + a1), 0.0)
                    w = term if w is None else w + term
                rows = slice(pr * k + PEER_NKEYS * j, pr * k + PEER_NKEYS * j + PEER_NKEYS)
                p_scr[k % 2, js, 128 * tc:128 * tc + 128] = (w * act_scr[tc, rows, :]).astype(BF16)
    acc_scr[...] += _dot(vt_ref[n_pair - 1], p_scr[(n_pair - 1) % 2])

    @pl.when(eb == n_eb - 1)
    def _():
        y = alpha * x_ref[0] + g2_ref[0] * acc_scr[...].T
        o_ref[0] = _ln(y) * lng_ref[...] + lnb_ref[...]


def _peer_ffn(x, shift, scale, gate2, ln_g, ln_b, wqt, keys, u_tab, v_tab_t, alpha):
    b, l, d = x.shape
    t = min(512, l)
    n_exp = u_tab.shape[0]
    n_pair = 4
    pr = 2 * PEER_NKEYS
    be = n_pair * pr
    n_eb = n_exp // be
    nh2 = 2 * PEER_HEADS
    k1 = PEER_TOPK + 1
    npairs = sum(1 for a in range(k1) for b_ in range(k1) if (a + 1) * (b_ + 1) <= k1)
    ncand = -(-npairs // 8) * 8
    vec = lambda b_, i, e: (0, 0)
    per_head = pltpu.VMEM((PEER_HEADS, t // 128, PEER_NKEYS, 128), F32)
    return pl.pallas_call(
        functools.partial(_peer_kernel, n_pair=n_pair, n_eb=n_eb, alpha=alpha),
        grid=(b, l // t, n_eb),
        in_specs=[pl.BlockSpec((1, t, d), lambda b_, i, e: (b_, i, 0)),
                  pl.BlockSpec((1, 1, d), _bmap(shift.shape[0])),
                  pl.BlockSpec((1, 1, d), _bmap(scale.shape[0])),
                  pl.BlockSpec((1, 1, d), _bmap(gate2.shape[0])),
                  pl.BlockSpec((1, d), vec), pl.BlockSpec((1, d), vec),
                  pl.BlockSpec((nh2 * 128, d), vec),
                  pl.BlockSpec((nh2, PEER_NKEYS, 128), lambda b_, i, e: (0, 0, 0)),
                  pl.BlockSpec((be, d), lambda b_, i, e: (e, 0)),
                  pl.BlockSpec((n_pair, d, pr), lambda b_, i, e: (e, 0, 0))],
        out_specs=pl.BlockSpec((1, t, d), lambda b_, i, e: (b_, i, 0)),
        out_shape=jax.ShapeDtypeStruct((b, l, d), F32),
        scratch_shapes=[pltpu.VMEM((d, t), BF16), pltpu.VMEM((nh2 * 128, t), BF16),
                        per_head, per_head, per_head,
                        pltpu.VMEM((ncand, 128), F32), pltpu.VMEM((t // 128, be, 128), F32),
                        pltpu.VMEM((2, pr, t), BF16),
                        pltpu.VMEM((d, t), F32)],
        compiler_params=_cp(("parallel", "parallel", "arbitrary")),
        name="peer_ffn",
    )(x, shift, scale, gate2, ln_g, ln_b, wqt, keys, u_tab, v_tab_t)


def _prep_layer_weights(w_in, pool_scale, ssd_conv_w, ssd_conv_b, ssd_dt_bias, ssd_a_log, ssd_d, ssd_norm_w,
                        gdn_conv_w, gdn_dt_bias, gdn_a_log, gdn_norm_w, peer_wq, peer_keys, peer_u, peer_v):
    nl, d, _ = w_in.shape
    seg = lambda a, b_: w_in[:, :, a:b_]
    main = jnp.concatenate([seg(4640, 8736), seg(0, 512), seg(512, 1024), seg(1024, 2048), seg(2576, 4112),
                            seg(2048, 2560), seg(4112, 4624)], axis=-1).astype(BF16)
    small = jnp.concatenate([seg(2560, 2576), seg(4624, 4632), seg(4632, 4640)], axis=-1)
    w_sm = jnp.pad(small, ((0, 0), (0, 0), (0, 96))).astype(BF16)
    w_smt = jnp.swapaxes(small, 1, 2).astype(BF16)
    conv_s = jnp.pad(ssd_conv_w, ((0, 0), (0, 8 - CONV_K), (0, 0)))
    conv_g = jnp.pad(gdn_conv_w, ((0, 0), (0, 8 - CONV_K), (0, 0)))
    zeros = lambda n: jnp.zeros((nl, n), F32)
    bias_row = jnp.concatenate([ssd_dt_bias.reshape(nl, 16), zeros(112)], axis=1)
    alog_row = jnp.concatenate([ssd_a_log.reshape(nl, 16), zeros(112)], axis=1)
    galog_row = jnp.concatenate([zeros(24), gdn_a_log.reshape(nl, 8), zeros(96)], axis=1)
    gbias_row = jnp.concatenate([zeros(24), gdn_dt_bias.reshape(nl, 8), zeros(96)], axis=1)
    par = jnp.stack([bias_row, alog_row, galog_row, gbias_row] + [zeros(128)] * 4, axis=1)
    par_t = jnp.pad(jnp.swapaxes(par[:, 0:4, 0:32], 1, 2), ((0, 0), (0, 0), (0, 124)))
    dx = jnp.repeat(ssd_d, SSD_HEADDIM, axis=1).reshape(nl, 1, BRANCH)
    return dict(
        main=main, w_sm=w_sm, w_smt=w_smt, conv_s=conv_s, conv_sb=ssd_conv_b.reshape(nl, 1, -1),
        conv_g=conv_g, conv_gb=jnp.zeros((nl, 1, 3 * BRANCH), F32), par=par, par_t=par_t, dx=dx,
        pool_scale=pool_scale.reshape(nl, 1, BRANCH), snw=ssd_norm_w.reshape(nl, 1, BRANCH),
        gnw=jnp.tile(gdn_norm_w, (1, GDN_HEADS)).reshape(nl, 1, BRANCH),
        wqt=jnp.swapaxes(peer_wq, 1, 2).astype(BF16),
        keys=peer_keys.reshape(nl, 2 * PEER_HEADS, PEER_NKEYS, -1).astype(BF16),
        u=peer_u.astype(BF16),
        vt=jnp.swapaxes(peer_v.reshape(nl, -1, 2 * PEER_NKEYS, d), 2, 3).astype(BF16))


def _expand_table():
    lane = jnp.arange(128)[:, None]
    col = jnp.arange(BRANCH)[None, :] // SSD_HEADDIM
    return jnp.stack([(lane == col), (lane == col + SSD_HEADS)]).astype(F32)


def _scans(x, shift, scale, lw, l_idx, expand, states):
    proj = _lnmod_mm(_fold_rows(x, shift.shape[0] == 1, 1024), shift, scale, lw['main'][l_idx], 512)
    proj = proj.reshape(x.shape[0], x.shape[1], -1)
    small, small_t = _small_proj(x, shift, scale, lw['w_sm'][l_idx], lw['w_smt'][l_idx])
    conv_s = _conv_silu(proj, OFF_XBC, 1024, lw['conv_s'][l_idx], lw['conv_sb'][l_idx])
    conv_g = _conv_silu(proj, OFF_QKV, 1536, lw['conv_g'][l_idx], lw['conv_gb'][l_idx])
    yf, yb, s_ssd = _ssd(conv_s, small, small_t, lw['par'][l_idx], lw['par_t'][l_idx], lw['dx'][l_idx], expand,
                         states[0])
    of, ob, s_gdn = _gdn(conv_g, small, small_t, lw['par'][l_idx], lw['par_t'][l_idx], states[1])
    return proj, (yf, yb, of, ob), (s_ssd, s_gdn)


def _mixer(x, mods, lw, l_idx, expand, tabs, states, pool_w, w_branch, w_out, ln_g, ln_b, two_d, alpha):
    proj, (yf, yb, of, ob), new_states = _scans(x, mods[0], mods[1], lw, l_idx, expand, states)
    pool_o = _pool(proj, pool_w, lw['pool_scale'][l_idx], two_d)
    four_o = _fourier(proj, tabs)
    x = _merge(x, mods[2], ln_g, ln_b, proj, pool_o, four_o, yf, yb, of, ob, lw['snw'][l_idx], lw['gnw'][l_idx],
               w_branch, w_out, alpha)
    return x, new_states


def _fold_rows(x, shared_mod, rows):
    b, l, d = x.shape
    if shared_mod and l < rows and rows % l == 0 and b % (rows // l) == 0:
        return x.reshape(b * l // rows, rows, d)
    return x


def _peer(x, mods, lw, l_idx, ln_g, ln_b, alpha):
    xr = _fold_rows(x, mods[3].shape[0] == 1, 512)
    y = _peer_ffn(xr, mods[3], mods[4], mods[5], ln_g, ln_b, lw['wqt'][l_idx], lw['keys'][l_idx], lw['u'][l_idx],
                  lw['vt'][l_idx], alpha)
    return y.reshape(x.shape)


def kernel(x, c, ctx, c_ctx, w_mod, b_mod, w_in, pool_w, pool_scale, ssd_conv_w, ssd_conv_b, ssd_dt_bias, ssd_a_log, ssd_d, ssd_norm_w, gdn_conv_w, gdn_dt_bias, gdn_a_log, gdn_norm_w, w_branch, w_out, ln1_g, ln1_b, peer_wq, peer_keys, peer_u, peer_v, ln2_g, ln2_b):
    bsz, seq, d = x.shape
    nl = w_in.shape[0]
    alpha = (2 * nl) ** 0.25
    lw = _prep_layer_weights(w_in, pool_scale, ssd_conv_w, ssd_conv_b, ssd_dt_bias, ssd_a_log, ssd_d, ssd_norm_w,
                             gdn_conv_w, gdn_dt_bias, gdn_a_log, gdn_norm_w, peer_wq, peer_keys, peer_u, peer_v)
    pool_wb = pool_w.astype(BF16)
    w_branch_b = w_branch.astype(BF16)
    w_out_b = w_out.astype(BF16)
    expand = _expand_table()
    tabs_x = _dft_tables(seq)
    tabs_c = _dft_tables(ctx.shape[1])
    cs = jnp.concatenate([c, c_ctx[None], jnp.zeros((8 - bsz - 1, d), F32)], axis=0)
    mod_all = _mod_all(cs, w_mod.astype(BF16), b_mod).reshape(nl, 8, 6, 1, d)
    zero_states = (jnp.zeros((bsz, 2, SSD_STATE, BRANCH), F32), jnp.zeros((bsz, 2, GDN_HEADS, GDN_DK, GDN_DK), F32))
    vec = lambda a, l_idx: a[l_idx].reshape(1, d)
    for l_idx in range(nl):
        mods_x = [mod_all[l_idx, :bsz, i] for i in range(6)]
        mods_c = [mod_all[l_idx, bsz:bsz + 1, i] for i in range(6)]
        g1, b1, g2, b2 = vec(ln1_g, l_idx), vec(ln1_b, l_idx), vec(ln2_g, l_idx), vec(ln2_b, l_idx)
        if l_idx == nl - 1:
            _, _, states = _scans(ctx, mods_c[0], mods_c[1], lw, l_idx, expand, zero_states)
        else:
            ctx, states = _mixer(ctx, mods_c, lw, l_idx, expand, tabs_c, zero_states, pool_wb[l_idx],
                                 w_branch_b[l_idx], w_out_b[l_idx], g1, b1, False, alpha)
            ctx = _peer(ctx, mods_c, lw, l_idx, g2, b2, alpha)
        x, _ = _mixer(x, mods_x, lw, l_idx, expand, tabs_x, states, pool_wb[l_idx], w_branch_b[l_idx],
                      w_out_b[l_idx], g1, b1, True, alpha)
        x = _peer(x, mods_x, lw, l_idx, g2, b2, alpha)
    return x
```

```python
import functools
import math

import jax
import jax.numpy as jnp
from jax import lax
from jax.experimental import pallas as pl
from jax.experimental.pallas import tpu as pltpu

F32 = jnp.float32
BF16 = jnp.bfloat16
HIGHEST = lax.Precision.HIGHEST
PROJ_DT = jnp.bfloat16

LN_EPS = 1e-6
GRID_W = 64
BRANCH = 512
POOL_WINDOWS = (2, 4, 8, 16)
SSD_HEADS = 8
SSD_HEADDIM = 64
SSD_STATE = 128
SSD_CHUNK = 128
GDN_HEADS = 4
GDN_DK = 128
GDN_CHUNK = 64
CONV_K = 5
PEER_HEADS = 8
PEER_NKEYS = 128
PEER_TOPK = 16
N_MAIN = 8704
OFF_GATES, OFF_POOL, OFF_FOUR, OFF_XBC, OFF_QKV, OFF_SZ, OFF_GZ = 0, 4096, 4608, 5120, 6144, 7680, 8192
SM_DT, SM_BETA, SM_A = 0, 16, 24
VMEM_LIMIT = 52 * 1024 * 1024
NEG_INF = float("-inf")
LOG2E = 1.4426950408889634


def _cp(sem):
    return pltpu.CompilerParams(dimension_semantics=sem, vmem_limit_bytes=VMEM_LIMIT)


def _sigmoid(x):
    return 1.0 / (1.0 + jnp.exp(-x))


def _silu(x):
    return x * _sigmoid(x)


def _softplus(x):
    return jnp.maximum(x, 0.0) + jnp.log(1.0 + jnp.exp(-jnp.abs(x)))


def _ln(x):
    mu = jnp.mean(x, axis=-1, keepdims=True)
    xc = x - mu
    var = jnp.mean(xc * xc, axis=-1, keepdims=True)
    return xc * lax.rsqrt(var + LN_EPS)


def _dot(a, b, **kw):
    return jnp.dot(a, b, preferred_element_type=F32, **kw)


def _dot_nt(a, b, **kw):
    return lax.dot_general(a, b, (((1,), (1,)), ((), ())), preferred_element_type=F32, **kw)


def _dot3(a, b):
    a_hi = a.astype(BF16)
    b_hi = b.astype(BF16)
    a_lo = (a - a_hi.astype(F32)).astype(BF16)
    b_lo = (b - b_hi.astype(F32)).astype(BF16)
    return _dot(a_hi, b_hi) + (_dot(a_hi, b_lo) + _dot(a_lo, b_hi))


def _dot_exact_rhs(a, b):
    bb = b.astype(BF16)
    a1 = a.astype(BF16)
    r1 = a - a1.astype(F32)
    a2 = r1.astype(BF16)
    a3 = (r1 - a2.astype(F32)).astype(BF16)
    return _dot(a1, bb) + (_dot(a2, bb) + _dot(a3, bb))


def _gelu(x):
    return 0.5 * x * (1.0 + lax.erf(x * (2.0 ** -0.5)))


def _mod_kernel(c_ref, w_ref, b_ref, o_ref):
    s = _silu(c_ref[...])
    o_ref[0] = _dot(s.astype(BF16), w_ref[0]) + b_ref[0]


def _mod_all(cs, w_mod, b_mod):
    nl, d, n6 = w_mod.shape
    tn = 1536
    return pl.pallas_call(
        _mod_kernel,
        grid=(nl, n6 // tn),
        in_specs=[pl.BlockSpec((8, d), lambda l, j: (0, 0)),
                  pl.BlockSpec((1, d, tn), lambda l, j: (l, 0, j)),
                  pl.BlockSpec((1, 1, tn), lambda l, j: (l, 0, j))],
        out_specs=pl.BlockSpec((1, 8, tn), lambda l, j: (l, 0, j)),
        out_shape=jax.ShapeDtypeStruct((nl, 8, n6), F32),
        compiler_params=_cp(("parallel", "parallel")),
        name="mod_vectors",
    )(cs, w_mod, b_mod.reshape(nl, 1, n6))


def _bmap(bs):
    return (lambda b, *_: (b, 0, 0)) if bs > 1 else (lambda b, *_: (0, 0, 0))


def _lnmod_mm_kernel(x_ref, sh_ref, sc_ref, w_ref, o_ref, h_scr):
    @pl.when(pl.program_id(2) == 0)
    def _():
        h = _ln(x_ref[0]) * (1.0 + sc_ref[0]) + sh_ref[0]
        h_scr[...] = h.astype(BF16)

    o_ref[0] = _dot(h_scr[...], w_ref[...]).astype(o_ref.dtype)


def _lnmod_mm(x, shift, scale, w, tn):
    b, l, d = x.shape
    n = w.shape[1]
    tm = min(2048, l)
    return pl.pallas_call(
        _lnmod_mm_kernel,
        grid=(b, l // tm, n // tn),
        in_specs=[pl.BlockSpec((1, tm, d), lambda b_, i, j: (b_, i, 0)),
                  pl.BlockSpec((1, 1, d), _bmap(shift.shape[0])),
                  pl.BlockSpec((1, 1, d), _bmap(scale.shape[0])),
                  pl.BlockSpec((d, tn), lambda b_, i, j: (0, j))],
        out_specs=pl.BlockSpec((1, tm, tn), lambda b_, i, j: (b_, i, j)),
        out_shape=jax.ShapeDtypeStruct((b, l, n), PROJ_DT),
        scratch_shapes=[pltpu.VMEM((tm, d), BF16)],
        compiler_params=_cp(("parallel", "parallel", "arbitrary")),
        name="in_proj",
    )(x, shift, scale, w)


def _small_kernel(x_ref, sh_ref, sc_ref, w_ref, wt_ref, o_ref, ot_ref):
    h = (_ln(x_ref[0]) * (1.0 + sc_ref[0]) + sh_ref[0]).astype(BF16)
    o_ref[0] = _dot(h, w_ref[...])
    ot_ref[0] = _dot_nt(wt_ref[...], h)


def _small_proj(x, shift, scale, w_sm, w_smt):
    b, l, d = x.shape
    tm = min(256, l)
    return pl.pallas_call(
        _small_kernel,
        grid=(b, l // tm),
        in_specs=[pl.BlockSpec((1, tm, d), lambda b_, i: (b_, i, 0)),
                  pl.BlockSpec((1, 1, d), _bmap(shift.shape[0])),
                  pl.BlockSpec((1, 1, d), _bmap(scale.shape[0])),
                  pl.BlockSpec((d, 128), lambda b_, i: (0, 0)),
                  pl.BlockSpec((32, d), lambda b_, i: (0, 0))],
        out_specs=[pl.BlockSpec((1, tm, 128), lambda b_, i: (b_, i, 0)),
                   pl.BlockSpec((1, 32, tm), lambda b_, i: (b_, 0, i))],
        out_shape=[jax.ShapeDtypeStruct((b, l, 128), F32), jax.ShapeDtypeStruct((b, 32, l), F32)],
        compiler_params=_cp(("parallel", "parallel")),
        name="small_proj",
    )(x, shift, scale, w_sm, w_smt)


def _conv_kernel(x_ref, w_ref, b_ref, o_ref, pad_scr, *, l, rc):
    tc = x_ref.shape[2]
    zeros8 = jnp.zeros((8, tc), F32)
    pad_scr[0:8, :] = zeros8
    pad_scr[l + 8:l + 16, :] = zeros8
    pad_scr[8:l + 8, :] = x_ref[0].astype(F32)
    half = CONV_K // 2
    for r0 in range(0, l, rc):
        acc = b_ref[...] + w_ref[0:1, :] * pad_scr[r0 + 8 - half:r0 + 8 - half + rc, :]
        for k in range(1, CONV_K):
            acc = acc + w_ref[k:k + 1, :] * pad_scr[r0 + 8 - half + k:r0 + 8 - half + k + rc, :]
        o_ref[0, r0:r0 + rc, :] = _silu(acc)


def _conv_silu(proj, col_off, width, w, bias):
    b, l, _ = proj.shape
    tc = 256
    cb0 = col_off // tc
    rc = min(512, l)
    return pl.pallas_call(
        functools.partial(_conv_kernel, l=l, rc=rc),
        grid=(b, width // tc),
        in_specs=[pl.BlockSpec((1, l, tc), lambda b_, j: (b_, 0, cb0 + j)),
                  pl.BlockSpec((8, tc), lambda b_, j: (0, j)),
                  pl.BlockSpec((1, tc), lambda b_, j: (0, j))],
        out_specs=pl.BlockSpec((1, l, tc), lambda b_, j: (b_, 0, j)),
        out_shape=jax.ShapeDtypeStruct((b, l, width), F32),
        scratch_shapes=[pltpu.VMEM((l + 16, tc), F32)],
        compiler_params=_cp(("parallel", "parallel")),
        name="conv_silu",
    )(proj, w, bias)


def _tri(n, lower):
    ii = lax.broadcasted_iota(jnp.int32, (n, n), 0)
    jj = lax.broadcasted_iota(jnp.int32, (n, n), 1)
    return (jj <= ii) if lower else (jj >= ii)


def _ssd_kernel(xf_ref, xb_ref, smf_ref, smb_ref, smtf_ref, smtb_ref, par_ref, part_ref, dx_ref, e_ref, s0_ref,
                yf_ref, yb_ref, sfin_ref, s_scr, *, nc):
    c = pl.program_id(1)

    @pl.when(c == 0)
    def _():
        s_scr[...] = s0_ref[0]

    q = SSD_CHUNK
    low = _tri(q, True)
    upp = _tri(q, False)
    lowf = low.astype(F32)
    uppf = upp.astype(F32)
    lane = lax.broadcasted_iota(jnp.int32, (1, 256), 1) // SSD_HEADDIM
    dirs = ((xf_ref, smf_ref, smtf_ref, yf_ref), (xb_ref, smb_ref, smtb_ref, yb_ref))
    for d, (x_ref, sm_ref, smt_ref, y_ref) in enumerate(dirs):
        mask = low if d == 0 else upp
        xbc = x_ref[0]
        xs = xbc[:, 0:512]
        dt = _softplus(sm_ref[0] + par_ref[0:1, :])
        la = dt * (-jnp.exp(par_ref[1:2, :]))
        cum = _dot(lowf if d == 0 else uppf, la, precision=HIGHEST)
        dtt = _softplus(smt_ref[0, 8 * d:8 * d + 8, :] + part_ref[8 * d:8 * d + 8, 0:1])
        lat = dtt * (-jnp.exp(part_ref[8 * d:8 * d + 8, 1:2]))
        cumt = _dot(lat, uppf if d == 0 else lowf, precision=HIGHEST)
        expand = e_ref[d]
        dtx = _dot_exact_rhs(dt, expand)
        cumx = _dot_exact_rhs(cum, expand)
        tot = cumx[q - 1:q, :] if d == 0 else cumx[0:1, :]
        xdt = xs * dtx
        ecum = jnp.exp(cumx)
        xw = (xdt * jnp.exp(tot - cumx)).astype(BF16)
        etot = jnp.exp(tot)
        ys = []
        for g in range(2):
            bg = xbc[:, 512 + 128 * g:640 + 128 * g]
            cg = xbc[:, 768 + 128 * g:896 + 128 * g].astype(BF16)
            sl = slice(256 * g, 256 * g + 256)
            s_old = s_scr[d, :, sl]
            scores = _dot_nt(cg, bg.astype(BF16))
            y = _dot(cg, s_old.astype(BF16)) * ecum[:, sl]
            xg = xdt[:, sl]
            for hh in range(4):
                h = 4 * g + hh
                col = cum[:, 8 * d + h:8 * d + h + 1]
                row = cumt[h:h + 1, :]
                dec = jnp.exp(jnp.where(mask, col - row, NEG_INF))
                xm = jnp.where(lane == hh, xg, 0.0).astype(BF16)
                y = y + _dot((scores * dec).astype(BF16), xm)
            ys.append(y)
            s_scr[d, :, sl] = s_old * etot[:, sl] + _dot(bg.T.astype(BF16), xw[:, sl])
        y = jnp.concatenate(ys, axis=1)
        if d == 0:
            y = y + dx_ref[...] * xs
        y_ref[0] = y

    @pl.when(c == nc - 1)
    def _():
        sfin_ref[0] = s_scr[...]


def _ssd(conv_x, small, small_t, par, par_t, dx, expand, s0):
    b, l, _ = conv_x.shape
    q = SSD_CHUNK
    nc = l // q
    fw = lambda b_, c: (b_, c, 0)
    bw = lambda b_, c: (b_, nc - 1 - c, 0)
    fwt = lambda b_, c: (b_, 0, c)
    bwt = lambda b_, c: (b_, 0, nc - 1 - c)
    full2 = lambda b_, c: (0, 0)
    return pl.pallas_call(
        functools.partial(_ssd_kernel, nc=nc),
        grid=(b, nc),
        in_specs=[pl.BlockSpec((1, q, 1024), fw), pl.BlockSpec((1, q, 1024), bw),
                  pl.BlockSpec((1, q, 128), fw), pl.BlockSpec((1, q, 128), bw),
                  pl.BlockSpec((1, 32, q), fwt), pl.BlockSpec((1, 32, q), bwt),
                  pl.BlockSpec((8, 128), full2), pl.BlockSpec((32, 128), full2),
                  pl.BlockSpec((1, 512), full2),
                  pl.BlockSpec((2, 128, 512), lambda b_, c: (0, 0, 0)),
                  pl.BlockSpec((1, 2, 128, 512), lambda b_, c: (b_, 0, 0, 0))],
        out_specs=[pl.BlockSpec((1, q, 512), fw), pl.BlockSpec((1, q, 512), bw),
                   pl.BlockSpec((1, 2, 128, 512), lambda b_, c: (b_, 0, 0, 0))],
        out_shape=[jax.ShapeDtypeStruct((b, l, 512), F32), jax.ShapeDtypeStruct((b, l, 512), F32),
                   jax.ShapeDtypeStruct((b, 2, 128, 512), F32)],
        scratch_shapes=[pltpu.VMEM((2, 128, 512), F32)],
        compiler_params=_cp(("parallel", "arbitrary")),
        name="ssd_scan",
    )(conv_x, conv_x, small, small, small_t, small_t, par, par_t, dx, expand, s0)


def _gdn_kernel(qf_ref, kf_ref, vf_ref, qb_ref, kb_ref, vb_ref, smf_ref, smb_ref, smtf_ref, smtb_ref,
                par_ref, part_ref, s0_ref, of_ref, ob_ref, sfin_ref, s_scr, *, nc):
    c = pl.program_id(1)

    @pl.when(c == 0)
    def _():
        s_scr[...] = s0_ref[0]

    cl = GDN_CHUNK
    nh = GDN_HEADS
    pk = nh * cl
    rows = 2 * cl
    ii = lax.broadcasted_iota(jnp.int32, (pk, pk), 0)
    jj = lax.broadcasted_iota(jnp.int32, (pk, pk), 1)
    same = (ii // cl) == (jj // cl)
    eye = (ii == jj).astype(F32)
    ri = lax.broadcasted_iota(jnp.int32, (rows, rows), 0)
    rj = lax.broadcasted_iota(jnp.int32, (rows, rows), 1)
    rsame = (ri // cl) == (rj // cl)
    zpad = jnp.zeros((cl, GDN_DK), F32)
    dirs = ((qf_ref, kf_ref, vf_ref, smf_ref, smtf_ref, of_ref), (qb_ref, kb_ref, vb_ref, smb_ref, smtb_ref, ob_ref))
    heads = range(nh)
    probs = []
    for d, (q_ref, k_ref, v_ref, sm_ref, smt_ref, o_ref) in enumerate(dirs):
        incl = same & ((jj <= ii) if d == 0 else (jj >= ii))
        strict = same & ((jj < ii) if d == 0 else (jj > ii))
        tri = (rsame & ((rj <= ri) if d == 0 else (rj >= ri))).astype(F32)
        tri_t = (rsame & ((ri <= rj) if d == 0 else (ri >= rj))).astype(F32)
        sm = sm_ref[0]
        beta_all = _sigmoid(sm)
        g_all = -jnp.exp(par_ref[2:3, :]) * _softplus(sm + par_ref[3:4, :])
        gt8 = -jnp.exp(part_ref[24:32, 2:3]) * _softplus(smt_ref[0, 24:32, :] + part_ref[24:32, 3:4])
        cum_all = _dot(tri, g_all, precision=HIGHEST)
        cumt_all = _dot(gt8, tri_t, precision=HIGHEST)
        for sc in ((0, 1) if d == 0 else (1, 0)):
            rs = slice(cl * sc, cl * sc + cl)
            stack = lambda ref: jnp.concatenate([ref[0, rs, GDN_DK * h:GDN_DK * h + GDN_DK] for h in heads], axis=0)
            qp, kp, vp = stack(q_ref), stack(k_ref), stack(v_ref)
            qp = qp * lax.rsqrt(jnp.sum(qp * qp, axis=-1, keepdims=True) + 1e-6) * (GDN_DK ** -0.5)
            kp = kp * lax.rsqrt(jnp.sum(kp * kp, axis=-1, keepdims=True) + 1e-6)
            col = jnp.concatenate([cum_all[rs, SM_A + 4 * d + h:SM_A + 4 * d + h + 1] for h in heads], axis=0)
            row = jnp.concatenate([cumt_all[4 * d + h:4 * d + h + 1, rs] for h in heads], axis=1)
            beta = jnp.concatenate([beta_all[rs, SM_BETA + 4 * d + h:SM_BETA + 4 * d + h + 1] for h in heads], axis=0)
            edge = cl - 1 if d == 0 else 0
            tot = jnp.concatenate([jnp.broadcast_to(col[cl * h + edge:cl * h + edge + 1, :], (cl, 1)) for h in heads],
                                  axis=0)
            dec = jnp.exp(jnp.where(incl, col - row, NEG_INF))
            kb = kp.astype(BF16)
            a_mat = jnp.where(strict, _dot_nt(kb, kb) * dec * beta, 0.0)
            ecol = jnp.exp(col)
            probs.append(dict(
                d=d, rs=rs, o_ref=o_ref, t=eye - a_mat, p=a_mat,
                rhs=jnp.concatenate([vp * beta, kp * (beta * ecol)], axis=1),
                attn=(_dot_nt(qp.astype(BF16), kb) * dec).astype(BF16),
                qg=qp * ecol, kdec=kp * jnp.exp(tot - col), etot=jnp.exp(tot)))
    for _ in range(5):
        for pb in probs:
            pbf = pb['p'].astype(BF16)
            pb['p'] = _dot(pbf, pbf)
        for pb in probs:
            pb['t'] = pb['t'] + _dot(pb['t'].astype(BF16), pb['p'].astype(BF16))
    for pb in probs:
        pb['uw'] = _dot3(pb['t'], pb['rhs'])
    for step in range(2):
        for pb in (probs[step], probs[2 + step]):
            d, rs, uw = pb['d'], pb['rs'], pb['uw']
            vnew, qs = [], []
            for h in heads:
                hr = slice(cl * h, cl * h + cl)
                wq = jnp.concatenate([uw[hr, GDN_DK:], pb['qg'][hr]], axis=0).astype(BF16)
                ws_qs = _dot(wq, s_scr[d, h].astype(BF16))
                vnew.append(uw[hr, :GDN_DK] - ws_qs[0:cl])
                qs.append(ws_qs[cl:])
            vn = jnp.concatenate(vnew, axis=0)
            out = jnp.concatenate(qs, axis=0) + _dot(pb['attn'], vn.astype(BF16))
            for h in heads:
                hr = slice(cl * h, cl * h + cl)
                pb['o_ref'][0, rs, GDN_DK * h:GDN_DK * h + GDN_DK] = out[hr]
                kd_t = jnp.concatenate([pb['kdec'][hr], zpad], axis=0).T.astype(BF16)
                vn_pad = jnp.concatenate([vnew[h], zpad], axis=0).astype(BF16)
                s_scr[d, h] = s_scr[d, h] * pb['etot'][cl * h:cl * h + 1, :] + _dot(kd_t, vn_pad)

    @pl.when(c == nc - 1)
    def _():
        sfin_ref[0] = s_scr[...]


def _gdn(conv_g, small, small_t, par, par_t, s0):
    b, l, _ = conv_g.shape
    rows = 2 * GDN_CHUNK
    nc = l // rows
    fwm = lambda j: (lambda b_, c: (b_, c, j))
    bwm = lambda j: (lambda b_, c: (b_, nc - 1 - c, j))
    fw, bw = fwm(0), bwm(0)
    fwt = lambda b_, c: (b_, 0, c)
    bwt = lambda b_, c: (b_, 0, nc - 1 - c)
    full2 = lambda b_, c: (0, 0)
    st = lambda b_, c: (b_, 0, 0, 0, 0)
    return pl.pallas_call(
        functools.partial(_gdn_kernel, nc=nc),
        grid=(b, nc),
        in_specs=[pl.BlockSpec((1, rows, 512), fwm(0)), pl.BlockSpec((1, rows, 512), fwm(1)),
                  pl.BlockSpec((1, rows, 512), fwm(2)),
                  pl.BlockSpec((1, rows, 512), bwm(0)), pl.BlockSpec((1, rows, 512), bwm(1)),
                  pl.BlockSpec((1, rows, 512), bwm(2)),
                  pl.BlockSpec((1, rows, 128), fw), pl.BlockSpec((1, rows, 128), bw),
                  pl.BlockSpec((1, 32, rows), fwt), pl.BlockSpec((1, 32, rows), bwt),
                  pl.BlockSpec((8, 128), full2), pl.BlockSpec((32, 128), full2),
                  pl.BlockSpec((1, 2, GDN_HEADS, 128, 128), st)],
        out_specs=[pl.BlockSpec((1, rows, 512), fw), pl.BlockSpec((1, rows, 512), bw),
                   pl.BlockSpec((1, 2, GDN_HEADS, 128, 128), st)],
        out_shape=[jax.ShapeDtypeStruct((b, l, 512), F32), jax.ShapeDtypeStruct((b, l, 512), F32),
                   jax.ShapeDtypeStruct((b, 2, GDN_HEADS, 128, 128), F32)],
        scratch_shapes=[pltpu.VMEM((2, GDN_HEADS, 128, 128), F32)],
        compiler_params=_cp(("parallel", "arbitrary")),
        name="gdn_scan",
    )(conv_g, conv_g, conv_g, conv_g, conv_g, conv_g, small, small, small_t, small_t, par, par_t, s0)


def _pool_body(x_ref, w_ref, sc_ref, o_ref, pa, pb, *, l, gw, two_d, win, rc):
    lo = win // 2
    hi = win - 1 - lo
    nrow = l // gw
    shift = int(math.log2(gw))
    pbw = 8 * gw
    z8 = jnp.zeros((8, 128), F32)
    pa[0:8, :] = z8
    pa[l + 8:l + 16, :] = z8
    pa[8:l + 8, :] = x_ref[0].astype(F32)
    if two_d:
        zb = jnp.zeros((pbw, 128), F32)
        pb[0:pbw, :] = zb
        pb[pbw + l:pbw + l + pbw, :] = zb

    def finish(r0, acc, pos):
        col = pos & (gw - 1)
        cnt = (jnp.minimum(col + hi, gw - 1) - jnp.maximum(col - lo, 0) + 1).astype(F32)
        if two_d:
            row = pos >> shift
            cnt = cnt * (jnp.minimum(row + hi, nrow - 1) - jnp.maximum(row - lo, 0) + 1).astype(F32)
        dlt = (acc / cnt - x_ref[0, r0:r0 + rc, :].astype(F32)).astype(BF16)
        o_ref[0, r0:r0 + rc, :] = _dot(dlt, w_ref[0]) * sc_ref[...]

    for r0 in range(0, l, rc):
        pos = lax.broadcasted_iota(jnp.int32, (rc, 128), 0) + r0
        col = pos & (gw - 1)
        acc = pa[8 + r0:8 + r0 + rc, :]
        for j in range(-lo, hi + 1):
            if j == 0:
                continue
            v = pa[8 + r0 + j:8 + r0 + j + rc, :]
            ok = (col + j >= 0) if j < 0 else (col + j < gw)
            acc = acc + jnp.where(ok, v, 0.0)
        if two_d:
            pb[pbw + r0:pbw + r0 + rc, :] = acc
        else:
            finish(r0, acc, pos)
    if two_d:
        for r0 in range(0, l, rc):
            pos = lax.broadcasted_iota(jnp.int32, (rc, 128), 0) + r0
            acc = pb[pbw + r0:pbw + r0 + rc, :]
            for j in range(-lo, hi + 1):
                if j != 0:
                    acc = acc + pb[pbw + r0 + gw * j:pbw + r0 + gw * j + rc, :]
            finish(r0, acc, pos)


def _pool_kernel(x_ref, w_ref, sc_ref, o_ref, pa, pb, **kw):
    g = pl.program_id(1)
    for gi, win in enumerate(POOL_WINDOWS):
        @pl.when(g == gi)
        def _(win=win):
            _pool_body(x_ref, w_ref, sc_ref, o_ref, pa, pb, win=win, **kw)


def _pool(proj, pool_w, pool_scale, two_d):
    b, l, _ = proj.shape
    gw = GRID_W if two_d else l
    assert gw & (gw - 1) == 0 and l % gw == 0
    rc = min(512, l)
    cb0 = OFF_POOL // 128
    pb_rows = l + 16 * gw if two_d else 8
    return pl.pallas_call(
        functools.partial(_pool_kernel, l=l, gw=gw, two_d=two_d, rc=rc),
        grid=(b, len(POOL_WINDOWS)),
        in_specs=[pl.BlockSpec((1, l, 128), lambda b_, g: (b_, 0, cb0 + g)),
                  pl.BlockSpec((1, 128, 128), lambda b_, g: (g, 0, 0)),
                  pl.BlockSpec((1, 128), lambda b_, g: (0, g))],
        out_specs=pl.BlockSpec((1, l, 128), lambda b_, g: (b_, 0, g)),
        out_shape=jax.ShapeDtypeStruct((b, l, BRANCH), F32),
        scratch_shapes=[pltpu.VMEM((l + 16, 128), F32), pltpu.VMEM((pb_rows, 128), F32)],
        compiler_params=_cp(("parallel", "parallel")),
        name="pool_branch",
    )(proj, pool_w, pool_scale)


def _four1_kernel(x_ref, cc_ref, ss_ref, xc_ref, xs_ref):
    x = x_ref[0].astype(F32)
    xc_ref[0] = _dot3(x, cc_ref[...]).astype(BF16)
    xs_ref[0] = _dot3(x, ss_ref[...]).astype(BF16)


def _four2_kernel(ct_ref, st_ref, xc_ref, xs_ref, o_ref):
    o_ref[0] = _dot(ct_ref[...], xc_ref[0]) + _dot(st_ref[...], xs_ref[0])


def _fourier(proj, tabs):
    cc, ss, cl_, sl_ = tabs
    b, l, _ = proj.shape
    tm = min(512, l)
    cb = OFF_FOUR // BRANCH
    xc, xs = pl.pallas_call(
        _four1_kernel,
        grid=(b, l // tm),
        in_specs=[pl.BlockSpec((1, tm, BRANCH), lambda b_, i: (b_, i, cb)),
                  pl.BlockSpec((BRANCH, BRANCH), lambda b_, i: (0, 0)),
                  pl.BlockSpec((BRANCH, BRANCH), lambda b_, i: (0, 0))],
        out_specs=[pl.BlockSpec((1, tm, BRANCH), lambda b_, i: (b_, i, 0))] * 2,
        out_shape=[jax.ShapeDtypeStruct((b, l, BRANCH), BF16)] * 2,
        compiler_params=_cp(("parallel", "parallel")),
        name="dft_channels",
    )(proj, cc, ss)
    tr = min(256, l)
    return pl.pallas_call(
        _four2_kernel,
        grid=(b, l // tr),
        in_specs=[pl.BlockSpec((tr, l), lambda b_, i: (i, 0)),
                  pl.BlockSpec((tr, l), lambda b_, i: (i, 0)),
                  pl.BlockSpec((1, l, BRANCH), lambda b_, i: (b_, 0, 0)),
                  pl.BlockSpec((1, l, BRANCH), lambda b_, i: (b_, 0, 0))],
        out_specs=pl.BlockSpec((1, tr, BRANCH), lambda b_, i: (b_, i, 0)),
        out_shape=jax.ShapeDtypeStruct((b, l, BRANCH), F32),
        compiler_params=_cp(("parallel", "parallel")),
        name="dft_positions",
    )(cl_, sl_, xc, xs)


def _dft_tables(l):
    c = BRANCH // 4
    k = jnp.arange(c, dtype=jnp.int32)
    ang = (2.0 * math.pi / c) * ((k[:, None] * k[None, :]) % c).astype(F32)
    scale = 1.0 / math.sqrt(l * c)
    eye4 = jnp.eye(4, dtype=F32)
    cc = jnp.kron(eye4, jnp.cos(ang) * scale)
    ss = jnp.kron(eye4, jnp.sin(ang) * scale)
    t = jnp.arange(l, dtype=jnp.int32)
    angl = (2.0 * math.pi / l) * ((t[:, None] * t[None, :]) % l).astype(F32)
    return cc, ss, jnp.cos(angl).astype(BF16), (-jnp.sin(angl)).astype(BF16)


def _merge_kernel(x_ref, g1_ref, lng_ref, lnb_ref, gates_ref, pool_ref, four_ref, yf_ref, yb_ref, sz_ref,
                  of_ref, ob_ref, gz_ref, snw_ref, gnw_ref, wbr_ref, wout_ref, o_ref, *, alpha):
    d = x_ref.shape[2]
    ssd_g = (yf_ref[0] + yb_ref[0]) * _silu(sz_ref[0].astype(F32))
    parts = []
    for g in range(2):
        blk = ssd_g[:, 256 * g:256 * g + 256]
        parts.append(blk * lax.rsqrt(jnp.mean(blk * blk, axis=-1, keepdims=True) + LN_EPS))
    ssd_out = jnp.concatenate(parts, axis=1) * snw_ref[...]
    o = of_ref[0] + ob_ref[0]
    parts = []
    for h in range(GDN_HEADS):
        blk = o[:, 128 * h:128 * h + 128]
        parts.append(blk * lax.rsqrt(jnp.mean(blk * blk, axis=-1, keepdims=True) + LN_EPS))
    gdn_out = jnp.concatenate(parts, axis=1) * gnw_ref[...] * _silu(gz_ref[0].astype(F32))
    merged = None
    for i, br in enumerate((pool_ref[0], four_ref[0], ssd_out, gdn_out)):
        term = _sigmoid(gates_ref[0, :, d * i:d * i + d].astype(F32)) * _dot(br.astype(BF16), wbr_ref[i])
        merged = term if merged is None else merged + term
    mix = _dot(merged.astype(BF16), wout_ref[...])
    y = alpha * x_ref[0] + g1_ref[0] * mix
    o_ref[0] = _ln(y) * lng_ref[...] + lnb_ref[...]


def _merge(x, gate1, ln_g, ln_b, proj, pool_o, four_o, yf, yb, of, ob, snw, gnw, wbr, wout, alpha):
    b, l, d = x.shape
    tm = min(256, l)
    row = lambda b_, i: (b_, i, 0)
    colb = lambda j: (lambda b_, i: (b_, i, j))
    vec = lambda b_, i: (0, 0)
    br = pl.BlockSpec((1, tm, BRANCH), row)
    return pl.pallas_call(
        functools.partial(_merge_kernel, alpha=alpha),
        grid=(b, l // tm),
        in_specs=[pl.BlockSpec((1, tm, d), row),
                  pl.BlockSpec((1, 1, d), _bmap(gate1.shape[0])),
                  pl.BlockSpec((1, d), vec), pl.BlockSpec((1, d), vec),
                  pl.BlockSpec((1, tm, 4 * d), colb(OFF_GATES // (4 * d))),
                  br, br, br, br,
                  pl.BlockSpec((1, tm, BRANCH), colb(OFF_SZ // BRANCH)),
                  br, br,
                  pl.BlockSpec((1, tm, BRANCH), colb(OFF_GZ // BRANCH)),
                  pl.BlockSpec((1, BRANCH), vec), pl.BlockSpec((1, BRANCH), vec),
                  pl.BlockSpec((4, BRANCH, d), lambda b_, i: (0, 0, 0)),
                  pl.BlockSpec((d, d), vec)],
        out_specs=pl.BlockSpec((1, tm, d), row),
        out_shape=jax.ShapeDtypeStruct((b, l, d), F32),
        compiler_params=_cp(("parallel", "parallel")),
        name="branch_merge",
    )(x, gate1, ln_g, ln_b, proj, pool_o, four_o, yf, yb, proj, of, ob, proj, snw, gnw, wbr, wout)


def _sort_network(n):
    pairs = []

    def merge(lo, hi, r):
        step = r * 2
        if step < hi - lo:
            merge(lo, hi, step)
            merge(lo + r, hi, step)
            pairs.extend((i, i + r) for i in range(lo + r, hi - r, step))
        else:
            pairs.append((lo, lo + r))

    def sort(lo, hi):
        if hi - lo >= 1:
            mid = lo + (hi - lo) // 2
            sort(lo, mid)
            sort(mid + 1, hi)
            merge(lo, hi, 1)

    sort(0, n - 1)
    return pairs


def _top_desc(x, n):
    nrow = x.shape[0] // 8
    lst = [x[8 * r:8 * r + 8, :] for r in range(nrow)]
    npad = 1 << (nrow - 1).bit_length()
    lst += [jnp.full(lst[0].shape, NEG_INF, F32)] * (npad - nrow)
    for i, j in _sort_network(npad):
        lst[i], lst[j] = jnp.maximum(lst[i], lst[j]), jnp.minimum(lst[i], lst[j])
    lst = lst[:nrow]
    sub = lax.broadcasted_iota(jnp.int32, lst[0].shape, 0).astype(F32)
    out = []
    for r in range(n):
        m = jnp.max(lst[0], axis=0, keepdims=True)
        out.append(m)
        left = n - r - 1
        if left:
            win = jnp.min(jnp.where(lst[0] == m, sub, 8.0), axis=0, keepdims=True)
            winner = sub == win
            for j in range(min(nrow, left)):
                nxt = lst[j + 1] if j + 1 < nrow else NEG_INF
                lst[j] = jnp.where(winner, nxt, lst[j])
    return out


def _peer_kernel(x_ref, sh_ref, sc_ref, g2_ref, lng_ref, lnb_ref, wqt_ref, keys_ref, u_ref, vt_ref, o_ref,
                 h_scr, qt_scr, th_scr, a1_scr, s2_scr, cand_scr, act_scr, p_scr, acc_scr,
                 *, n_i1, n_eb, alpha):
    eb = pl.program_id(2)
    t = x_ref.shape[1]
    ntc = t // 128
    k1 = PEER_TOPK + 1
    pairs = [(a, b) for a in range(k1) for b in range(k1) if (a + 1) * (b + 1) <= k1]

    @pl.when(eb == 0)
    def _():
        h = (_ln(x_ref[0]) * (1.0 + sc_ref[0]) + sh_ref[0]).T.astype(BF16)
        h_scr[...] = h
        qt_scr[...] = _dot(wqt_ref[...], h).astype(BF16)
        acc_scr[...] = jnp.zeros(acc_scr.shape, F32)
        p_scr[1] = jnp.zeros(p_scr.shape[1:], BF16)
        for r in range(len(pairs), cand_scr.shape[0]):
            cand_scr[r:r + 1, :] = jnp.full((1, 128), NEG_INF, F32)

        def head_body(hd, carry):
            for tc in range(ntc):
                ts = slice(128 * tc, 128 * tc + 128)
                sv, tops = [], []
                for s in range(2):
                    hs = 2 * hd + s
                    q_rows = qt_scr[pl.ds(pl.multiple_of(hs * 128, 128), 128), ts]
                    st = _dot(keys_ref[hs], q_rows)
                    sv.append(st)
                    tops.append(_top_desc(st, k1))
                v1, v2 = tops
                for r, (a, b) in enumerate(pairs):
                    cand_scr[r:r + 1, :] = v1[a] + v2[b]
                best = _top_desc(cand_scr[...], k1)
                z = None
                for kk in range(PEER_TOPK):
                    term = jnp.exp(best[kk] - best[0])
                    z = term if z is None else z + term
                thr = 0.5 * (best[PEER_TOPK - 1] + best[PEER_TOPK])
                th_scr[hd, tc] = (thr - sv[0]) * LOG2E
                a1_scr[hd, tc] = (sv[0] - best[0]) * LOG2E - jnp.log(z) * LOG2E
                s2_scr[hd, tc] = sv[1] * LOG2E
            return carry

        lax.fori_loop(0, PEER_HEADS, head_body, 0)

    def activations(slot, part, parts):
        n = u_ref.shape[0] // parts
        rs = slice(n * part, n * part + n)
        act = _gelu(_dot(u_ref[rs, :], h_scr[...]))
        for tc in range(ntc):
            act_scr[slot, tc, rs, :] = act[:, 128 * tc:128 * tc + 128]

    def gated(slot, r):
        i1 = (eb - 1) * n_i1 + r
        rows = slice(PEER_NKEYS * r, PEER_NKEYS * r + PEER_NKEYS)
        for tc in range(ntc):
            w = None
            for hd in range(PEER_HEADS):
                th = th_scr[hd, tc, pl.ds(i1, 1), :]
                a1 = a1_scr[hd, tc, pl.ds(i1, 1), :]
                s2 = s2_scr[hd, tc]
                term = jnp.where(s2 > th, jnp.exp2(s2 + a1), 0.0)
                w = term if w is None else w + term
            p_scr[slot, rows, 128 * tc:128 * tc + 128] = (w * act_scr[slot, tc, rows, :]).astype(BF16)

    def accumulate(slot, part, parts):
        n = vt_ref.shape[0] // parts
        rs = slice(n * part, n * part + n)
        acc_scr[rs, :] += _dot(vt_ref[rs, :], p_scr[slot])

    def interleaved(act_slot, acc_slot, gate_slot):
        parts = n_i1 // 2
        for q in range(parts):
            if act_slot is not None:
                activations(act_slot, q, parts)
            gated(gate_slot, 2 * q)
            accumulate(acc_slot, q, parts)
            gated(gate_slot, 2 * q + 1)

    @pl.when(eb == 0)
    def _():
        activations(0, 0, 1)

    for parity in range(2):
        @pl.when((eb >= 1) & (eb < n_eb) & (eb % 2 == parity))
        def _(parity=parity):
            interleaved(parity, parity, 1 - parity)

    @pl.when(eb == n_eb)
    def _():
        interleaved(None, n_eb % 2, (n_eb - 1) % 2)

    @pl.when(eb == n_eb + 1)
    def _():
        ffn = acc_scr[...] + _dot(vt_ref[...], p_scr[(n_eb - 1) % 2])
        y = alpha * x_ref[0] + g2_ref[0] * ffn.T
        o_ref[0] = _ln(y) * lng_ref[...] + lnb_ref[...]


def _peer_ffn(x, shift, scale, gate2, ln_g, ln_b, wqt, keys, u_tab, v_tab_t, alpha):
    b, l, d = x.shape
    t = min(512, l)
    n_exp = u_tab.shape[0]
    n_i1 = 8
    be = n_i1 * PEER_NKEYS
    n_eb = n_exp // be
    nh2 = 2 * PEER_HEADS
    k1 = PEER_TOPK + 1
    npairs = sum(1 for a in range(k1) for b_ in range(k1) if (a + 1) * (b_ + 1) <= k1)
    ncand = -(-npairs // 8) * 8
    vec = lambda b_, i, e: (0, 0)
    per_head = pltpu.VMEM((PEER_HEADS, t // 128, PEER_NKEYS, 128), F32)
    return pl.pallas_call(
        functools.partial(_peer_kernel, n_i1=n_i1, n_eb=n_eb, alpha=alpha),
        grid=(b, l // t, n_eb + 2),
        in_specs=[pl.BlockSpec((1, t, d), lambda b_, i, e: (b_, i, 0)),
                  pl.BlockSpec((1, 1, d), _bmap(shift.shape[0])),
                  pl.BlockSpec((1, 1, d), _bmap(scale.shape[0])),
                  pl.BlockSpec((1, 1, d), _bmap(gate2.shape[0])),
                  pl.BlockSpec((1, d), vec), pl.BlockSpec((1, d), vec),
                  pl.BlockSpec((nh2 * 128, d), vec),
                  pl.BlockSpec((nh2, PEER_NKEYS, 128), lambda b_, i, e: (0, 0, 0)),
                  pl.BlockSpec((be, d), lambda b_, i, e: (jnp.minimum(e, n_eb - 1), 0)),
                  pl.BlockSpec((d, be), lambda b_, i, e: (0, jnp.clip(e - 2, 0, n_eb - 1)))],
        out_specs=pl.BlockSpec((1, t, d), lambda b_, i, e: (b_, i, 0)),
        out_shape=jax.ShapeDtypeStruct((b, l, d), F32),
        scratch_shapes=[pltpu.VMEM((d, t), BF16), pltpu.VMEM((nh2 * 128, t), BF16),
                        per_head, per_head, per_head,
                        pltpu.VMEM((ncand, 128), F32), pltpu.VMEM((2, t // 128, be, 128), F32),
                        pltpu.VMEM((2, be, t), BF16),
                        pltpu.VMEM((d, t), F32)],
        compiler_params=_cp(("parallel", "parallel", "arbitrary")),
        name="peer_ffn",
    )(x, shift, scale, gate2, ln_g, ln_b, wqt, keys, u_tab, v_tab_t)


def _prep_layer_weights(w_in, pool_scale, ssd_conv_w, ssd_conv_b, ssd_dt_bias, ssd_a_log, ssd_d, ssd_norm_w,
                        gdn_conv_w, gdn_dt_bias, gdn_a_log, gdn_norm_w, peer_wq, peer_keys, peer_u, peer_v):
    nl, d, _ = w_in.shape
    seg = lambda a, b_: w_in[:, :, a:b_]
    main = jnp.concatenate([seg(4640, 8736), seg(0, 512), seg(512, 1024), seg(1024, 2048), seg(2576, 4112),
                            seg(2048, 2560), seg(4112, 4624)], axis=-1).astype(BF16)
    small = jnp.concatenate([seg(2560, 2576), seg(4624, 4632), seg(4632, 4640)], axis=-1)
    w_sm = jnp.pad(small, ((0, 0), (0, 0), (0, 96))).astype(BF16)
    w_smt = jnp.swapaxes(small, 1, 2).astype(BF16)
    conv_s = jnp.pad(ssd_conv_w, ((0, 0), (0, 8 - CONV_K), (0, 0)))
    conv_g = jnp.pad(gdn_conv_w, ((0, 0), (0, 8 - CONV_K), (0, 0)))
    zeros = lambda n: jnp.zeros((nl, n), F32)
    bias_row = jnp.concatenate([ssd_dt_bias.reshape(nl, 16), zeros(112)], axis=1)
    alog_row = jnp.concatenate([ssd_a_log.reshape(nl, 16), zeros(112)], axis=1)
    galog_row = jnp.concatenate([zeros(24), gdn_a_log.reshape(nl, 8), zeros(96)], axis=1)
    gbias_row = jnp.concatenate([zeros(24), gdn_dt_bias.reshape(nl, 8), zeros(96)], axis=1)
    par = jnp.stack([bias_row, alog_row, galog_row, gbias_row] + [zeros(128)] * 4, axis=1)
    par_t = jnp.pad(jnp.swapaxes(par[:, 0:4, 0:32], 1, 2), ((0, 0), (0, 0), (0, 124)))
    dx = jnp.repeat(ssd_d, SSD_HEADDIM, axis=1).reshape(nl, 1, BRANCH)
    return dict(
        main=main, w_sm=w_sm, w_smt=w_smt, conv_s=conv_s, conv_sb=ssd_conv_b.reshape(nl, 1, -1),
        conv_g=conv_g, conv_gb=jnp.zeros((nl, 1, 3 * BRANCH), F32), par=par, par_t=par_t, dx=dx,
        pool_scale=pool_scale.reshape(nl, 1, BRANCH), snw=ssd_norm_w.reshape(nl, 1, BRANCH),
        gnw=jnp.tile(gdn_norm_w, (1, GDN_HEADS)).reshape(nl, 1, BRANCH),
        wqt=jnp.swapaxes(peer_wq, 1, 2).astype(BF16),
        keys=peer_keys.reshape(nl, 2 * PEER_HEADS, PEER_NKEYS, -1).astype(BF16),
        u=peer_u.astype(BF16),
        vt=jnp.swapaxes(peer_v, 1, 2).astype(BF16))


def _expand_table():
    lane = jnp.arange(128)[:, None]
    col = jnp.arange(BRANCH)[None, :] // SSD_HEADDIM
    return jnp.stack([(lane == col), (lane == col + SSD_HEADS)]).astype(F32)


def _scans(x, shift, scale, lw, l_idx, expand, states):
    proj = _lnmod_mm(_fold_rows(x, shift.shape[0] == 1, 1024), shift, scale, lw['main'][l_idx], 512)
    proj = proj.reshape(x.shape[0], x.shape[1], -1)
    small, small_t = _small_proj(x, shift, scale, lw['w_sm'][l_idx], lw['w_smt'][l_idx])
    conv_s = _conv_silu(proj, OFF_XBC, 1024, lw['conv_s'][l_idx], lw['conv_sb'][l_idx])
    conv_g = _conv_silu(proj, OFF_QKV, 1536, lw['conv_g'][l_idx], lw['conv_gb'][l_idx])
    yf, yb, s_ssd = _ssd(conv_s, small, small_t, lw['par'][l_idx], lw['par_t'][l_idx], lw['dx'][l_idx], expand,
                         states[0])
    of, ob, s_gdn = _gdn(conv_g, small, small_t, lw['par'][l_idx], lw['par_t'][l_idx], states[1])
    return proj, (yf, yb, of, ob), (s_ssd, s_gdn)


def _mixer(x, mods, lw, l_idx, expand, tabs, states, pool_w, w_branch, w_out, ln_g, ln_b, two_d, alpha):
    proj, (yf, yb, of, ob), new_states = _scans(x, mods[0], mods[1], lw, l_idx, expand, states)
    pool_o = _pool(proj, pool_w, lw['pool_scale'][l_idx], two_d)
    four_o = _fourier(proj, tabs)
    x = _merge(x, mods[2], ln_g, ln_b, proj, pool_o, four_o, yf, yb, of, ob, lw['snw'][l_idx], lw['gnw'][l_idx],
               w_branch, w_out, alpha)
    return x, new_states


def _fold_rows(x, shared_mod, rows):
    b, l, d = x.shape
    if shared_mod and l < rows and rows % l == 0 and b % (rows // l) == 0:
        return x.reshape(b * l // rows, rows, d)
    return x


def _peer(x, mods, lw, l_idx, ln_g, ln_b, alpha):
    xr = _fold_rows(x, mods[3].shape[0] == 1, 512)
    y = _peer_ffn(xr, mods[3], mods[4], mods[5], ln_g, ln_b, lw['wqt'][l_idx], lw['keys'][l_idx], lw['u'][l_idx],
                  lw['vt'][l_idx], alpha)
    return y.reshape(x.shape)


def kernel(x, c, ctx, c_ctx, w_mod, b_mod, w_in, pool_w, pool_scale, ssd_conv_w, ssd_conv_b, ssd_dt_bias, ssd_a_log, ssd_d, ssd_norm_w, gdn_conv_w, gdn_dt_bias, gdn_a_log, gdn_norm_w, w_branch, w_out, ln1_g, ln1_b, peer_wq, peer_keys, peer_u, peer_v, ln2_g, ln2_b):
    bsz, seq, d = x.shape
    nl = w_in.shape[0]
    alpha = (2 * nl) ** 0.25
    lw = _prep_layer_weights(w_in, pool_scale, ssd_conv_w, ssd_conv_b, ssd_dt_bias, ssd_a_log, ssd_d, ssd_norm_w,
                             gdn_conv_w, gdn_dt_bias, gdn_a_log, gdn_norm_w, peer_wq, peer_keys, peer_u, peer_v)
    pool_wb = pool_w.astype(BF16)
    w_branch_b = w_branch.astype(BF16)
    w_out_b = w_out.astype(BF16)
    expand = _expand_table()
    tabs_x = _dft_tables(seq)
    tabs_c = _dft_tables(ctx.shape[1])
    cs = jnp.concatenate([c, c_ctx[None], jnp.zeros((8 - bsz - 1, d), F32)], axis=0)
    mod_all = _mod_all(cs, w_mod.astype(BF16), b_mod).reshape(nl, 8, 6, 1, d)
    zero_states = (jnp.zeros((bsz, 2, SSD_STATE, BRANCH), F32), jnp.zeros((bsz, 2, GDN_HEADS, GDN_DK, GDN_DK), F32))
    vec = lambda a, l_idx: a[l_idx].reshape(1, d)
    for l_idx in range(nl):
        mods_x = [mod_all[l_idx, :bsz, i] for i in range(6)]
        mods_c = [mod_all[l_idx, bsz:bsz + 1, i] for i in range(6)]
        g1, b1, g2, b2 = vec(ln1_g, l_idx), vec(ln1_b, l_idx), vec(ln2_g, l_idx), vec(ln2_b, l_idx)
        if l_idx == nl - 1:
            _, _, states = _scans(ctx, mods_c[0], mods_c[1], lw, l_idx, expand, zero_states)
        else:
            ctx, states = _mixer(ctx, mods_c, lw, l_idx, expand, tabs_c, zero_states, pool_wb[l_idx],
                                 w_branch_b[l_idx], w_out_b[l_idx], g1, b1, False, alpha)
            ctx = _peer(ctx, mods_c, lw, l_idx, g2, b2, alpha)
        x, _ = _mixer(x, mods_x, lw, l_idx, expand, tabs_x, states, pool_wb[l_idx], w_branch_b[l_idx],
                      w_out_b[l_idx], g1, b1, True, alpha)
        x = _peer(x, mods_x, lw, l_idx, g2, b2, alpha)
    return x
```

```python
import functools
import math

import jax
import jax.numpy as jnp
from jax import lax
from jax.experimental import pallas as pl
from jax.experimental.pallas import tpu as pltpu

F32 = jnp.float32
BF16 = jnp.bfloat16
HIGHEST = lax.Precision.HIGHEST
PROJ_DT = jnp.bfloat16

LN_EPS = 1e-6
GRID_W = 64
BRANCH = 512
POOL_WINDOWS = (2, 4, 8, 16)
SSD_HEADS = 8
SSD_HEADDIM = 64
SSD_STATE = 128
SSD_CHUNK = 128
GDN_HEADS = 4
GDN_DK = 128
GDN_CHUNK = 64
CONV_K = 5
PEER_HEADS = 8
PEER_NKEYS = 128
PEER_TOPK = 16
N_MAIN = 8704
OFF_GATES, OFF_POOL, OFF_FOUR, OFF_XBC, OFF_QKV, OFF_SZ, OFF_GZ = 0, 4096, 4608, 5120, 6144, 7680, 8192
SM_DT, SM_BETA, SM_A = 0, 16, 24
VMEM_LIMIT = 52 * 1024 * 1024
NEG_INF = float("-inf")
LOG2E = 1.4426950408889634


def _cp(sem):
    return pltpu.CompilerParams(dimension_semantics=sem, vmem_limit_bytes=VMEM_LIMIT)


def _sigmoid(x):
    return 1.0 / (1.0 + jnp.exp(-x))


def _silu(x):
    return x * _sigmoid(x)


def _softplus(x):
    return jnp.maximum(x, 0.0) + jnp.log(1.0 + jnp.exp(-jnp.abs(x)))


def _ln(x):
    mu = jnp.mean(x, axis=-1, keepdims=True)
    xc = x - mu
    var = jnp.mean(xc * xc, axis=-1, keepdims=True)
    return xc * lax.rsqrt(var + LN_EPS)


def _dot(a, b, **kw):
    return jnp.dot(a, b, preferred_element_type=F32, **kw)


def _dot_nt(a, b, **kw):
    return lax.dot_general(a, b, (((1,), (1,)), ((), ())), preferred_element_type=F32, **kw)


def _dot3(a, b):
    a_hi = a.astype(BF16)
    b_hi = b.astype(BF16)
    a_lo = (a - a_hi.astype(F32)).astype(BF16)
    b_lo = (b - b_hi.astype(F32)).astype(BF16)
    return _dot(a_hi, b_hi) + (_dot(a_hi, b_lo) + _dot(a_lo, b_hi))


def _dot_exact_rhs(a, b):
    bb = b.astype(BF16)
    a1 = a.astype(BF16)
    r1 = a - a1.astype(F32)
    a2 = r1.astype(BF16)
    a3 = (r1 - a2.astype(F32)).astype(BF16)
    return _dot(a1, bb) + (_dot(a2, bb) + _dot(a3, bb))


def _gelu(x):
    return 0.5 * x * (1.0 + lax.erf(x * (2.0 ** -0.5)))


def _mod_kernel(c_ref, w_ref, b_ref, o_ref):
    s = _silu(c_ref[...])
    o_ref[0] = _dot(s.astype(BF16), w_ref[0]) + b_ref[0]


def _mod_all(cs, w_mod, b_mod):
    nl, d, n6 = w_mod.shape
    tn = 1536
    return pl.pallas_call(
        _mod_kernel,
        grid=(nl, n6 // tn),
        in_specs=[pl.BlockSpec((8, d), lambda l, j: (0, 0)),
                  pl.BlockSpec((1, d, tn), lambda l, j: (l, 0, j)),
                  pl.BlockSpec((1, 1, tn), lambda l, j: (l, 0, j))],
        out_specs=pl.BlockSpec((1, 8, tn), lambda l, j: (l, 0, j)),
        out_shape=jax.ShapeDtypeStruct((nl, 8, n6), F32),
        compiler_params=_cp(("parallel", "parallel")),
        name="mod_vectors",
    )(cs, w_mod, b_mod.reshape(nl, 1, n6))


def _bmap(bs):
    return (lambda b, *_: (b, 0, 0)) if bs > 1 else (lambda b, *_: (0, 0, 0))


def _lnmod_mm_kernel(x_ref, sh_ref, sc_ref, w_ref, o_ref, h_scr):
    @pl.when(pl.program_id(2) == 0)
    def _():
        h = _ln(x_ref[0]) * (1.0 + sc_ref[0]) + sh_ref[0]
        h_scr[...] = h.astype(BF16)

    o_ref[0] = _dot(h_scr[...], w_ref[...]).astype(o_ref.dtype)


def _lnmod_mm(x, shift, scale, w, tn):
    b, l, d = x.shape
    n = w.shape[1]
    tm = min(2048, l)
    return pl.pallas_call(
        _lnmod_mm_kernel,
        grid=(b, l // tm, n // tn),
        in_specs=[pl.BlockSpec((1, tm, d), lambda b_, i, j: (b_, i, 0)),
                  pl.BlockSpec((1, 1, d), _bmap(shift.shape[0])),
                  pl.BlockSpec((1, 1, d), _bmap(scale.shape[0])),
                  pl.BlockSpec((d, tn), lambda b_, i, j: (0, j))],
        out_specs=pl.BlockSpec((1, tm, tn), lambda b_, i, j: (b_, i, j)),
        out_shape=jax.ShapeDtypeStruct((b, l, n), PROJ_DT),
        scratch_shapes=[pltpu.VMEM((tm, d), BF16)],
        compiler_params=_cp(("parallel", "parallel", "arbitrary")),
        name="in_proj",
    )(x, shift, scale, w)


def _small_kernel(x_ref, sh_ref, sc_ref, w_ref, wt_ref, o_ref, ot_ref):
    h = (_ln(x_ref[0]) * (1.0 + sc_ref[0]) + sh_ref[0]).astype(BF16)
    o_ref[0] = _dot(h, w_ref[...])
    ot_ref[0] = _dot_nt(wt_ref[...], h)


def _small_proj(x, shift, scale, w_sm, w_smt):
    b, l, d = x.shape
    tm = min(256, l)
    return pl.pallas_call(
        _small_kernel,
        grid=(b, l // tm),
        in_specs=[pl.BlockSpec((1, tm, d), lambda b_, i: (b_, i, 0)),
                  pl.BlockSpec((1, 1, d), _bmap(shift.shape[0])),
                  pl.BlockSpec((1, 1, d), _bmap(scale.shape[0])),
                  pl.BlockSpec((d, 128), lambda b_, i: (0, 0)),
                  pl.BlockSpec((32, d), lambda b_, i: (0, 0))],
        out_specs=[pl.BlockSpec((1, tm, 128), lambda b_, i: (b_, i, 0)),
                   pl.BlockSpec((1, 32, tm), lambda b_, i: (b_, 0, i))],
        out_shape=[jax.ShapeDtypeStruct((b, l, 128), F32), jax.ShapeDtypeStruct((b, 32, l), F32)],
        compiler_params=_cp(("parallel", "parallel")),
        name="small_proj",
    )(x, shift, scale, w_sm, w_smt)


def _conv_kernel(x_ref, w_ref, b_ref, o_ref, pad_scr, *, l, rc):
    tc = x_ref.shape[2]
    zeros8 = jnp.zeros((8, tc), F32)
    pad_scr[0:8, :] = zeros8
    pad_scr[l + 8:l + 16, :] = zeros8
    pad_scr[8:l + 8, :] = x_ref[0].astype(F32)
    half = CONV_K // 2
    for r0 in range(0, l, rc):
        acc = b_ref[...] + w_ref[0:1, :] * pad_scr[r0 + 8 - half:r0 + 8 - half + rc, :]
        for k in range(1, CONV_K):
            acc = acc + w_ref[k:k + 1, :] * pad_scr[r0 + 8 - half + k:r0 + 8 - half + k + rc, :]
        o_ref[0, r0:r0 + rc, :] = _silu(acc)


def _conv_silu(proj, col_off, width, w, bias):
    b, l, _ = proj.shape
    tc = 256
    cb0 = col_off // tc
    rc = min(512, l)
    return pl.pallas_call(
        functools.partial(_conv_kernel, l=l, rc=rc),
        grid=(b, width // tc),
        in_specs=[pl.BlockSpec((1, l, tc), lambda b_, j: (b_, 0, cb0 + j)),
                  pl.BlockSpec((8, tc), lambda b_, j: (0, j)),
                  pl.BlockSpec((1, tc), lambda b_, j: (0, j))],
        out_specs=pl.BlockSpec((1, l, tc), lambda b_, j: (b_, 0, j)),
        out_shape=jax.ShapeDtypeStruct((b, l, width), F32),
        scratch_shapes=[pltpu.VMEM((l + 16, tc), F32)],
        compiler_params=_cp(("parallel", "parallel")),
        name="conv_silu",
    )(proj, w, bias)


def _tri(n, lower):
    ii = lax.broadcasted_iota(jnp.int32, (n, n), 0)
    jj = lax.broadcasted_iota(jnp.int32, (n, n), 1)
    return (jj <= ii) if lower else (jj >= ii)


def _ssd_kernel(xf_ref, xb_ref, smf_ref, smb_ref, smtf_ref, smtb_ref, par_ref, part_ref, dx_ref, e_ref, s0_ref,
                yf_ref, yb_ref, sfin_ref, s_scr, *, nc):
    c = pl.program_id(1)

    @pl.when(c == 0)
    def _():
        s_scr[...] = s0_ref[0]

    q = SSD_CHUNK
    low = _tri(q, True)
    upp = _tri(q, False)
    lowf = low.astype(F32)
    uppf = upp.astype(F32)
    lane = lax.broadcasted_iota(jnp.int32, (1, 256), 1) // SSD_HEADDIM
    dirs = ((xf_ref, smf_ref, smtf_ref, yf_ref), (xb_ref, smb_ref, smtb_ref, yb_ref))
    for d, (x_ref, sm_ref, smt_ref, y_ref) in enumerate(dirs):
        mask = low if d == 0 else upp
        xbc = x_ref[0]
        xs = xbc[:, 0:512]
        dt = _softplus(sm_ref[0] + par_ref[0:1, :])
        la = dt * (-jnp.exp(par_ref[1:2, :]))
        cum = _dot(lowf if d == 0 else uppf, la, precision=HIGHEST)
        dtt = _softplus(smt_ref[0, 8 * d:8 * d + 8, :] + part_ref[8 * d:8 * d + 8, 0:1])
        lat = dtt * (-jnp.exp(part_ref[8 * d:8 * d + 8, 1:2]))
        cumt = _dot(lat, uppf if d == 0 else lowf, precision=HIGHEST)
        expand = e_ref[d]
        dtx = _dot_exact_rhs(dt, expand)
        cumx = _dot_exact_rhs(cum, expand)
        tot = cumx[q - 1:q, :] if d == 0 else cumx[0:1, :]
        xdt = xs * dtx
        ecum = jnp.exp(cumx)
        xw = (xdt * jnp.exp(tot - cumx)).astype(BF16)
        etot = jnp.exp(tot)
        ys = []
        for g in range(2):
            bg = xbc[:, 512 + 128 * g:640 + 128 * g]
            cg = xbc[:, 768 + 128 * g:896 + 128 * g].astype(BF16)
            sl = slice(256 * g, 256 * g + 256)
            s_old = s_scr[d, :, sl]
            scores = _dot_nt(cg, bg.astype(BF16))
            y = _dot(cg, s_old.astype(BF16)) * ecum[:, sl]
            xg = xdt[:, sl]
            for hh in range(4):
                h = 4 * g + hh
                col = cum[:, 8 * d + h:8 * d + h + 1]
                row = cumt[h:h + 1, :]
                dec = jnp.exp(jnp.where(mask, col - row, NEG_INF))
                xm = jnp.where(lane == hh, xg, 0.0).astype(BF16)
                y = y + _dot((scores * dec).astype(BF16), xm)
            ys.append(y)
            s_scr[d, :, sl] = s_old * etot[:, sl] + _dot(bg.T.astype(BF16), xw[:, sl])
        y = jnp.concatenate(ys, axis=1)
        if d == 0:
            y = y + dx_ref[...] * xs
        y_ref[0] = y

    @pl.when(c == nc - 1)
    def _():
        sfin_ref[0] = s_scr[...]


def _ssd(conv_x, small, small_t, par, par_t, dx, expand, s0):
    b, l, _ = conv_x.shape
    q = SSD_CHUNK
    nc = l // q
    fw = lambda b_, c: (b_, c, 0)
    bw = lambda b_, c: (b_, nc - 1 - c, 0)
    fwt = lambda b_, c: (b_, 0, c)
    bwt = lambda b_, c: (b_, 0, nc - 1 - c)
    full2 = lambda b_, c: (0, 0)
    return pl.pallas_call(
        functools.partial(_ssd_kernel, nc=nc),
        grid=(b, nc),
        in_specs=[pl.BlockSpec((1, q, 1024), fw), pl.BlockSpec((1, q, 1024), bw),
                  pl.BlockSpec((1, q, 128), fw), pl.BlockSpec((1, q, 128), bw),
                  pl.BlockSpec((1, 32, q), fwt), pl.BlockSpec((1, 32, q), bwt),
                  pl.BlockSpec((8, 128), full2), pl.BlockSpec((32, 128), full2),
                  pl.BlockSpec((1, 512), full2),
                  pl.BlockSpec((2, 128, 512), lambda b_, c: (0, 0, 0)),
                  pl.BlockSpec((1, 2, 128, 512), lambda b_, c: (b_, 0, 0, 0))],
        out_specs=[pl.BlockSpec((1, q, 512), fw), pl.BlockSpec((1, q, 512), bw),
                   pl.BlockSpec((1, 2, 128, 512), lambda b_, c: (b_, 0, 0, 0))],
        out_shape=[jax.ShapeDtypeStruct((b, l, 512), F32), jax.ShapeDtypeStruct((b, l, 512), F32),
                   jax.ShapeDtypeStruct((b, 2, 128, 512), F32)],
        scratch_shapes=[pltpu.VMEM((2, 128, 512), F32)],
        compiler_params=_cp(("parallel", "arbitrary")),
        name="ssd_scan",
    )(conv_x, conv_x, small, small, small_t, small_t, par, par_t, dx, expand, s0)


def _gdn_kernel(qf_ref, kf_ref, vf_ref, qb_ref, kb_ref, vb_ref, smf_ref, smb_ref, smtf_ref, smtb_ref,
                par_ref, part_ref, s0_ref, of_ref, ob_ref, sfin_ref, s_scr, *, nc):
    c = pl.program_id(1)

    @pl.when(c == 0)
    def _():
        s_scr[...] = s0_ref[0]

    cl = GDN_CHUNK
    nh = GDN_HEADS
    pk = nh * cl
    rows = 2 * cl
    ii = lax.broadcasted_iota(jnp.int32, (pk, pk), 0)
    jj = lax.broadcasted_iota(jnp.int32, (pk, pk), 1)
    same = (ii // cl) == (jj // cl)
    eye = (ii == jj).astype(F32)
    ri = lax.broadcasted_iota(jnp.int32, (rows, rows), 0)
    rj = lax.broadcasted_iota(jnp.int32, (rows, rows), 1)
    rsame = (ri // cl) == (rj // cl)
    zpad = jnp.zeros((cl, GDN_DK), F32)
    dirs = ((qf_ref, kf_ref, vf_ref, smf_ref, smtf_ref, of_ref), (qb_ref, kb_ref, vb_ref, smb_ref, smtb_ref, ob_ref))
    heads = range(nh)
    probs = []
    for d, (q_ref, k_ref, v_ref, sm_ref, smt_ref, o_ref) in enumerate(dirs):
        incl = same & ((jj <= ii) if d == 0 else (jj >= ii))
        strict = same & ((jj < ii) if d == 0 else (jj > ii))
        tri = (rsame & ((rj <= ri) if d == 0 else (rj >= ri))).astype(F32)
        tri_t = (rsame & ((ri <= rj) if d == 0 else (ri >= rj))).astype(F32)
        sm = sm_ref[0]
        beta_all = _sigmoid(sm)
        g_all = -jnp.exp(par_ref[2:3, :]) * _softplus(sm + par_ref[3:4, :])
        gt8 = -jnp.exp(part_ref[24:32, 2:3]) * _softplus(smt_ref[0, 24:32, :] + part_ref[24:32, 3:4])
        cum_all = _dot(tri, g_all, precision=HIGHEST)
        cumt_all = _dot(gt8, tri_t, precision=HIGHEST)
        for sc in ((0, 1) if d == 0 else (1, 0)):
            rs = slice(cl * sc, cl * sc + cl)
            stack = lambda ref: jnp.concatenate([ref[0, rs, GDN_DK * h:GDN_DK * h + GDN_DK] for h in heads], axis=0)
            qp, kp, vp = stack(q_ref), stack(k_ref), stack(v_ref)
            qp = qp * lax.rsqrt(jnp.sum(qp * qp, axis=-1, keepdims=True) + 1e-6) * (GDN_DK ** -0.5)
            kp = kp * lax.rsqrt(jnp.sum(kp * kp, axis=-1, keepdims=True) + 1e-6)
            col = jnp.concatenate([cum_all[rs, SM_A + 4 * d + h:SM_A + 4 * d + h + 1] for h in heads], axis=0)
            row = jnp.concatenate([cumt_all[4 * d + h:4 * d + h + 1, rs] for h in heads], axis=1)
            beta = jnp.concatenate([beta_all[rs, SM_BETA + 4 * d + h:SM_BETA + 4 * d + h + 1] for h in heads], axis=0)
            edge = cl - 1 if d == 0 else 0
            tot = jnp.concatenate([jnp.broadcast_to(col[cl * h + edge:cl * h + edge + 1, :], (cl, 1)) for h in heads],
                                  axis=0)
            dec = jnp.exp(jnp.where(incl, col - row, NEG_INF))
            kb = kp.astype(BF16)
            a_mat = jnp.where(strict, _dot_nt(kb, kb) * dec * beta, 0.0)
            ecol = jnp.exp(col)
            probs.append(dict(
                d=d, rs=rs, o_ref=o_ref, t=eye - a_mat, p=a_mat,
                rhs=jnp.concatenate([vp * beta, kp * (beta * ecol)], axis=1),
                attn=(_dot_nt(qp.astype(BF16), kb) * dec).astype(BF16),
                qg=qp * ecol, kdec=kp * jnp.exp(tot - col), etot=jnp.exp(tot)))
    for _ in range(5):
        for pb in probs:
            pbf = pb['p'].astype(BF16)
            pb['p'] = _dot(pbf, pbf)
        for pb in probs:
            pb['t'] = pb['t'] + _dot(pb['t'].astype(BF16), pb['p'].astype(BF16))
    for pb in probs:
        pb['uw'] = _dot3(pb['t'], pb['rhs'])
    for step in range(2):
        for pb in (probs[step], probs[2 + step]):
            d, rs, uw = pb['d'], pb['rs'], pb['uw']
            vnew, qs = [], []
            for h in heads:
                hr = slice(cl * h, cl * h + cl)
                wq = jnp.concatenate([uw[hr, GDN_DK:], pb['qg'][hr]], axis=0).astype(BF16)
                ws_qs = _dot(wq, s_scr[d, h].astype(BF16))
                vnew.append(uw[hr, :GDN_DK] - ws_qs[0:cl])
                qs.append(ws_qs[cl:])
            vn = jnp.concatenate(vnew, axis=0)
            out = jnp.concatenate(qs, axis=0) + _dot(pb['attn'], vn.astype(BF16))
            for h in heads:
                hr = slice(cl * h, cl * h + cl)
                pb['o_ref'][0, rs, GDN_DK * h:GDN_DK * h + GDN_DK] = out[hr]
                kd_t = jnp.concatenate([pb['kdec'][hr], zpad], axis=0).T.astype(BF16)
                vn_pad = jnp.concatenate([vnew[h], zpad], axis=0).astype(BF16)
                s_scr[d, h] = s_scr[d, h] * pb['etot'][cl * h:cl * h + 1, :] + _dot(kd_t, vn_pad)

    @pl.when(c == nc - 1)
    def _():
        sfin_ref[0] = s_scr[...]


def _gdn(conv_g, small, small_t, par, par_t, s0):
    b, l, _ = conv_g.shape
    rows = 2 * GDN_CHUNK
    nc = l // rows
    fwm = lambda j: (lambda b_, c: (b_, c, j))
    bwm = lambda j: (lambda b_, c: (b_, nc - 1 - c, j))
    fw, bw = fwm(0), bwm(0)
    fwt = lambda b_, c: (b_, 0, c)
    bwt = lambda b_, c: (b_, 0, nc - 1 - c)
    full2 = lambda b_, c: (0, 0)
    st = lambda b_, c: (b_, 0, 0, 0, 0)
    return pl.pallas_call(
        functools.partial(_gdn_kernel, nc=nc),
        grid=(b, nc),
        in_specs=[pl.BlockSpec((1, rows, 512), fwm(0)), pl.BlockSpec((1, rows, 512), fwm(1)),
                  pl.BlockSpec((1, rows, 512), fwm(2)),
                  pl.BlockSpec((1, rows, 512), bwm(0)), pl.BlockSpec((1, rows, 512), bwm(1)),
                  pl.BlockSpec((1, rows, 512), bwm(2)),
                  pl.BlockSpec((1, rows, 128), fw), pl.BlockSpec((1, rows, 128), bw),
                  pl.BlockSpec((1, 32, rows), fwt), pl.BlockSpec((1, 32, rows), bwt),
                  pl.BlockSpec((8, 128), full2), pl.BlockSpec((32, 128), full2),
                  pl.BlockSpec((1, 2, GDN_HEADS, 128, 128), st)],
        out_specs=[pl.BlockSpec((1, rows, 512), fw), pl.BlockSpec((1, rows, 512), bw),
                   pl.BlockSpec((1, 2, GDN_HEADS, 128, 128), st)],
        out_shape=[jax.ShapeDtypeStruct((b, l, 512), F32), jax.ShapeDtypeStruct((b, l, 512), F32),
                   jax.ShapeDtypeStruct((b, 2, GDN_HEADS, 128, 128), F32)],
        scratch_shapes=[pltpu.VMEM((2, GDN_HEADS, 128, 128), F32)],
        compiler_params=_cp(("parallel", "arbitrary")),
        name="gdn_scan",
    )(conv_g, conv_g, conv_g, conv_g, conv_g, conv_g, small, small, small_t, small_t, par, par_t, s0)


def _pool_body(x_ref, w_ref, sc_ref, o_ref, pa, pb, *, l, gw, two_d, win, rc):
    lo = win // 2
    hi = win - 1 - lo
    nrow = l // gw
    shift = int(math.log2(gw))
    pbw = 8 * gw
    z8 = jnp.zeros((8, 128), F32)
    pa[0:8, :] = z8
    pa[l + 8:l + 16, :] = z8
    pa[8:l + 8, :] = x_ref[0].astype(F32)
    if two_d:
        zb = jnp.zeros((pbw, 128), F32)
        pb[0:pbw, :] = zb
        pb[pbw + l:pbw + l + pbw, :] = zb

    def finish(r0, acc, pos):
        col = pos & (gw - 1)
        cnt = (jnp.minimum(col + hi, gw - 1) - jnp.maximum(col - lo, 0) + 1).astype(F32)
        if two_d:
            row = pos >> shift
            cnt = cnt * (jnp.minimum(row + hi, nrow - 1) - jnp.maximum(row - lo, 0) + 1).astype(F32)
        dlt = (acc / cnt - x_ref[0, r0:r0 + rc, :].astype(F32)).astype(BF16)
        o_ref[0, r0:r0 + rc, :] = _dot(dlt, w_ref[0]) * sc_ref[...]

    for r0 in range(0, l, rc):
        pos = lax.broadcasted_iota(jnp.int32, (rc, 128), 0) + r0
        col = pos & (gw - 1)
        acc = pa[8 + r0:8 + r0 + rc, :]
        for j in range(-lo, hi + 1):
            if j == 0:
                continue
            v = pa[8 + r0 + j:8 + r0 + j + rc, :]
            ok = (col + j >= 0) if j < 0 else (col + j < gw)
            acc = acc + jnp.where(ok, v, 0.0)
        if two_d:
            pb[pbw + r0:pbw + r0 + rc, :] = acc
        else:
            finish(r0, acc, pos)
    if two_d:
        for r0 in range(0, l, rc):
            pos = lax.broadcasted_iota(jnp.int32, (rc, 128), 0) + r0
            acc = pb[pbw + r0:pbw + r0 + rc, :]
            for j in range(-lo, hi + 1):
                if j != 0:
                    acc = acc + pb[pbw + r0 + gw * j:pbw + r0 + gw * j + rc, :]
            finish(r0, acc, pos)


def _pool_kernel(x_ref, w_ref, sc_ref, o_ref, pa, pb, **kw):
    g = pl.program_id(1)
    for gi, win in enumerate(POOL_WINDOWS):
        @pl.when(g == gi)
        def _(win=win):
            _pool_body(x_ref, w_ref, sc_ref, o_ref, pa, pb, win=win, **kw)


def _pool(proj, pool_w, pool_scale, two_d):
    b, l, _ = proj.shape
    gw = GRID_W if two_d else l
    assert gw & (gw - 1) == 0 and l % gw == 0
    rc = min(512, l)
    cb0 = OFF_POOL // 128
    pb_rows = l + 16 * gw if two_d else 8
    return pl.pallas_call(
        functools.partial(_pool_kernel, l=l, gw=gw, two_d=two_d, rc=rc),
        grid=(b, len(POOL_WINDOWS)),
        in_specs=[pl.BlockSpec((1, l, 128), lambda b_, g: (b_, 0, cb0 + g)),
                  pl.BlockSpec((1, 128, 128), lambda b_, g: (g, 0, 0)),
                  pl.BlockSpec((1, 128), lambda b_, g: (0, g))],
        out_specs=pl.BlockSpec((1, l, 128), lambda b_, g: (b_, 0, g)),
        out_shape=jax.ShapeDtypeStruct((b, l, BRANCH), F32),
        scratch_shapes=[pltpu.VMEM((l + 16, 128), F32), pltpu.VMEM((pb_rows, 128), F32)],
        compiler_params=_cp(("parallel", "parallel")),
        name="pool_branch",
    )(proj, pool_w, pool_scale)


def _four1_kernel(x_ref, cc_ref, ss_ref, xc_ref, xs_ref):
    x = x_ref[0].astype(F32)
    xc_ref[0] = _dot3(x, cc_ref[...]).astype(BF16)
    xs_ref[0] = _dot3(x, ss_ref[...]).astype(BF16)


def _four2_kernel(ct_ref, st_ref, xc_ref, xs_ref, o_ref):
    o_ref[0] = _dot(ct_ref[...], xc_ref[0]) + _dot(st_ref[...], xs_ref[0])


def _fourier(proj, tabs):
    cc, ss, cl_, sl_ = tabs
    b, l, _ = proj.shape
    tm = min(512, l)
    cb = OFF_FOUR // BRANCH
    xc, xs = pl.pallas_call(
        _four1_kernel,
        grid=(b, l // tm),
        in_specs=[pl.BlockSpec((1, tm, BRANCH), lambda b_, i: (b_, i, cb)),
                  pl.BlockSpec((BRANCH, BRANCH), lambda b_, i: (0, 0)),
                  pl.BlockSpec((BRANCH, BRANCH), lambda b_, i: (0, 0))],
        out_specs=[pl.BlockSpec((1, tm, BRANCH), lambda b_, i: (b_, i, 0))] * 2,
        out_shape=[jax.ShapeDtypeStruct((b, l, BRANCH), BF16)] * 2,
        compiler_params=_cp(("parallel", "parallel")),
        name="dft_channels",
    )(proj, cc, ss)
    tr = min(256, l)
    return pl.pallas_call(
        _four2_kernel,
        grid=(b, l // tr),
        in_specs=[pl.BlockSpec((tr, l), lambda b_, i: (i, 0)),
                  pl.BlockSpec((tr, l), lambda b_, i: (i, 0)),
                  pl.BlockSpec((1, l, BRANCH), lambda b_, i: (b_, 0, 0)),
                  pl.BlockSpec((1, l, BRANCH), lambda b_, i: (b_, 0, 0))],
        out_specs=pl.BlockSpec((1, tr, BRANCH), lambda b_, i: (b_, i, 0)),
        out_shape=jax.ShapeDtypeStruct((b, l, BRANCH), F32),
        compiler_params=_cp(("parallel", "parallel")),
        name="dft_positions",
    )(cl_, sl_, xc, xs)


def _dft_tables(l):
    c = BRANCH // 4
    k = jnp.arange(c, dtype=jnp.int32)
    ang = (2.0 * math.pi / c) * ((k[:, None] * k[None, :]) % c).astype(F32)
    scale = 1.0 / math.sqrt(l * c)
    eye4 = jnp.eye(4, dtype=F32)
    cc = jnp.kron(eye4, jnp.cos(ang) * scale)
    ss = jnp.kron(eye4, jnp.sin(ang) * scale)
    g = 64
    assert l % g == 0
    kk = jnp.arange(l, dtype=jnp.int32)[None, :]
    alpha = (2.0 * math.pi / l) * ((g * jnp.arange(l // g, dtype=jnp.int32)[:, None] * kk) % l).astype(F32)
    beta = (2.0 * math.pi / l) * ((jnp.arange(g, dtype=jnp.int32)[:, None] * kk) % l).astype(F32)
    ca, sa = jnp.cos(alpha)[:, None, :], jnp.sin(alpha)[:, None, :]
    cb, sb = jnp.cos(beta)[None, :, :], jnp.sin(beta)[None, :, :]
    cos_l = (ca * cb - sa * sb).reshape(l, l).astype(BF16)
    neg_sin_l = (-(sa * cb + ca * sb)).reshape(l, l).astype(BF16)
    return cc, ss, cos_l, neg_sin_l


def _merge_kernel(x_ref, g1_ref, lng_ref, lnb_ref, gates_ref, pool_ref, four_ref, yf_ref, yb_ref, sz_ref,
                  of_ref, ob_ref, gz_ref, snw_ref, gnw_ref, wbr_ref, wout_ref, o_ref, *, alpha):
    d = x_ref.shape[2]
    ssd_g = (yf_ref[0] + yb_ref[0]) * _silu(sz_ref[0].astype(F32))
    parts = []
    for g in range(2):
        blk = ssd_g[:, 256 * g:256 * g + 256]
        parts.append(blk * lax.rsqrt(jnp.mean(blk * blk, axis=-1, keepdims=True) + LN_EPS))
    ssd_out = jnp.concatenate(parts, axis=1) * snw_ref[...]
    o = of_ref[0] + ob_ref[0]
    parts = []
    for h in range(GDN_HEADS):
        blk = o[:, 128 * h:128 * h + 128]
        parts.append(blk * lax.rsqrt(jnp.mean(blk * blk, axis=-1, keepdims=True) + LN_EPS))
    gdn_out = jnp.concatenate(parts, axis=1) * gnw_ref[...] * _silu(gz_ref[0].astype(F32))
    merged = None
    for i, br in enumerate((pool_ref[0], four_ref[0], ssd_out, gdn_out)):
        term = _sigmoid(gates_ref[0, :, d * i:d * i + d].astype(F32)) * _dot(br.astype(BF16), wbr_ref[i])
        merged = term if merged is None else merged + term
    mix = _dot(merged.astype(BF16), wout_ref[...])
    y = alpha * x_ref[0] + g1_ref[0] * mix
    o_ref[0] = _ln(y) * lng_ref[...] + lnb_ref[...]


def _merge(x, gate1, ln_g, ln_b, proj, pool_o, four_o, yf, yb, of, ob, snw, gnw, wbr, wout, alpha):
    b, l, d = x.shape
    tm = min(256, l)
    row = lambda b_, i: (b_, i, 0)
    colb = lambda j: (lambda b_, i: (b_, i, j))
    vec = lambda b_, i: (0, 0)
    br = pl.BlockSpec((1, tm, BRANCH), row)
    return pl.pallas_call(
        functools.partial(_merge_kernel, alpha=alpha),
        grid=(b, l // tm),
        in_specs=[pl.BlockSpec((1, tm, d), row),
                  pl.BlockSpec((1, 1, d), _bmap(gate1.shape[0])),
                  pl.BlockSpec((1, d), vec), pl.BlockSpec((1, d), vec),
                  pl.BlockSpec((1, tm, 4 * d), colb(OFF_GATES // (4 * d))),
                  br, br, br, br,
                  pl.BlockSpec((1, tm, BRANCH), colb(OFF_SZ // BRANCH)),
                  br, br,
                  pl.BlockSpec((1, tm, BRANCH), colb(OFF_GZ // BRANCH)),
                  pl.BlockSpec((1, BRANCH), vec), pl.BlockSpec((1, BRANCH), vec),
                  pl.BlockSpec((4, BRANCH, d), lambda b_, i: (0, 0, 0)),
                  pl.BlockSpec((d, d), vec)],
        out_specs=pl.BlockSpec((1, tm, d), row),
        out_shape=jax.ShapeDtypeStruct((b, l, d), F32),
        compiler_params=_cp(("parallel", "parallel")),
        name="branch_merge",
    )(x, gate1, ln_g, ln_b, proj, pool_o, four_o, yf, yb, proj, of, ob, proj, snw, gnw, wbr, wout)


def _sort_network(n):
    pairs = []

    def merge(lo, hi, r):
        step = r * 2
        if step < hi - lo:
            merge(lo, hi, step)
            merge(lo + r, hi, step)
            pairs.extend((i, i + r) for i in range(lo + r, hi - r, step))
        else:
            pairs.append((lo, lo + r))

    def sort(lo, hi):
        if hi - lo >= 1:
            mid = lo + (hi - lo) // 2
            sort(lo, mid)
            sort(mid + 1, hi)
            merge(lo, hi, 1)

    sort(0, n - 1)
    return pairs


def _top_desc(x, n):
    nrow = x.shape[0] // 8
    lst = [x[8 * r:8 * r + 8, :] for r in range(nrow)]
    npad = 1 << (nrow - 1).bit_length()
    lst += [jnp.full(lst[0].shape, NEG_INF, F32)] * (npad - nrow)
    for i, j in _sort_network(npad):
        lst[i], lst[j] = jnp.maximum(lst[i], lst[j]), jnp.minimum(lst[i], lst[j])
    lst = lst[:nrow]
    sub = lax.broadcasted_iota(jnp.int32, lst[0].shape, 0).astype(F32)
    out = []
    for r in range(n):
        m = jnp.max(lst[0], axis=0, keepdims=True)
        out.append(m)
        left = n - r - 1
        if left:
            win = jnp.min(jnp.where(lst[0] == m, sub, 8.0), axis=0, keepdims=True)
            winner = sub == win
            for j in range(min(nrow, left)):
                nxt = lst[j + 1] if j + 1 < nrow else NEG_INF
                lst[j] = jnp.where(winner, nxt, lst[j])
    return out


def _peer_kernel(x_ref, sh_ref, sc_ref, g2_ref, lng_ref, lnb_ref, wqt_ref, keys_ref, u_ref, vt_ref, o_ref,
                 h_scr, qt_scr, th_scr, a1_scr, s2_scr, cand_scr, act_scr, p_scr, acc_scr,
                 *, n_i1, n_eb, alpha):
    eb = pl.program_id(2)
    t = x_ref.shape[1]
    ntc = t // 128
    k1 = PEER_TOPK + 1
    pairs = [(a, b) for a in range(k1) for b in range(k1) if (a + 1) * (b + 1) <= k1]

    @pl.when(eb == 0)
    def _():
        h = (_ln(x_ref[0]) * (1.0 + sc_ref[0]) + sh_ref[0]).T.astype(BF16)
        h_scr[...] = h
        qt_scr[...] = _dot(wqt_ref[...], h).astype(BF16)
        acc_scr[...] = jnp.zeros(acc_scr.shape, F32)
        p_scr[1] = jnp.zeros(p_scr.shape[1:], BF16)
        for r in range(len(pairs), cand_scr.shape[0]):
            cand_scr[r:r + 1, :] = jnp.full((1, 128), NEG_INF, F32)

        def head_body(hd, carry):
            for tc in range(ntc):
                ts = slice(128 * tc, 128 * tc + 128)
                sv, tops = [], []
                for s in range(2):
                    hs = 2 * hd + s
                    q_rows = qt_scr[pl.ds(pl.multiple_of(hs * 128, 128), 128), ts]
                    st = _dot(keys_ref[hs], q_rows)
                    sv.append(st)
                    tops.append(_top_desc(st, k1))
                v1, v2 = tops
                for r, (a, b) in enumerate(pairs):
                    cand_scr[r:r + 1, :] = v1[a] + v2[b]
                best = _top_desc(cand_scr[...], k1)
                z = None
                for kk in range(PEER_TOPK):
                    term = jnp.exp(best[kk] - best[0])
                    z = term if z is None else z + term
                thr = 0.5 * (best[PEER_TOPK - 1] + best[PEER_TOPK])
                th_scr[hd, tc] = (thr - sv[0]) * LOG2E
                a1_scr[hd, tc] = (sv[0] - best[0]) * LOG2E - jnp.log(z) * LOG2E
                s2_scr[hd, tc] = sv[1] * LOG2E
            return carry

        lax.fori_loop(0, PEER_HEADS, head_body, 0)

    def activations(slot, part, parts):
        n = u_ref.shape[0] // parts
        rs = slice(n * part, n * part + n)
        act = _gelu(_dot(u_ref[rs, :], h_scr[...]))
        for tc in range(ntc):
            act_scr[slot, tc, rs, :] = act[:, 128 * tc:128 * tc + 128]

    def gated(slot, r):
        i1 = (eb - 1) * n_i1 + r
        rows = slice(PEER_NKEYS * r, PEER_NKEYS * r + PEER_NKEYS)
        for tc in range(ntc):
            w = None
            for hd in range(PEER_HEADS):
                th = th_scr[hd, tc, pl.ds(i1, 1), :]
                a1 = a1_scr[hd, tc, pl.ds(i1, 1), :]
                s2 = s2_scr[hd, tc]
                term = jnp.where(s2 > th, jnp.exp2(s2 + a1), 0.0)
                w = term if w is None else w + term
            p_scr[slot, rows, 128 * tc:128 * tc + 128] = (w * act_scr[slot, tc, rows, :]).astype(BF16)

    def accumulate(slot, part, parts):
        n = vt_ref.shape[0] // parts
        rs = slice(n * part, n * part + n)
        acc_scr[rs, :] += _dot(vt_ref[rs, :], p_scr[slot])

    def interleaved(act_slot, acc_slot, gate_slot):
        parts = n_i1 // 2
        for q in range(parts):
            if act_slot is not None:
                activations(act_slot, q, parts)
            gated(gate_slot, 2 * q)
            accumulate(acc_slot, q, parts)
            gated(gate_slot, 2 * q + 1)

    @pl.when(eb == 0)
    def _():
        activations(0, 0, 1)

    for parity in range(2):
        @pl.when((eb >= 1) & (eb < n_eb) & (eb % 2 == parity))
        def _(parity=parity):
            interleaved(parity, parity, 1 - parity)

    @pl.when(eb == n_eb)
    def _():
        interleaved(None, n_eb % 2, (n_eb - 1) % 2)

    @pl.when(eb == n_eb + 1)
    def _():
        ffn = acc_scr[...] + _dot(vt_ref[...], p_scr[(n_eb - 1) % 2])
        y = alpha * x_ref[0] + g2_ref[0] * ffn.T
        o_ref[0] = _ln(y) * lng_ref[...] + lnb_ref[...]


def _peer_ffn(x, shift, scale, gate2, ln_g, ln_b, wqt, keys, u_tab, v_tab_t, alpha):
    b, l, d = x.shape
    t = min(512, l)
    n_exp = u_tab.shape[0]
    n_i1 = 8
    be = n_i1 * PEER_NKEYS
    n_eb = n_exp // be
    nh2 = 2 * PEER_HEADS
    k1 = PEER_TOPK + 1
    npairs = sum(1 for a in range(k1) for b_ in range(k1) if (a + 1) * (b_ + 1) <= k1)
    ncand = -(-npairs // 8) * 8
    vec = lambda b_, i, e: (0, 0)
    per_head = pltpu.VMEM((PEER_HEADS, t // 128, PEER_NKEYS, 128), F32)
    return pl.pallas_call(
        functools.partial(_peer_kernel, n_i1=n_i1, n_eb=n_eb, alpha=alpha),
        grid=(b, l // t, n_eb + 2),
        in_specs=[pl.BlockSpec((1, t, d), lambda b_, i, e: (b_, i, 0)),
                  pl.BlockSpec((1, 1, d), _bmap(shift.shape[0])),
                  pl.BlockSpec((1, 1, d), _bmap(scale.shape[0])),
                  pl.BlockSpec((1, 1, d), _bmap(gate2.shape[0])),
                  pl.BlockSpec((1, d), vec), pl.BlockSpec((1, d), vec),
                  pl.BlockSpec((nh2 * 128, d), vec),
                  pl.BlockSpec((nh2, PEER_NKEYS, 128), lambda b_, i, e: (0, 0, 0)),
                  pl.BlockSpec((be, d), lambda b_, i, e: (jnp.minimum(e, n_eb - 1), 0)),
                  pl.BlockSpec((d, be), lambda b_, i, e: (0, jnp.clip(e - 2, 0, n_eb - 1)))],
        out_specs=pl.BlockSpec((1, t, d), lambda b_, i, e: (b_, i, 0)),
        out_shape=jax.ShapeDtypeStruct((b, l, d), F32),
        scratch_shapes=[pltpu.VMEM((d, t), BF16), pltpu.VMEM((nh2 * 128, t), BF16),
                        per_head, per_head, per_head,
                        pltpu.VMEM((ncand, 128), F32), pltpu.VMEM((2, t // 128, be, 128), F32),
                        pltpu.VMEM((2, be, t), BF16),
                        pltpu.VMEM((d, t), F32)],
        compiler_params=_cp(("parallel", "parallel", "arbitrary")),
        name="peer_ffn",
    )(x, shift, scale, gate2, ln_g, ln_b, wqt, keys, u_tab, v_tab_t)


def _prep_layer_weights(w_in, pool_scale, ssd_conv_w, ssd_conv_b, ssd_dt_bias, ssd_a_log, ssd_d, ssd_norm_w,
                        gdn_conv_w, gdn_dt_bias, gdn_a_log, gdn_norm_w, peer_wq, peer_keys, peer_u, peer_v):
    nl, d, _ = w_in.shape
    seg = lambda a, b_: w_in[:, :, a:b_]
    main = jnp.concatenate([seg(4640, 8736), seg(0, 512), seg(512, 1024), seg(1024, 2048), seg(2576, 4112),
                            seg(2048, 2560), seg(4112, 4624)], axis=-1).astype(BF16)
    small = jnp.concatenate([seg(2560, 2576), seg(4624, 4632), seg(4632, 4640)], axis=-1)
    w_sm = jnp.pad(small, ((0, 0), (0, 0), (0, 96))).astype(BF16)
    w_smt = jnp.swapaxes(small, 1, 2).astype(BF16)
    conv_s = jnp.pad(ssd_conv_w, ((0, 0), (0, 8 - CONV_K), (0, 0)))
    conv_g = jnp.pad(gdn_conv_w, ((0, 0), (0, 8 - CONV_K), (0, 0)))
    zeros = lambda n: jnp.zeros((nl, n), F32)
    bias_row = jnp.concatenate([ssd_dt_bias.reshape(nl, 16), zeros(112)], axis=1)
    alog_row = jnp.concatenate([ssd_a_log.reshape(nl, 16), zeros(112)], axis=1)
    galog_row = jnp.concatenate([zeros(24), gdn_a_log.reshape(nl, 8), zeros(96)], axis=1)
    gbias_row = jnp.concatenate([zeros(24), gdn_dt_bias.reshape(nl, 8), zeros(96)], axis=1)
    par = jnp.stack([bias_row, alog_row, galog_row, gbias_row] + [zeros(128)] * 4, axis=1)
    par_t = jnp.pad(jnp.swapaxes(par[:, 0:4, 0:32], 1, 2), ((0, 0), (0, 0), (0, 124)))
    dx = jnp.repeat(ssd_d, SSD_HEADDIM, axis=1).reshape(nl, 1, BRANCH)
    return dict(
        main=main, w_sm=w_sm, w_smt=w_smt, conv_s=conv_s, conv_sb=ssd_conv_b.reshape(nl, 1, -1),
        conv_g=conv_g, conv_gb=jnp.zeros((nl, 1, 3 * BRANCH), F32), par=par, par_t=par_t, dx=dx,
        pool_scale=pool_scale.reshape(nl, 1, BRANCH), snw=ssd_norm_w.reshape(nl, 1, BRANCH),
        gnw=jnp.tile(gdn_norm_w, (1, GDN_HEADS)).reshape(nl, 1, BRANCH),
        wqt=jnp.swapaxes(peer_wq, 1, 2).astype(BF16),
        keys=peer_keys.reshape(nl, 2 * PEER_HEADS, PEER_NKEYS, -1).astype(BF16),
        u=peer_u.astype(BF16),
        vt=jnp.swapaxes(peer_v.astype(BF16), 1, 2))


def _expand_table():
    lane = jnp.arange(128)[:, None]
    col = jnp.arange(BRANCH)[None, :] // SSD_HEADDIM
    return jnp.stack([(lane == col), (lane == col + SSD_HEADS)]).astype(F32)


def _scans(x, shift, scale, lw, l_idx, expand, states):
    proj = _lnmod_mm(_fold_rows(x, shift.shape[0] == 1, 1024), shift, scale, lw['main'][l_idx], 512)
    proj = proj.reshape(x.shape[0], x.shape[1], -1)
    small, small_t = _small_proj(x, shift, scale, lw['w_sm'][l_idx], lw['w_smt'][l_idx])
    conv_s = _conv_silu(proj, OFF_XBC, 1024, lw['conv_s'][l_idx], lw['conv_sb'][l_idx])
    conv_g = _conv_silu(proj, OFF_QKV, 1536, lw['conv_g'][l_idx], lw['conv_gb'][l_idx])
    yf, yb, s_ssd = _ssd(conv_s, small, small_t, lw['par'][l_idx], lw['par_t'][l_idx], lw['dx'][l_idx], expand,
                         states[0])
    of, ob, s_gdn = _gdn(conv_g, small, small_t, lw['par'][l_idx], lw['par_t'][l_idx], states[1])
    return proj, (yf, yb, of, ob), (s_ssd, s_gdn)


def _mixer(x, mods, lw, l_idx, expand, tabs, states, pool_w, w_branch, w_out, ln_g, ln_b, two_d, alpha):
    proj, (yf, yb, of, ob), new_states = _scans(x, mods[0], mods[1], lw, l_idx, expand, states)
    pool_o = _pool(proj, pool_w, lw['pool_scale'][l_idx], two_d)
    four_o = _fourier(proj, tabs)
    x = _merge(x, mods[2], ln_g, ln_b, proj, pool_o, four_o, yf, yb, of, ob, lw['snw'][l_idx], lw['gnw'][l_idx],
               w_branch, w_out, alpha)
    return x, new_states


def _fold_rows(x, shared_mod, rows):
    b, l, d = x.shape
    if shared_mod and l < rows and rows % l == 0 and b % (rows // l) == 0:
        return x.reshape(b * l // rows, rows, d)
    return x


def _peer(x, mods, lw, l_idx, ln_g, ln_b, alpha):
    xr = _fold_rows(x, mods[3].shape[0] == 1, 512)
    y = _peer_ffn(xr, mods[3], mods[4], mods[5], ln_g, ln_b, lw['wqt'][l_idx], lw['keys'][l_idx], lw['u'][l_idx],
                  lw['vt'][l_idx], alpha)
    return y.reshape(x.shape)


def kernel(x, c, ctx, c_ctx, w_mod, b_mod, w_in, pool_w, pool_scale, ssd_conv_w, ssd_conv_b, ssd_dt_bias, ssd_a_log, ssd_d, ssd_norm_w, gdn_conv_w, gdn_dt_bias, gdn_a_log, gdn_norm_w, w_branch, w_out, ln1_g, ln1_b, peer_wq, peer_keys, peer_u, peer_v, ln2_g, ln2_b):
    bsz, seq, d = x.shape
    nl = w_in.shape[0]
    alpha = (2 * nl) ** 0.25
    lw = _prep_layer_weights(w_in, pool_scale, ssd_conv_w, ssd_conv_b, ssd_dt_bias, ssd_a_log, ssd_d, ssd_norm_w,
                             gdn_conv_w, gdn_dt_bias, gdn_a_log, gdn_norm_w, peer_wq, peer_keys, peer_u, peer_v)
    pool_wb = pool_w.astype(BF16)
    w_branch_b = w_branch.astype(BF16)
    w_out_b = w_out.astype(BF16)
    expand = _expand_table()
    tabs_x = _dft_tables(seq)
    tabs_c = _dft_tables(ctx.shape[1])
    cs = jnp.concatenate([c, c_ctx[None], jnp.zeros((8 - bsz - 1, d), F32)], axis=0)
    mod_all = _mod_all(cs, w_mod.astype(BF16), b_mod).reshape(nl, 8, 6, 1, d)
    zero_states = (jnp.zeros((bsz, 2, SSD_STATE, BRANCH), F32), jnp.zeros((bsz, 2, GDN_HEADS, GDN_DK, GDN_DK), F32))
    vec = lambda a, l_idx: a[l_idx].reshape(1, d)
    for l_idx in range(nl):
        mods_x = [mod_all[l_idx, :bsz, i] for i in range(6)]
        mods_c = [mod_all[l_idx, bsz:bsz + 1, i] for i in range(6)]
        g1, b1, g2, b2 = vec(ln1_g, l_idx), vec(ln1_b, l_idx), vec(ln2_g, l_idx), vec(ln2_b, l_idx)
        if l_idx == nl - 1:
            _, _, states = _scans(ctx, mods_c[0], mods_c[1], lw, l_idx, expand, zero_states)
        else:
            ctx, states = _mixer(ctx, mods_c, lw, l_idx, expand, tabs_c, zero_states, pool_wb[l_idx],
                                 w_branch_b[l_idx], w_out_b[l_idx], g1, b1, False, alpha)
            ctx = _peer(ctx, mods_c, lw, l_idx, g2, b2, alpha)
        x, _ = _mixer(x, mods_x, lw, l_idx, expand, tabs_x, states, pool_wb[l_idx], w_branch_b[l_idx],
                      w_out_b[l_idx], g1, b1, True, alpha)
        x = _peer(x, mods_x, lw, l_idx, g2, b2, alpha)
    return x
```

```python
import functools
import math

import jax
import jax.numpy as jnp
from jax import lax
from jax.experimental import pallas as pl
from jax.experimental.pallas import tpu as pltpu

F32 = jnp.float32
BF16 = jnp.bfloat16
HIGHEST = lax.Precision.HIGHEST
PROJ_DT = jnp.bfloat16

LN_EPS = 1e-6
GRID_W = 64
BRANCH = 512
POOL_WINDOWS = (2, 4, 8, 16)
SSD_HEADS = 8
SSD_HEADDIM = 64
SSD_STATE = 128
SSD_CHUNK = 128
GDN_HEADS = 4
GDN_DK = 128
GDN_CHUNK = 64
CONV_K = 5
PEER_HEADS = 8
PEER_NKEYS = 128
PEER_TOPK = 16
N_MAIN = 8704
OFF_GATES, OFF_POOL, OFF_FOUR, OFF_XBC, OFF_QKV, OFF_SZ, OFF_GZ = 0, 4096, 4608, 5120, 6144, 7680, 8192
SM_DT, SM_BETA, SM_A = 0, 16, 24
VMEM_LIMIT = 52 * 1024 * 1024
NEG_INF = float("-inf")
LOG2E = 1.4426950408889634


def _cp(sem):
    return pltpu.CompilerParams(dimension_semantics=sem, vmem_limit_bytes=VMEM_LIMIT)


def _sigmoid(x):
    return 1.0 / (1.0 + jnp.exp(-x))


def _silu(x):
    return x * _sigmoid(x)


def _softplus(x):
    return jnp.maximum(x, 0.0) + jnp.log(1.0 + jnp.exp(-jnp.abs(x)))


def _ln(x):
    mu = jnp.mean(x, axis=-1, keepdims=True)
    xc = x - mu
    var = jnp.mean(xc * xc, axis=-1, keepdims=True)
    return xc * lax.rsqrt(var + LN_EPS)


def _dot(a, b, **kw):
    return jnp.dot(a, b, preferred_element_type=F32, **kw)


def _dot_nt(a, b, **kw):
    return lax.dot_general(a, b, (((1,), (1,)), ((), ())), preferred_element_type=F32, **kw)


def _dot3(a, b):
    a_hi = a.astype(BF16)
    b_hi = b.astype(BF16)
    a_lo = (a - a_hi.astype(F32)).astype(BF16)
    b_lo = (b - b_hi.astype(F32)).astype(BF16)
    return _dot(a_hi, b_hi) + (_dot(a_hi, b_lo) + _dot(a_lo, b_hi))


def _dot_exact_rhs(a, b):
    bb = b.astype(BF16)
    a1 = a.astype(BF16)
    r1 = a - a1.astype(F32)
    a2 = r1.astype(BF16)
    a3 = (r1 - a2.astype(F32)).astype(BF16)
    return _dot(a1, bb) + (_dot(a2, bb) + _dot(a3, bb))


def _gelu(x):
    return 0.5 * x * (1.0 + lax.erf(x * (2.0 ** -0.5)))


def _mod_kernel(c_ref, w_ref, b_ref, o_ref):
    s = _silu(c_ref[...])
    o_ref[0] = _dot(s.astype(BF16), w_ref[0]) + b_ref[0]


def _mod_all(cs, w_mod, b_mod):
    nl, d, n6 = w_mod.shape
    tn = 1536
    return pl.pallas_call(
        _mod_kernel,
        grid=(nl, n6 // tn),
        in_specs=[pl.BlockSpec((8, d), lambda l, j: (0, 0)),
                  pl.BlockSpec((1, d, tn), lambda l, j: (l, 0, j)),
                  pl.BlockSpec((1, 1, tn), lambda l, j: (l, 0, j))],
        out_specs=pl.BlockSpec((1, 8, tn), lambda l, j: (l, 0, j)),
        out_shape=jax.ShapeDtypeStruct((nl, 8, n6), F32),
        compiler_params=_cp(("parallel", "parallel")),
        name="mod_vectors",
    )(cs, w_mod, b_mod.reshape(nl, 1, n6))


def _bmap(bs):
    return (lambda b, *_: (b, 0, 0)) if bs > 1 else (lambda b, *_: (0, 0, 0))


def _lnmod_mm_kernel(x_ref, sh_ref, sc_ref, w_ref, o_ref, h_scr):
    @pl.when(pl.program_id(2) == 0)
    def _():
        h = _ln(x_ref[0]) * (1.0 + sc_ref[0]) + sh_ref[0]
        h_scr[...] = h.astype(BF16)

    o_ref[0] = _dot(h_scr[...], w_ref[...]).astype(o_ref.dtype)


def _lnmod_mm(x, shift, scale, w, tn):
    b, l, d = x.shape
    n = w.shape[1]
    tm = min(2048, l)
    return pl.pallas_call(
        _lnmod_mm_kernel,
        grid=(b, l // tm, n // tn),
        in_specs=[pl.BlockSpec((1, tm, d), lambda b_, i, j: (b_, i, 0)),
                  pl.BlockSpec((1, 1, d), _bmap(shift.shape[0])),
                  pl.BlockSpec((1, 1, d), _bmap(scale.shape[0])),
                  pl.BlockSpec((d, tn), lambda b_, i, j: (0, j))],
        out_specs=pl.BlockSpec((1, tm, tn), lambda b_, i, j: (b_, i, j)),
        out_shape=jax.ShapeDtypeStruct((b, l, n), PROJ_DT),
        scratch_shapes=[pltpu.VMEM((tm, d), BF16)],
        compiler_params=_cp(("parallel", "parallel", "arbitrary")),
        name="in_proj",
    )(x, shift, scale, w)


def _small_kernel(x_ref, sh_ref, sc_ref, w_ref, wt_ref, o_ref, ot_ref):
    h = (_ln(x_ref[0]) * (1.0 + sc_ref[0]) + sh_ref[0]).astype(BF16)
    o_ref[0] = _dot(h, w_ref[...])
    ot_ref[0] = _dot_nt(wt_ref[...], h)


def _small_proj(x, shift, scale, w_sm, w_smt):
    b, l, d = x.shape
    tm = min(256, l)
    return pl.pallas_call(
        _small_kernel,
        grid=(b, l // tm),
        in_specs=[pl.BlockSpec((1, tm, d), lambda b_, i: (b_, i, 0)),
                  pl.BlockSpec((1, 1, d), _bmap(shift.shape[0])),
                  pl.BlockSpec((1, 1, d), _bmap(scale.shape[0])),
                  pl.BlockSpec((d, 128), lambda b_, i: (0, 0)),
                  pl.BlockSpec((32, d), lambda b_, i: (0, 0))],
        out_specs=[pl.BlockSpec((1, tm, 128), lambda b_, i: (b_, i, 0)),
                   pl.BlockSpec((1, 32, tm), lambda b_, i: (b_, 0, i))],
        out_shape=[jax.ShapeDtypeStruct((b, l, 128), F32), jax.ShapeDtypeStruct((b, 32, l), F32)],
        compiler_params=_cp(("parallel", "parallel")),
        name="small_proj",
    )(x, shift, scale, w_sm, w_smt)


def _conv_kernel(x_ref, w_ref, b_ref, o_ref, pad_scr, *, l, rc):
    tc = x_ref.shape[2]
    zeros8 = jnp.zeros((8, tc), F32)
    pad_scr[0:8, :] = zeros8
    pad_scr[l + 8:l + 16, :] = zeros8
    pad_scr[8:l + 8, :] = x_ref[0].astype(F32)
    half = CONV_K // 2
    for r0 in range(0, l, rc):
        acc = b_ref[...] + w_ref[0:1, :] * pad_scr[r0 + 8 - half:r0 + 8 - half + rc, :]
        for k in range(1, CONV_K):
            acc = acc + w_ref[k:k + 1, :] * pad_scr[r0 + 8 - half + k:r0 + 8 - half + k + rc, :]
        o_ref[0, r0:r0 + rc, :] = _silu(acc)


def _conv_silu(proj, col_off, width, w, bias):
    b, l, _ = proj.shape
    tc = 256
    cb0 = col_off // tc
    rc = min(512, l)
    return pl.pallas_call(
        functools.partial(_conv_kernel, l=l, rc=rc),
        grid=(b, width // tc),
        in_specs=[pl.BlockSpec((1, l, tc), lambda b_, j: (b_, 0, cb0 + j)),
                  pl.BlockSpec((8, tc), lambda b_, j: (0, j)),
                  pl.BlockSpec((1, tc), lambda b_, j: (0, j))],
        out_specs=pl.BlockSpec((1, l, tc), lambda b_, j: (b_, 0, j)),
        out_shape=jax.ShapeDtypeStruct((b, l, width), F32),
        scratch_shapes=[pltpu.VMEM((l + 16, tc), F32)],
        compiler_params=_cp(("parallel", "parallel")),
        name="conv_silu",
    )(proj, w, bias)


def _tri(n, lower):
    ii = lax.broadcasted_iota(jnp.int32, (n, n), 0)
    jj = lax.broadcasted_iota(jnp.int32, (n, n), 1)
    return (jj <= ii) if lower else (jj >= ii)


def _ssd_kernel(xf_ref, xb_ref, smf_ref, smb_ref, smtf_ref, smtb_ref, par_ref, part_ref, dx_ref, e_ref, s0_ref,
                yf_ref, yb_ref, sfin_ref, s_scr, *, nc):
    c = pl.program_id(1)

    @pl.when(c == 0)
    def _():
        s_scr[...] = s0_ref[0]

    q = SSD_CHUNK
    low = _tri(q, True)
    upp = _tri(q, False)
    lowf = low.astype(F32)
    uppf = upp.astype(F32)
    lane = lax.broadcasted_iota(jnp.int32, (1, 256), 1) // SSD_HEADDIM
    dirs = ((xf_ref, smf_ref, smtf_ref, yf_ref), (xb_ref, smb_ref, smtb_ref, yb_ref))
    probs = []
    for d, (x_ref, sm_ref, smt_ref, y_ref) in enumerate(dirs):
        xbc = x_ref[0]
        xs = xbc[:, 0:512]
        dt = _softplus(sm_ref[0] + par_ref[0:1, :])
        la = dt * (-jnp.exp(par_ref[1:2, :]))
        cum = _dot(lowf if d == 0 else uppf, la, precision=HIGHEST)
        dtt = _softplus(smt_ref[0, 8 * d:8 * d + 8, :] + part_ref[8 * d:8 * d + 8, 0:1])
        lat = dtt * (-jnp.exp(part_ref[8 * d:8 * d + 8, 1:2]))
        cumt = _dot(lat, uppf if d == 0 else lowf, precision=HIGHEST)
        expand = e_ref[d]
        dtx = _dot_exact_rhs(dt, expand)
        cumx = _dot_exact_rhs(cum, expand)
        tot = cumx[q - 1:q, :] if d == 0 else cumx[0:1, :]
        xdt = xs * dtx
        ecum = jnp.exp(cumx)
        xw = (xdt * jnp.exp(tot - cumx)).astype(BF16)
        etot = jnp.exp(tot)
        for g in range(2):
            bg = xbc[:, 512 + 128 * g:640 + 128 * g]
            cg = xbc[:, 768 + 128 * g:896 + 128 * g].astype(BF16)
            sl = slice(256 * g, 256 * g + 256)
            s_old = s_scr[d, :, sl]
            probs.append(dict(
                d=d, g=g, sl=sl, mask=low if d == 0 else upp, cum=cum, cumt=cumt, xg=xdt[:, sl],
                scores=_dot_nt(cg, bg.astype(BF16)), y=_dot(cg, s_old.astype(BF16)) * ecum[:, sl],
                s_new=s_old * etot[:, sl] + _dot(bg.T.astype(BF16), xw[:, sl]),
                y_ref=y_ref, skip=dx_ref[...] * xs if d == 0 else None))
    for hh in range(4):
        for pb in probs:
            d, h = pb['d'], 4 * pb['g'] + hh
            col = pb['cum'][:, 8 * d + h:8 * d + h + 1]
            row = pb['cumt'][h:h + 1, :]
            dec = jnp.exp(jnp.where(pb['mask'], col - row, NEG_INF))
            xm = jnp.where(lane == hh, pb['xg'], 0.0).astype(BF16)
            pb['y'] = pb['y'] + _dot((pb['scores'] * dec).astype(BF16), xm)
    for pb in probs:
        s_scr[pb['d'], :, pb['sl']] = pb['s_new']
    for d in range(2):
        y = jnp.concatenate([probs[2 * d]['y'], probs[2 * d + 1]['y']], axis=1)
        if probs[2 * d]['skip'] is not None:
            y = y + probs[2 * d]['skip']
        probs[2 * d]['y_ref'][0] = y

    @pl.when(c == nc - 1)
    def _():
        sfin_ref[0] = s_scr[...]


def _ssd(conv_x, small, small_t, par, par_t, dx, expand, s0):
    b, l, _ = conv_x.shape
    q = SSD_CHUNK
    nc = l // q
    fw = lambda b_, c: (b_, c, 0)
    bw = lambda b_, c: (b_, nc - 1 - c, 0)
    fwt = lambda b_, c: (b_, 0, c)
    bwt = lambda b_, c: (b_, 0, nc - 1 - c)
    full2 = lambda b_, c: (0, 0)
    return pl.pallas_call(
        functools.partial(_ssd_kernel, nc=nc),
        grid=(b, nc),
        in_specs=[pl.BlockSpec((1, q, 1024), fw), pl.BlockSpec((1, q, 1024), bw),
                  pl.BlockSpec((1, q, 128), fw), pl.BlockSpec((1, q, 128), bw),
                  pl.BlockSpec((1, 32, q), fwt), pl.BlockSpec((1, 32, q), bwt),
                  pl.BlockSpec((8, 128), full2), pl.BlockSpec((32, 128), full2),
                  pl.BlockSpec((1, 512), full2),
                  pl.BlockSpec((2, 128, 512), lambda b_, c: (0, 0, 0)),
                  pl.BlockSpec((1, 2, 128, 512), lambda b_, c: (b_, 0, 0, 0))],
        out_specs=[pl.BlockSpec((1, q, 512), fw), pl.BlockSpec((1, q, 512), bw),
                   pl.BlockSpec((1, 2, 128, 512), lambda b_, c: (b_, 0, 0, 0))],
        out_shape=[jax.ShapeDtypeStruct((b, l, 512), F32), jax.ShapeDtypeStruct((b, l, 512), F32),
                   jax.ShapeDtypeStruct((b, 2, 128, 512), F32)],
        scratch_shapes=[pltpu.VMEM((2, 128, 512), F32)],
        compiler_params=_cp(("parallel", "arbitrary")),
        name="ssd_scan",
    )(conv_x, conv_x, small, small, small_t, small_t, par, par_t, dx, expand, s0)


def _gdn_kernel(qf_ref, kf_ref, vf_ref, qb_ref, kb_ref, vb_ref, smf_ref, smb_ref, smtf_ref, smtb_ref,
                par_ref, part_ref, s0_ref, of_ref, ob_ref, sfin_ref, s_scr, *, nc):
    c = pl.program_id(1)

    @pl.when(c == 0)
    def _():
        s_scr[...] = s0_ref[0]

    cl = GDN_CHUNK
    nh = GDN_HEADS
    pk = nh * cl
    rows = 2 * cl
    ii = lax.broadcasted_iota(jnp.int32, (pk, pk), 0)
    jj = lax.broadcasted_iota(jnp.int32, (pk, pk), 1)
    same = (ii // cl) == (jj // cl)
    eye = (ii == jj).astype(F32)
    ri = lax.broadcasted_iota(jnp.int32, (rows, rows), 0)
    rj = lax.broadcasted_iota(jnp.int32, (rows, rows), 1)
    rsame = (ri // cl) == (rj // cl)
    zpad = jnp.zeros((cl, GDN_DK), F32)
    dirs = ((qf_ref, kf_ref, vf_ref, smf_ref, smtf_ref, of_ref), (qb_ref, kb_ref, vb_ref, smb_ref, smtb_ref, ob_ref))
    heads = range(nh)
    probs = []
    for d, (q_ref, k_ref, v_ref, sm_ref, smt_ref, o_ref) in enumerate(dirs):
        incl = same & ((jj <= ii) if d == 0 else (jj >= ii))
        strict = same & ((jj < ii) if d == 0 else (jj > ii))
        tri = (rsame & ((rj <= ri) if d == 0 else (rj >= ri))).astype(F32)
        tri_t = (rsame & ((ri <= rj) if d == 0 else (ri >= rj))).astype(F32)
        sm = sm_ref[0]
        beta_all = _sigmoid(sm)
        g_all = -jnp.exp(par_ref[2:3, :]) * _softplus(sm + par_ref[3:4, :])
        gt8 = -jnp.exp(part_ref[24:32, 2:3]) * _softplus(smt_ref[0, 24:32, :] + part_ref[24:32, 3:4])
        cum_all = _dot(tri, g_all, precision=HIGHEST)
        cumt_all = _dot(gt8, tri_t, precision=HIGHEST)
        for sc in ((0, 1) if d == 0 else (1, 0)):
            rs = slice(cl * sc, cl * sc + cl)
            stack = lambda ref: jnp.concatenate([ref[0, rs, GDN_DK * h:GDN_DK * h + GDN_DK] for h in heads], axis=0)
            qp, kp, vp = stack(q_ref), stack(k_ref), stack(v_ref)
            qp = qp * lax.rsqrt(jnp.sum(qp * qp, axis=-1, keepdims=True) + 1e-6) * (GDN_DK ** -0.5)
            kp = kp * lax.rsqrt(jnp.sum(kp * kp, axis=-1, keepdims=True) + 1e-6)
            col = jnp.concatenate([cum_all[rs, SM_A + 4 * d + h:SM_A + 4 * d + h + 1] for h in heads], axis=0)
            row = jnp.concatenate([cumt_all[4 * d + h:4 * d + h + 1, rs] for h in heads], axis=1)
            beta = jnp.concatenate([beta_all[rs, SM_BETA + 4 * d + h:SM_BETA + 4 * d + h + 1] for h in heads], axis=0)
            edge = cl - 1 if d == 0 else 0
            tot = jnp.concatenate([jnp.broadcast_to(col[cl * h + edge:cl * h + edge + 1, :], (cl, 1)) for h in heads],
                                  axis=0)
            dec = jnp.exp(jnp.where(incl, col - row, NEG_INF))
            kb = kp.astype(BF16)
            a_mat = jnp.where(strict, _dot_nt(kb, kb) * dec * beta, 0.0)
            ecol = jnp.exp(col)
            probs.append(dict(
                d=d, rs=rs, o_ref=o_ref, t=eye - a_mat, p=a_mat,
                rhs=jnp.concatenate([vp * beta, kp * (beta * ecol)], axis=1),
                attn=(_dot_nt(qp.astype(BF16), kb) * dec).astype(BF16),
                qg=qp * ecol, kdec=kp * jnp.exp(tot - col), etot=jnp.exp(tot)))
    for _ in range(5):
        for pb in probs:
            pbf = pb['p'].astype(BF16)
            pb['p'] = _dot(pbf, pbf)
        for pb in probs:
            pb['t'] = pb['t'] + _dot(pb['t'].astype(BF16), pb['p'].astype(BF16))
    for pb in probs:
        pb['uw'] = _dot3(pb['t'], pb['rhs'])
    for step in range(2):
        for pb in (probs[step], probs[2 + step]):
            d, rs, uw = pb['d'], pb['rs'], pb['uw']
            vnew, qs = [], []
            for h in heads:
                hr = slice(cl * h, cl * h + cl)
                wq = jnp.concatenate([uw[hr, GDN_DK:], pb['qg'][hr]], axis=0).astype(BF16)
                ws_qs = _dot(wq, s_scr[d, h].astype(BF16))
                vnew.append(uw[hr, :GDN_DK] - ws_qs[0:cl])
                qs.append(ws_qs[cl:])
            vn = jnp.concatenate(vnew, axis=0)
            out = jnp.concatenate(qs, axis=0) + _dot(pb['attn'], vn.astype(BF16))
            for h in heads:
                hr = slice(cl * h, cl * h + cl)
                pb['o_ref'][0, rs, GDN_DK * h:GDN_DK * h + GDN_DK] = out[hr]
                kd_t = jnp.concatenate([pb['kdec'][hr], zpad], axis=0).T.astype(BF16)
                vn_pad = jnp.concatenate([vnew[h], zpad], axis=0).astype(BF16)
                s_scr[d, h] = s_scr[d, h] * pb['etot'][cl * h:cl * h + 1, :] + _dot(kd_t, vn_pad)

    @pl.when(c == nc - 1)
    def _():
        sfin_ref[0] = s_scr[...]


def _gdn(conv_g, small, small_t, par, par_t, s0):
    b, l, _ = conv_g.shape
    rows = 2 * GDN_CHUNK
    nc = l // rows
    fwm = lambda j: (lambda b_, c: (b_, c, j))
    bwm = lambda j: (lambda b_, c: (b_, nc - 1 - c, j))
    fw, bw = fwm(0), bwm(0)
    fwt = lambda b_, c: (b_, 0, c)
    bwt = lambda b_, c: (b_, 0, nc - 1 - c)
    full2 = lambda b_, c: (0, 0)
    st = lambda b_, c: (b_, 0, 0, 0, 0)
    return pl.pallas_call(
        functools.partial(_gdn_kernel, nc=nc),
        grid=(b, nc),
        in_specs=[pl.BlockSpec((1, rows, 512), fwm(0)), pl.BlockSpec((1, rows, 512), fwm(1)),
                  pl.BlockSpec((1, rows, 512), fwm(2)),
                  pl.BlockSpec((1, rows, 512), bwm(0)), pl.BlockSpec((1, rows, 512), bwm(1)),
                  pl.BlockSpec((1, rows, 512), bwm(2)),
                  pl.BlockSpec((1, rows, 128), fw), pl.BlockSpec((1, rows, 128), bw),
                  pl.BlockSpec((1, 32, rows), fwt), pl.BlockSpec((1, 32, rows), bwt),
                  pl.BlockSpec((8, 128), full2), pl.BlockSpec((32, 128), full2),
                  pl.BlockSpec((1, 2, GDN_HEADS, 128, 128), st)],
        out_specs=[pl.BlockSpec((1, rows, 512), fw), pl.BlockSpec((1, rows, 512), bw),
                   pl.BlockSpec((1, 2, GDN_HEADS, 128, 128), st)],
        out_shape=[jax.ShapeDtypeStruct((b, l, 512), F32), jax.ShapeDtypeStruct((b, l, 512), F32),
                   jax.ShapeDtypeStruct((b, 2, GDN_HEADS, 128, 128), F32)],
        scratch_shapes=[pltpu.VMEM((2, GDN_HEADS, 128, 128), F32)],
        compiler_params=_cp(("parallel", "arbitrary")),
        name="gdn_scan",
    )(conv_g, conv_g, conv_g, conv_g, conv_g, conv_g, small, small, small_t, small_t, par, par_t, s0)


def _pool_body(x_ref, w_ref, sc_ref, o_ref, pa, pb, *, l, gw, two_d, win, rc):
    lo = win // 2
    hi = win - 1 - lo
    nrow = l // gw
    shift = int(math.log2(gw))
    pbw = 8 * gw
    z8 = jnp.zeros((8, 128), F32)
    pa[0:8, :] = z8
    pa[l + 8:l + 16, :] = z8
    pa[8:l + 8, :] = x_ref[0].astype(F32)
    if two_d:
        zb = jnp.zeros((pbw, 128), F32)
        pb[0:pbw, :] = zb
        pb[pbw + l:pbw + l + pbw, :] = zb

    def finish(r0, acc, pos):
        col = pos & (gw - 1)
        cnt = (jnp.minimum(col + hi, gw - 1) - jnp.maximum(col - lo, 0) + 1).astype(F32)
        if two_d:
            row = pos >> shift
            cnt = cnt * (jnp.minimum(row + hi, nrow - 1) - jnp.maximum(row - lo, 0) + 1).astype(F32)
        dlt = (acc / cnt - x_ref[0, r0:r0 + rc, :].astype(F32)).astype(BF16)
        o_ref[0, r0:r0 + rc, :] = _dot(dlt, w_ref[0]) * sc_ref[...]

    for r0 in range(0, l, rc):
        pos = lax.broadcasted_iota(jnp.int32, (rc, 128), 0) + r0
        col = pos & (gw - 1)
        acc = pa[8 + r0:8 + r0 + rc, :]
        for j in range(-lo, hi + 1):
            if j == 0:
                continue
            v = pa[8 + r0 + j:8 + r0 + j + rc, :]
            ok = (col + j >= 0) if j < 0 else (col + j < gw)
            acc = acc + jnp.where(ok, v, 0.0)
        if two_d:
            pb[pbw + r0:pbw + r0 + rc, :] = acc
        else:
            finish(r0, acc, pos)
    if two_d:
        for r0 in range(0, l, rc):
            pos = lax.broadcasted_iota(jnp.int32, (rc, 128), 0) + r0
            acc = pb[pbw + r0:pbw + r0 + rc, :]
            for j in range(-lo, hi + 1):
                if j != 0:
                    acc = acc + pb[pbw + r0 + gw * j:pbw + r0 + gw * j + rc, :]
            finish(r0, acc, pos)


def _pool_kernel(x_ref, w_ref, sc_ref, o_ref, pa, pb, **kw):
    g = pl.program_id(1)
    for gi, win in enumerate(POOL_WINDOWS):
        @pl.when(g == gi)
        def _(win=win):
            _pool_body(x_ref, w_ref, sc_ref, o_ref, pa, pb, win=win, **kw)


def _pool(proj, pool_w, pool_scale, two_d):
    b, l, _ = proj.shape
    gw = GRID_W if two_d else l
    assert gw & (gw - 1) == 0 and l % gw == 0
    rc = min(512, l)
    cb0 = OFF_POOL // 128
    pb_rows = l + 16 * gw if two_d else 8
    return pl.pallas_call(
        functools.partial(_pool_kernel, l=l, gw=gw, two_d=two_d, rc=rc),
        grid=(b, len(POOL_WINDOWS)),
        in_specs=[pl.BlockSpec((1, l, 128), lambda b_, g: (b_, 0, cb0 + g)),
                  pl.BlockSpec((1, 128, 128), lambda b_, g: (g, 0, 0)),
                  pl.BlockSpec((1, 128), lambda b_, g: (0, g))],
        out_specs=pl.BlockSpec((1, l, 128), lambda b_, g: (b_, 0, g)),
        out_shape=jax.ShapeDtypeStruct((b, l, BRANCH), F32),
        scratch_shapes=[pltpu.VMEM((l + 16, 128), F32), pltpu.VMEM((pb_rows, 128), F32)],
        compiler_params=_cp(("parallel", "parallel")),
        name="pool_branch",
    )(proj, pool_w, pool_scale)


def _four1_kernel(x_ref, cc_ref, ss_ref, xc_ref, xs_ref):
    x = x_ref[0].astype(F32)
    xc_ref[0] = _dot3(x, cc_ref[...]).astype(BF16)
    xs_ref[0] = _dot3(x, ss_ref[...]).astype(BF16)


def _four2_kernel(ct_ref, st_ref, xc_ref, xs_ref, o_ref):
    o_ref[0] = _dot(ct_ref[...], xc_ref[0]) + _dot(st_ref[...], xs_ref[0])


def _fourier(proj, tabs):
    cc, ss, cl_, sl_ = tabs
    b, l, _ = proj.shape
    tm = min(512, l)
    cb = OFF_FOUR // BRANCH
    xc, xs = pl.pallas_call(
        _four1_kernel,
        grid=(b, l // tm),
        in_specs=[pl.BlockSpec((1, tm, BRANCH), lambda b_, i: (b_, i, cb)),
                  pl.BlockSpec((BRANCH, BRANCH), lambda b_, i: (0, 0)),
                  pl.BlockSpec((BRANCH, BRANCH), lambda b_, i: (0, 0))],
        out_specs=[pl.BlockSpec((1, tm, BRANCH), lambda b_, i: (b_, i, 0))] * 2,
        out_shape=[jax.ShapeDtypeStruct((b, l, BRANCH), BF16)] * 2,
        compiler_params=_cp(("parallel", "parallel")),
        name="dft_channels",
    )(proj, cc, ss)
    tr = min(256, l)
    return pl.pallas_call(
        _four2_kernel,
        grid=(b, l // tr),
        in_specs=[pl.BlockSpec((tr, l), lambda b_, i: (i, 0)),
                  pl.BlockSpec((tr, l), lambda b_, i: (i, 0)),
                  pl.BlockSpec((1, l, BRANCH), lambda b_, i: (b_, 0, 0)),
                  pl.BlockSpec((1, l, BRANCH), lambda b_, i: (b_, 0, 0))],
        out_specs=pl.BlockSpec((1, tr, BRANCH), lambda b_, i: (b_, i, 0)),
        out_shape=jax.ShapeDtypeStruct((b, l, BRANCH), F32),
        compiler_params=_cp(("parallel", "parallel")),
        name="dft_positions",
    )(cl_, sl_, xc, xs)


def _dft_tables(l):
    c = BRANCH // 4
    k = jnp.arange(c, dtype=jnp.int32)
    ang = (2.0 * math.pi / c) * ((k[:, None] * k[None, :]) % c).astype(F32)
    scale = 1.0 / math.sqrt(l * c)
    eye4 = jnp.eye(4, dtype=F32)
    cc = jnp.kron(eye4, jnp.cos(ang) * scale)
    ss = jnp.kron(eye4, jnp.sin(ang) * scale)
    g = 64
    assert l % g == 0
    kk = jnp.arange(l, dtype=jnp.int32)[None, :]
    alpha = (2.0 * math.pi / l) * ((g * jnp.arange(l // g, dtype=jnp.int32)[:, None] * kk) % l).astype(F32)
    beta = (2.0 * math.pi / l) * ((jnp.arange(g, dtype=jnp.int32)[:, None] * kk) % l).astype(F32)
    ca, sa = jnp.cos(alpha)[:, None, :], jnp.sin(alpha)[:, None, :]
    cb, sb = jnp.cos(beta)[None, :, :], jnp.sin(beta)[None, :, :]
    cos_l = (ca * cb - sa * sb).reshape(l, l).astype(BF16)
    neg_sin_l = (-(sa * cb + ca * sb)).reshape(l, l).astype(BF16)
    return cc, ss, cos_l, neg_sin_l


def _merge_kernel(x_ref, g1_ref, lng_ref, lnb_ref, gates_ref, pool_ref, four_ref, yf_ref, yb_ref, sz_ref,
                  of_ref, ob_ref, gz_ref, snw_ref, gnw_ref, wbr_ref, wout_ref, o_ref, *, alpha):
    d = x_ref.shape[2]
    ssd_g = (yf_ref[0] + yb_ref[0]) * _silu(sz_ref[0].astype(F32))
    parts = []
    for g in range(2):
        blk = ssd_g[:, 256 * g:256 * g + 256]
        parts.append(blk * lax.rsqrt(jnp.mean(blk * blk, axis=-1, keepdims=True) + LN_EPS))
    ssd_out = jnp.concatenate(parts, axis=1) * snw_ref[...]
    o = of_ref[0] + ob_ref[0]
    parts = []
    for h in range(GDN_HEADS):
        blk = o[:, 128 * h:128 * h + 128]
        parts.append(blk * lax.rsqrt(jnp.mean(blk * blk, axis=-1, keepdims=True) + LN_EPS))
    gdn_out = jnp.concatenate(parts, axis=1) * gnw_ref[...] * _silu(gz_ref[0].astype(F32))
    merged = None
    for i, br in enumerate((pool_ref[0], four_ref[0], ssd_out, gdn_out)):
        term = _sigmoid(gates_ref[0, :, d * i:d * i + d].astype(F32)) * _dot(br.astype(BF16), wbr_ref[i])
        merged = term if merged is None else merged + term
    mix = _dot(merged.astype(BF16), wout_ref[...])
    y = alpha * x_ref[0] + g1_ref[0] * mix
    o_ref[0] = _ln(y) * lng_ref[...] + lnb_ref[...]


def _merge(x, gate1, ln_g, ln_b, proj, pool_o, four_o, yf, yb, of, ob, snw, gnw, wbr, wout, alpha):
    b, l, d = x.shape
    tm = min(256, l)
    row = lambda b_, i: (b_, i, 0)
    colb = lambda j: (lambda b_, i: (b_, i, j))
    vec = lambda b_, i: (0, 0)
    br = pl.BlockSpec((1, tm, BRANCH), row)
    return pl.pallas_call(
        functools.partial(_merge_kernel, alpha=alpha),
        grid=(b, l // tm),
        in_specs=[pl.BlockSpec((1, tm, d), row),
                  pl.BlockSpec((1, 1, d), _bmap(gate1.shape[0])),
                  pl.BlockSpec((1, d), vec), pl.BlockSpec((1, d), vec),
                  pl.BlockSpec((1, tm, 4 * d), colb(OFF_GATES // (4 * d))),
                  br, br, br, br,
                  pl.BlockSpec((1, tm, BRANCH), colb(OFF_SZ // BRANCH)),
                  br, br,
                  pl.BlockSpec((1, tm, BRANCH), colb(OFF_GZ // BRANCH)),
                  pl.BlockSpec((1, BRANCH), vec), pl.BlockSpec((1, BRANCH), vec),
                  pl.BlockSpec((4, BRANCH, d), lambda b_, i: (0, 0, 0)),
                  pl.BlockSpec((d, d), vec)],
        out_specs=pl.BlockSpec((1, tm, d), row),
        out_shape=jax.ShapeDtypeStruct((b, l, d), F32),
        compiler_params=_cp(("parallel", "parallel")),
        name="branch_merge",
    )(x, gate1, ln_g, ln_b, proj, pool_o, four_o, yf, yb, proj, of, ob, proj, snw, gnw, wbr, wout)


def _sort_network(n):
    pairs = []

    def merge(lo, hi, r):
        step = r * 2
        if step < hi - lo:
            merge(lo, hi, step)
            merge(lo + r, hi, step)
            pairs.extend((i, i + r) for i in range(lo + r, hi - r, step))
        else:
            pairs.append((lo, lo + r))

    def sort(lo, hi):
        if hi - lo >= 1:
            mid = lo + (hi - lo) // 2
            sort(lo, mid)
            sort(mid + 1, hi)
            merge(lo, hi, 1)

    sort(0, n - 1)
    return pairs


def _top_desc(x, n):
    nrow = x.shape[0] // 8
    lst = [x[8 * r:8 * r + 8, :] for r in range(nrow)]
    npad = 1 << (nrow - 1).bit_length()
    lst += [jnp.full(lst[0].shape, NEG_INF, F32)] * (npad - nrow)
    for i, j in _sort_network(npad):
        lst[i], lst[j] = jnp.maximum(lst[i], lst[j]), jnp.minimum(lst[i], lst[j])
    lst = lst[:nrow]
    sub = lax.broadcasted_iota(jnp.int32, lst[0].shape, 0).astype(F32)
    out = []
    for r in range(n):
        m = jnp.max(lst[0], axis=0, keepdims=True)
        out.append(m)
        left = n - r - 1
        if left:
            win = jnp.min(jnp.where(lst[0] == m, sub, 8.0), axis=0, keepdims=True)
            winner = sub == win
            for j in range(min(nrow, left)):
                nxt = lst[j + 1] if j + 1 < nrow else NEG_INF
                lst[j] = jnp.where(winner, nxt, lst[j])
    return out


def _peer_kernel(x_ref, sh_ref, sc_ref, g2_ref, lng_ref, lnb_ref, wqt_ref, keys_ref, u_ref, vt_ref, o_ref,
                 h_scr, qt_scr, th_scr, a1_scr, s2_scr, cand_scr, act_scr, p_scr, acc_scr,
                 *, n_i1, n_eb, alpha):
    eb = pl.program_id(2)
    t = x_ref.shape[1]
    ntc = t // 128
    k1 = PEER_TOPK + 1
    pairs = [(a, b) for a in range(k1) for b in range(k1) if (a + 1) * (b + 1) <= k1]

    @pl.when(eb == 0)
    def _():
        h = (_ln(x_ref[0]) * (1.0 + sc_ref[0]) + sh_ref[0]).T.astype(BF16)
        h_scr[...] = h
        qt_scr[...] = _dot(wqt_ref[...], h).astype(BF16)
        acc_scr[...] = jnp.zeros(acc_scr.shape, F32)
        p_scr[1] = jnp.zeros(p_scr.shape[1:], BF16)
        for r in range(len(pairs), cand_scr.shape[0]):
            cand_scr[r:r + 1, :] = jnp.full((1, 128), NEG_INF, F32)

        def head_body(hd, carry):
            for tc in range(ntc):
                ts = slice(128 * tc, 128 * tc + 128)
                sv, tops = [], []
                for s in range(2):
                    hs = 2 * hd + s
                    q_rows = qt_scr[pl.ds(pl.multiple_of(hs * 128, 128), 128), ts]
                    st = _dot(keys_ref[hs], q_rows)
                    sv.append(st)
                    tops.append(_top_desc(st, k1))
                v1, v2 = tops
                for r, (a, b) in enumerate(pairs):
                    cand_scr[r:r + 1, :] = v1[a] + v2[b]
                best = _top_desc(cand_scr[...], k1)
                z = None
                for kk in range(PEER_TOPK):
                    term = jnp.exp(best[kk] - best[0])
                    z = term if z is None else z + term
                thr = 0.5 * (best[PEER_TOPK - 1] + best[PEER_TOPK])
                th_scr[hd, tc] = (thr - sv[0]) * LOG2E
                a1_scr[hd, tc] = (sv[0] - best[0]) * LOG2E - jnp.log(z) * LOG2E
                s2_scr[hd, tc] = sv[1] * LOG2E
            return carry

        lax.fori_loop(0, PEER_HEADS, head_body, 0)

    def activations(slot, part, parts):
        n = u_ref.shape[0] // parts
        rs = slice(n * part, n * part + n)
        act = _gelu(_dot(u_ref[rs, :], h_scr[...]))
        for tc in range(ntc):
            act_scr[slot, tc, rs, :] = act[:, 128 * tc:128 * tc + 128]

    def gated(slot, r):
        i1 = (eb - 1) * n_i1 + r
        rows = slice(PEER_NKEYS * r, PEER_NKEYS * r + PEER_NKEYS)
        for tc in range(ntc):
            w = None
            for hd in range(PEER_HEADS):
                th = th_scr[hd, tc, pl.ds(i1, 1), :]
                a1 = a1_scr[hd, tc, pl.ds(i1, 1), :]
                s2 = s2_scr[hd, tc]
                term = jnp.where(s2 > th, jnp.exp2(s2 + a1), 0.0)
                w = term if w is None else w + term
            p_scr[slot, rows, 128 * tc:128 * tc + 128] = (w * act_scr[slot, tc, rows, :]).astype(BF16)

    def accumulate(slot, part, parts):
        n = vt_ref.shape[0] // parts
        rs = slice(n * part, n * part + n)
        acc_scr[rs, :] += _dot(vt_ref[rs, :], p_scr[slot])

    def interleaved(act_slot, acc_slot, gate_slot):
        parts = n_i1 // 2
        for q in range(parts):
            if act_slot is not None:
                activations(act_slot, q, parts)
            gated(gate_slot, 2 * q)
            accumulate(acc_slot, q, parts)
            gated(gate_slot, 2 * q + 1)

    @pl.when(eb == 0)
    def _():
        activations(0, 0, 1)

    for parity in range(2):
        @pl.when((eb >= 1) & (eb < n_eb) & (eb % 2 == parity))
        def _(parity=parity):
            interleaved(parity, parity, 1 - parity)

    @pl.when(eb == n_eb)
    def _():
        interleaved(None, n_eb % 2, (n_eb - 1) % 2)

    @pl.when(eb == n_eb + 1)
    def _():
        ffn = acc_scr[...] + _dot(vt_ref[...], p_scr[(n_eb - 1) % 2])
        y = alpha * x_ref[0] + g2_ref[0] * ffn.T
        o_ref[0] = _ln(y) * lng_ref[...] + lnb_ref[...]


def _peer_ffn(x, shift, scale, gate2, ln_g, ln_b, wqt, keys, u_tab, v_tab_t, alpha):
    b, l, d = x.shape
    t = min(512, l)
    n_exp = u_tab.shape[0]
    n_i1 = 8
    be = n_i1 * PEER_NKEYS
    n_eb = n_exp // be
    nh2 = 2 * PEER_HEADS
    k1 = PEER_TOPK + 1
    npairs = sum(1 for a in range(k1) for b_ in range(k1) if (a + 1) * (b_ + 1) <= k1)
    ncand = -(-npairs // 8) * 8
    vec = lambda b_, i, e: (0, 0)
    per_head = pltpu.VMEM((PEER_HEADS, t // 128, PEER_NKEYS, 128), F32)
    return pl.pallas_call(
        functools.partial(_peer_kernel, n_i1=n_i1, n_eb=n_eb, alpha=alpha),
        grid=(b, l // t, n_eb + 2),
        in_specs=[pl.BlockSpec((1, t, d), lambda b_, i, e: (b_, i, 0)),
                  pl.BlockSpec((1, 1, d), _bmap(shift.shape[0])),
                  pl.BlockSpec((1, 1, d), _bmap(scale.shape[0])),
                  pl.BlockSpec((1, 1, d), _bmap(gate2.shape[0])),
                  pl.BlockSpec((1, d), vec), pl.BlockSpec((1, d), vec),
                  pl.BlockSpec((nh2 * 128, d), vec),
                  pl.BlockSpec((nh2, PEER_NKEYS, 128), lambda b_, i, e: (0, 0, 0)),
                  pl.BlockSpec((be, d), lambda b_, i, e: (jnp.minimum(e, n_eb - 1), 0)),
                  pl.BlockSpec((d, be), lambda b_, i, e: (0, jnp.clip(e - 2, 0, n_eb - 1)))],
        out_specs=pl.BlockSpec((1, t, d), lambda b_, i, e: (b_, i, 0)),
        out_shape=jax.ShapeDtypeStruct((b, l, d), F32),
        scratch_shapes=[pltpu.VMEM((d, t), BF16), pltpu.VMEM((nh2 * 128, t), BF16),
                        per_head, per_head, per_head,
                        pltpu.VMEM((ncand, 128), F32), pltpu.VMEM((2, t // 128, be, 128), F32),
                        pltpu.VMEM((2, be, t), BF16),
                        pltpu.VMEM((d, t), F32)],
        compiler_params=_cp(("parallel", "parallel", "arbitrary")),
        name="peer_ffn",
    )(x, shift, scale, gate2, ln_g, ln_b, wqt, keys, u_tab, v_tab_t)


def _prep_layer_weights(w_in, pool_scale, ssd_conv_w, ssd_conv_b, ssd_dt_bias, ssd_a_log, ssd_d, ssd_norm_w,
                        gdn_conv_w, gdn_dt_bias, gdn_a_log, gdn_norm_w, peer_wq, peer_keys, peer_u, peer_v):
    nl, d, _ = w_in.shape
    seg = lambda a, b_: w_in[:, :, a:b_]
    main = jnp.concatenate([seg(4640, 8736), seg(0, 512), seg(512, 1024), seg(1024, 2048), seg(2576, 4112),
                            seg(2048, 2560), seg(4112, 4624)], axis=-1).astype(BF16)
    small = jnp.concatenate([seg(2560, 2576), seg(4624, 4632), seg(4632, 4640)], axis=-1)
    w_sm = jnp.pad(small, ((0, 0), (0, 0), (0, 96))).astype(BF16)
    w_smt = jnp.swapaxes(small, 1, 2).astype(BF16)
    conv_s = jnp.pad(ssd_conv_w, ((0, 0), (0, 8 - CONV_K), (0, 0)))
    conv_g = jnp.pad(gdn_conv_w, ((0, 0), (0, 8 - CONV_K), (0, 0)))
    zeros = lambda n: jnp.zeros((nl, n), F32)
    bias_row = jnp.concatenate([ssd_dt_bias.reshape(nl, 16), zeros(112)], axis=1)
    alog_row = jnp.concatenate([ssd_a_log.reshape(nl, 16), zeros(112)], axis=1)
    galog_row = jnp.concatenate([zeros(24), gdn_a_log.reshape(nl, 8), zeros(96)], axis=1)
    gbias_row = jnp.concatenate([zeros(24), gdn_dt_bias.reshape(nl, 8), zeros(96)], axis=1)
    par = jnp.stack([bias_row, alog_row, galog_row, gbias_row] + [zeros(128)] * 4, axis=1)
    par_t = jnp.pad(jnp.swapaxes(par[:, 0:4, 0:32], 1, 2), ((0, 0), (0, 0), (0, 124)))
    dx = jnp.repeat(ssd_d, SSD_HEADDIM, axis=1).reshape(nl, 1, BRANCH)
    return dict(
        main=main, w_sm=w_sm, w_smt=w_smt, conv_s=conv_s, conv_sb=ssd_conv_b.reshape(nl, 1, -1),
        conv_g=conv_g, conv_gb=jnp.zeros((nl, 1, 3 * BRANCH), F32), par=par, par_t=par_t, dx=dx,
        pool_scale=pool_scale.reshape(nl, 1, BRANCH), snw=ssd_norm_w.reshape(nl, 1, BRANCH),
        gnw=jnp.tile(gdn_norm_w, (1, GDN_HEADS)).reshape(nl, 1, BRANCH),
        wqt=jnp.swapaxes(peer_wq, 1, 2).astype(BF16),
        keys=peer_keys.reshape(nl, 2 * PEER_HEADS, PEER_NKEYS, -1).astype(BF16),
        u=peer_u.astype(BF16),
        vt=jnp.swapaxes(peer_v.astype(BF16), 1, 2))


def _expand_table():
    lane = jnp.arange(128)[:, None]
    col = jnp.arange(BRANCH)[None, :] // SSD_HEADDIM
    return jnp.stack([(lane == col), (lane == col + SSD_HEADS)]).astype(F32)


def _scans(x, shift, scale, lw, l_idx, expand, states):
    proj = _lnmod_mm(_fold_rows(x, shift.shape[0] == 1, 1024), shift, scale, lw['main'][l_idx], 512)
    proj = proj.reshape(x.shape[0], x.shape[1], -1)
    small, small_t = _small_proj(x, shift, scale, lw['w_sm'][l_idx], lw['w_smt'][l_idx])
    conv_s = _conv_silu(proj, OFF_XBC, 1024, lw['conv_s'][l_idx], lw['conv_sb'][l_idx])
    conv_g = _conv_silu(proj, OFF_QKV, 1536, lw['conv_g'][l_idx], lw['conv_gb'][l_idx])
    yf, yb, s_ssd = _ssd(conv_s, small, small_t, lw['par'][l_idx], lw['par_t'][l_idx], lw['dx'][l_idx], expand,
                         states[0])
    of, ob, s_gdn = _gdn(conv_g, small, small_t, lw['par'][l_idx], lw['par_t'][l_idx], states[1])
    return proj, (yf, yb, of, ob), (s_ssd, s_gdn)


def _mixer(x, mods, lw, l_idx, expand, tabs, states, pool_w, w_branch, w_out, ln_g, ln_b, two_d, alpha):
    proj, (yf, yb, of, ob), new_states = _scans(x, mods[0], mods[1], lw, l_idx, expand, states)
    pool_o = _pool(proj, pool_w, lw['pool_scale'][l_idx], two_d)
    four_o = _fourier(proj, tabs)
    x = _merge(x, mods[2], ln_g, ln_b, proj, pool_o, four_o, yf, yb, of, ob, lw['snw'][l_idx], lw['gnw'][l_idx],
               w_branch, w_out, alpha)
    return x, new_states


def _fold_rows(x, shared_mod, rows):
    b, l, d = x.shape
    if shared_mod and l < rows and rows % l == 0 and b % (rows // l) == 0:
        return x.reshape(b * l // rows, rows, d)
    return x


def _peer(x, mods, lw, l_idx, ln_g, ln_b, alpha):
    xr = _fold_rows(x, mods[3].shape[0] == 1, 512)
    y = _peer_ffn(xr, mods[3], mods[4], mods[5], ln_g, ln_b, lw['wqt'][l_idx], lw['keys'][l_idx], lw['u'][l_idx],
                  lw['vt'][l_idx], alpha)
    return y.reshape(x.shape)


def kernel(x, c, ctx, c_ctx, w_mod, b_mod, w_in, pool_w, pool_scale, ssd_conv_w, ssd_conv_b, ssd_dt_bias, ssd_a_log, ssd_d, ssd_norm_w, gdn_conv_w, gdn_dt_bias, gdn_a_log, gdn_norm_w, w_branch, w_out, ln1_g, ln1_b, peer_wq, peer_keys, peer_u, peer_v, ln2_g, ln2_b):
    bsz, seq, d = x.shape
    nl = w_in.shape[0]
    alpha = (2 * nl) ** 0.25
    lw = _prep_layer_weights(w_in, pool_scale, ssd_conv_w, ssd_conv_b, ssd_dt_bias, ssd_a_log, ssd_d, ssd_norm_w,
                             gdn_conv_w, gdn_dt_bias, gdn_a_log, gdn_norm_w, peer_wq, peer_keys, peer_u, peer_v)
    pool_wb = pool_w.astype(BF16)
    w_branch_b = w_branch.astype(BF16)
    w_out_b = w_out.astype(BF16)
    expand = _expand_table()
    tabs_x = _dft_tables(seq)
    tabs_c = _dft_tables(ctx.shape[1])
    cs = jnp.concatenate([c, c_ctx[None], jnp.zeros((8 - bsz - 1, d), F32)], axis=0)
    mod_all = _mod_all(cs, w_mod.astype(BF16), b_mod).reshape(nl, 8, 6, 1, d)
    zero_states = (jnp.zeros((bsz, 2, SSD_STATE, BRANCH), F32), jnp.zeros((bsz, 2, GDN_HEADS, GDN_DK, GDN_DK), F32))
    vec = lambda a, l_idx: a[l_idx].reshape(1, d)
    for l_idx in range(nl):
        mods_x = [mod_all[l_idx, :bsz, i] for i in range(6)]
        mods_c = [mod_all[l_idx, bsz:bsz + 1, i] for i in range(6)]
        g1, b1, g2, b2 = vec(ln1_g, l_idx), vec(ln1_b, l_idx), vec(ln2_g, l_idx), vec(ln2_b, l_idx)
        if l_idx == nl - 1:
            _, _, states = _scans(ctx, mods_c[0], mods_c[1], lw, l_idx, expand, zero_states)
        else:
            ctx, states = _mixer(ctx, mods_c, lw, l_idx, expand, tabs_c, zero_states, pool_wb[l_idx],
                                 w_branch_b[l_idx], w_out_b[l_idx], g1, b1, False, alpha)
            ctx = _peer(ctx, mods_c, lw, l_idx, g2, b2, alpha)
        x, _ = _mixer(x, mods_x, lw, l_idx, expand, tabs_x, states, pool_wb[l_idx], w_branch_b[l_idx],
                      w_out_b[l_idx], g1, b1, True, alpha)
        x = _peer(x, mods_x, lw, l_idx, g2, b2, alpha)
    return x
```

```python
import functools
import math

import jax
import jax.numpy as jnp
from jax import lax
from jax.experimental import pallas as pl
from jax.experimental.pallas import tpu as pltpu

F32 = jnp.float32
BF16 = jnp.bfloat16
HIGHEST = lax.Precision.HIGHEST
PROJ_DT = jnp.bfloat16

LN_EPS = 1e-6
GRID_W = 64
BRANCH = 512
POOL_WINDOWS = (2, 4, 8, 16)
SSD_HEADS = 8
SSD_HEADDIM = 64
SSD_STATE = 128
SSD_CHUNK = 128
GDN_HEADS = 4
GDN_DK = 128
GDN_CHUNK = 64
CONV_K = 5
PEER_HEADS = 8
PEER_NKEYS = 128
PEER_TOPK = 16
N_MAIN = 8704
OFF_GATES, OFF_POOL, OFF_FOUR, OFF_XBC, OFF_QKV, OFF_SZ, OFF_GZ = 0, 4096, 4608, 5120, 6144, 7680, 8192
SM_DT, SM_BETA, SM_A = 0, 16, 24
VMEM_LIMIT = 52 * 1024 * 1024
NEG_INF = float("-inf")
LOG2E = 1.4426950408889634


def _cp(sem):
    return pltpu.CompilerParams(dimension_semantics=sem, vmem_limit_bytes=VMEM_LIMIT)


def _sigmoid(x):
    return 1.0 / (1.0 + jnp.exp(-x))


def _silu(x):
    return x * _sigmoid(x)


def _softplus(x):
    return jnp.maximum(x, 0.0) + jnp.log(1.0 + jnp.exp(-jnp.abs(x)))


def _ln(x):
    mu = jnp.mean(x, axis=-1, keepdims=True)
    xc = x - mu
    var = jnp.mean(xc * xc, axis=-1, keepdims=True)
    return xc * lax.rsqrt(var + LN_EPS)


def _dot(a, b, **kw):
    return jnp.dot(a, b, preferred_element_type=F32, **kw)


def _dot_nt(a, b, **kw):
    return lax.dot_general(a, b, (((1,), (1,)), ((), ())), preferred_element_type=F32, **kw)


def _dot3(a, b):
    a_hi = a.astype(BF16)
    b_hi = b.astype(BF16)
    a_lo = (a - a_hi.astype(F32)).astype(BF16)
    b_lo = (b - b_hi.astype(F32)).astype(BF16)
    return _dot(a_hi, b_hi) + (_dot(a_hi, b_lo) + _dot(a_lo, b_hi))


def _dot_exact_rhs(a, b):
    bb = b.astype(BF16)
    a1 = a.astype(BF16)
    r1 = a - a1.astype(F32)
    a2 = r1.astype(BF16)
    a3 = (r1 - a2.astype(F32)).astype(BF16)
    return _dot(a1, bb) + (_dot(a2, bb) + _dot(a3, bb))


def _gelu(x):
    return 0.5 * x * (1.0 + lax.erf(x * (2.0 ** -0.5)))


def _mod_kernel(c_ref, w_ref, b_ref, o_ref):
    s = _silu(c_ref[...])
    o_ref[0] = _dot(s.astype(BF16), w_ref[0]) + b_ref[0]


def _mod_all(cs, w_mod, b_mod):
    nl, d, n6 = w_mod.shape
    tn = 1536
    return pl.pallas_call(
        _mod_kernel,
        grid=(nl, n6 // tn),
        in_specs=[pl.BlockSpec((8, d), lambda l, j: (0, 0)),
                  pl.BlockSpec((1, d, tn), lambda l, j: (l, 0, j)),
                  pl.BlockSpec((1, 1, tn), lambda l, j: (l, 0, j))],
        out_specs=pl.BlockSpec((1, 8, tn), lambda l, j: (l, 0, j)),
        out_shape=jax.ShapeDtypeStruct((nl, 8, n6), F32),
        compiler_params=_cp(("parallel", "parallel")),
        name="mod_vectors",
    )(cs, w_mod, b_mod.reshape(nl, 1, n6))


def _bmap(bs):
    return (lambda b, *_: (b, 0, 0)) if bs > 1 else (lambda b, *_: (0, 0, 0))


def _lnmod_mm_kernel(x_ref, sh_ref, sc_ref, w_ref, o_ref, h_scr):
    @pl.when(pl.program_id(2) == 0)
    def _():
        h = _ln(x_ref[0]) * (1.0 + sc_ref[0]) + sh_ref[0]
        h_scr[...] = h.astype(BF16)

    o_ref[0] = _dot(h_scr[...], w_ref[...]).astype(o_ref.dtype)


def _lnmod_mm(x, shift, scale, w, tn):
    b, l, d = x.shape
    n = w.shape[1]
    tm = min(2048, l)
    return pl.pallas_call(
        _lnmod_mm_kernel,
        grid=(b, l // tm, n // tn),
        in_specs=[pl.BlockSpec((1, tm, d), lambda b_, i, j: (b_, i, 0)),
                  pl.BlockSpec((1, 1, d), _bmap(shift.shape[0])),
                  pl.BlockSpec((1, 1, d), _bmap(scale.shape[0])),
                  pl.BlockSpec((d, tn), lambda b_, i, j: (0, j))],
        out_specs=pl.BlockSpec((1, tm, tn), lambda b_, i, j: (b_, i, j)),
        out_shape=jax.ShapeDtypeStruct((b, l, n), PROJ_DT),
        scratch_shapes=[pltpu.VMEM((tm, d), BF16)],
        compiler_params=_cp(("parallel", "parallel", "arbitrary")),
        name="in_proj",
    )(x, shift, scale, w)


def _small_kernel(x_ref, sh_ref, sc_ref, w_ref, wt_ref, o_ref, ot_ref):
    h = (_ln(x_ref[0]) * (1.0 + sc_ref[0]) + sh_ref[0]).astype(BF16)
    o_ref[0] = _dot(h, w_ref[...])
    ot_ref[0] = _dot_nt(wt_ref[...], h)


def _small_proj(x, shift, scale, w_sm, w_smt):
    b, l, d = x.shape
    tm = min(256, l)
    return pl.pallas_call(
        _small_kernel,
        grid=(b, l // tm),
        in_specs=[pl.BlockSpec((1, tm, d), lambda b_, i: (b_, i, 0)),
                  pl.BlockSpec((1, 1, d), _bmap(shift.shape[0])),
                  pl.BlockSpec((1, 1, d), _bmap(scale.shape[0])),
                  pl.BlockSpec((d, 128), lambda b_, i: (0, 0)),
                  pl.BlockSpec((32, d), lambda b_, i: (0, 0))],
        out_specs=[pl.BlockSpec((1, tm, 128), lambda b_, i: (b_, i, 0)),
                   pl.BlockSpec((1, 32, tm), lambda b_, i: (b_, 0, i))],
        out_shape=[jax.ShapeDtypeStruct((b, l, 128), F32), jax.ShapeDtypeStruct((b, 32, l), F32)],
        compiler_params=_cp(("parallel", "parallel")),
        name="small_proj",
    )(x, shift, scale, w_sm, w_smt)


def _conv_kernel(x_ref, w_ref, b_ref, o_ref, pad_scr, *, l, rc):
    tc = x_ref.shape[2]
    zeros8 = jnp.zeros((8, tc), F32)
    pad_scr[0:8, :] = zeros8
    pad_scr[l + 8:l + 16, :] = zeros8
    pad_scr[8:l + 8, :] = x_ref[0].astype(F32)
    half = CONV_K // 2
    for r0 in range(0, l, rc):
        acc = b_ref[...] + w_ref[0:1, :] * pad_scr[r0 + 8 - half:r0 + 8 - half + rc, :]
        for k in range(1, CONV_K):
            acc = acc + w_ref[k:k + 1, :] * pad_scr[r0 + 8 - half + k:r0 + 8 - half + k + rc, :]
        o_ref[0, r0:r0 + rc, :] = _silu(acc)


def _conv_silu(proj, col_off, width, w, bias):
    b, l, _ = proj.shape
    tc = 512
    cb0 = col_off // tc
    rc = min(512, l)
    return pl.pallas_call(
        functools.partial(_conv_kernel, l=l, rc=rc),
        grid=(b, width // tc),
        in_specs=[pl.BlockSpec((1, l, tc), lambda b_, j: (b_, 0, cb0 + j)),
                  pl.BlockSpec((8, tc), lambda b_, j: (0, j)),
                  pl.BlockSpec((1, tc), lambda b_, j: (0, j))],
        out_specs=pl.BlockSpec((1, l, tc), lambda b_, j: (b_, 0, j)),
        out_shape=jax.ShapeDtypeStruct((b, l, width), F32),
        scratch_shapes=[pltpu.VMEM((l + 16, tc), F32)],
        compiler_params=_cp(("parallel", "parallel")),
        name="conv_silu",
    )(proj, w, bias)


def _tri(n, lower):
    ii = lax.broadcasted_iota(jnp.int32, (n, n), 0)
    jj = lax.broadcasted_iota(jnp.int32, (n, n), 1)
    return (jj <= ii) if lower else (jj >= ii)


def _ssd_kernel(xf_ref, xb_ref, smf_ref, smb_ref, smtf_ref, smtb_ref, par_ref, part_ref, dx_ref, e_ref, s0_ref,
                yf_ref, yb_ref, sfin_ref, s_scr, *, nc):
    c = pl.program_id(1)

    @pl.when(c == 0)
    def _():
        s_scr[...] = s0_ref[0]

    q = SSD_CHUNK
    low = _tri(q, True)
    upp = _tri(q, False)
    lowf = low.astype(F32)
    uppf = upp.astype(F32)
    lane = lax.broadcasted_iota(jnp.int32, (1, 256), 1) // SSD_HEADDIM
    dirs = ((xf_ref, smf_ref, smtf_ref, yf_ref), (xb_ref, smb_ref, smtb_ref, yb_ref))
    probs = []
    for d, (x_ref, sm_ref, smt_ref, y_ref) in enumerate(dirs):
        xbc = x_ref[0]
        xs = xbc[:, 0:512]
        dt = _softplus(sm_ref[0] + par_ref[0:1, :])
        la = dt * (-jnp.exp(par_ref[1:2, :]))
        cum = _dot(lowf if d == 0 else uppf, la, precision=HIGHEST)
        dtt = _softplus(smt_ref[0, 8 * d:8 * d + 8, :] + part_ref[8 * d:8 * d + 8, 0:1])
        lat = dtt * (-jnp.exp(part_ref[8 * d:8 * d + 8, 1:2]))
        cumt = _dot(lat, uppf if d == 0 else lowf, precision=HIGHEST)
        expand = e_ref[d]
        dtx = _dot_exact_rhs(dt, expand)
        cumx = _dot_exact_rhs(cum, expand)
        tot = cumx[q - 1:q, :] if d == 0 else cumx[0:1, :]
        xdt = xs * dtx
        ecum = jnp.exp(cumx)
        xw = (xdt * jnp.exp(tot - cumx)).astype(BF16)
        etot = jnp.exp(tot)
        for g in range(2):
            bg = xbc[:, 512 + 128 * g:640 + 128 * g]
            cg = xbc[:, 768 + 128 * g:896 + 128 * g].astype(BF16)
            sl = slice(256 * g, 256 * g + 256)
            s_old = s_scr[d, :, sl]
            probs.append(dict(
                d=d, g=g, sl=sl, mask=low if d == 0 else upp, cum=cum, cumt=cumt, xg=xdt[:, sl],
                scores=_dot_nt(cg, bg.astype(BF16)), y=_dot(cg, s_old.astype(BF16)) * ecum[:, sl],
                s_new=s_old * etot[:, sl] + _dot(bg.T.astype(BF16), xw[:, sl]),
                y_ref=y_ref, skip=dx_ref[...] * xs if d == 0 else None))
    for hh in range(4):
        for pb in probs:
            d, h = pb['d'], 4 * pb['g'] + hh
            col = pb['cum'][:, 8 * d + h:8 * d + h + 1]
            row = pb['cumt'][h:h + 1, :]
            dec = jnp.exp(jnp.where(pb['mask'], col - row, NEG_INF))
            xm = jnp.where(lane == hh, pb['xg'], 0.0).astype(BF16)
            pb['y'] = pb['y'] + _dot((pb['scores'] * dec).astype(BF16), xm)
    for pb in probs:
        s_scr[pb['d'], :, pb['sl']] = pb['s_new']
    for d in range(2):
        y = jnp.concatenate([probs[2 * d]['y'], probs[2 * d + 1]['y']], axis=1)
        if probs[2 * d]['skip'] is not None:
            y = y + probs[2 * d]['skip']
        probs[2 * d]['y_ref'][0] = y

    @pl.when(c == nc - 1)
    def _():
        sfin_ref[0] = s_scr[...]


def _ssd(conv_x, small, small_t, par, par_t, dx, expand, s0):
    b, l, _ = conv_x.shape
    q = SSD_CHUNK
    nc = l // q
    fw = lambda b_, c: (b_, c, 0)
    bw = lambda b_, c: (b_, nc - 1 - c, 0)
    fwt = lambda b_, c: (b_, 0, c)
    bwt = lambda b_, c: (b_, 0, nc - 1 - c)
    full2 = lambda b_, c: (0, 0)
    return pl.pallas_call(
        functools.partial(_ssd_kernel, nc=nc),
        grid=(b, nc),
        in_specs=[pl.BlockSpec((1, q, 1024), fw), pl.BlockSpec((1, q, 1024), bw),
                  pl.BlockSpec((1, q, 128), fw), pl.BlockSpec((1, q, 128), bw),
                  pl.BlockSpec((1, 32, q), fwt), pl.BlockSpec((1, 32, q), bwt),
                  pl.BlockSpec((8, 128), full2), pl.BlockSpec((32, 128), full2),
                  pl.BlockSpec((1, 512), full2),
                  pl.BlockSpec((2, 128, 512), lambda b_, c: (0, 0, 0)),
                  pl.BlockSpec((1, 2, 128, 512), lambda b_, c: (b_, 0, 0, 0))],
        out_specs=[pl.BlockSpec((1, q, 512), fw), pl.BlockSpec((1, q, 512), bw),
                   pl.BlockSpec((1, 2, 128, 512), lambda b_, c: (b_, 0, 0, 0))],
        out_shape=[jax.ShapeDtypeStruct((b, l, 512), F32), jax.ShapeDtypeStruct((b, l, 512), F32),
                   jax.ShapeDtypeStruct((b, 2, 128, 512), F32)],
        scratch_shapes=[pltpu.VMEM((2, 128, 512), F32)],
        compiler_params=_cp(("parallel", "arbitrary")),
        name="ssd_scan",
    )(conv_x, conv_x, small, small, small_t, small_t, par, par_t, dx, expand, s0)


def _gdn_kernel(qf_ref, kf_ref, vf_ref, qb_ref, kb_ref, vb_ref, smf_ref, smb_ref, smtf_ref, smtb_ref,
                par_ref, part_ref, s0_ref, of_ref, ob_ref, sfin_ref, s_scr, *, nc):
    c = pl.program_id(1)

    @pl.when(c == 0)
    def _():
        s_scr[...] = s0_ref[0]

    cl = GDN_CHUNK
    nh = GDN_HEADS
    pk = nh * cl
    rows = 2 * cl
    ii = lax.broadcasted_iota(jnp.int32, (pk, pk), 0)
    jj = lax.broadcasted_iota(jnp.int32, (pk, pk), 1)
    same = (ii // cl) == (jj // cl)
    eye = (ii == jj).astype(F32)
    ri = lax.broadcasted_iota(jnp.int32, (rows, rows), 0)
    rj = lax.broadcasted_iota(jnp.int32, (rows, rows), 1)
    rsame = (ri // cl) == (rj // cl)
    zpad = jnp.zeros((cl, GDN_DK), F32)
    dirs = ((qf_ref, kf_ref, vf_ref, smf_ref, smtf_ref, of_ref), (qb_ref, kb_ref, vb_ref, smb_ref, smtb_ref, ob_ref))
    heads = range(nh)
    probs = []
    for d, (q_ref, k_ref, v_ref, sm_ref, smt_ref, o_ref) in enumerate(dirs):
        incl = same & ((jj <= ii) if d == 0 else (jj >= ii))
        strict = same & ((jj < ii) if d == 0 else (jj > ii))
        tri = (rsame & ((rj <= ri) if d == 0 else (rj >= ri))).astype(F32)
        tri_t = (rsame & ((ri <= rj) if d == 0 else (ri >= rj))).astype(F32)
        sm = sm_ref[0]
        beta_all = _sigmoid(sm)
        g_all = -jnp.exp(par_ref[2:3, :]) * _softplus(sm + par_ref[3:4, :])
        gt8 = -jnp.exp(part_ref[24:32, 2:3]) * _softplus(smt_ref[0, 24:32, :] + part_ref[24:32, 3:4])
        cum_all = _dot(tri, g_all, precision=HIGHEST)
        cumt_all = _dot(gt8, tri_t, precision=HIGHEST)
        for sc in ((0, 1) if d == 0 else (1, 0)):
            rs = slice(cl * sc, cl * sc + cl)
            stack = lambda ref: jnp.concatenate([ref[0, rs, GDN_DK * h:GDN_DK * h + GDN_DK] for h in heads], axis=0)
            qp, kp, vp = stack(q_ref), stack(k_ref), stack(v_ref)
            qp = qp * lax.rsqrt(jnp.sum(qp * qp, axis=-1, keepdims=True) + 1e-6) * (GDN_DK ** -0.5)
            kp = kp * lax.rsqrt(jnp.sum(kp * kp, axis=-1, keepdims=True) + 1e-6)
            col = jnp.concatenate([cum_all[rs, SM_A + 4 * d + h:SM_A + 4 * d + h + 1] for h in heads], axis=0)
            row = jnp.concatenate([cumt_all[4 * d + h:4 * d + h + 1, rs] for h in heads], axis=1)
            beta = jnp.concatenate([beta_all[rs, SM_BETA + 4 * d + h:SM_BETA + 4 * d + h + 1] for h in heads], axis=0)
            edge = cl - 1 if d == 0 else 0
            tot = jnp.concatenate([jnp.broadcast_to(col[cl * h + edge:cl * h + edge + 1, :], (cl, 1)) for h in heads],
                                  axis=0)
            dec = jnp.exp(jnp.where(incl, col - row, NEG_INF))
            kb = kp.astype(BF16)
            a_mat = jnp.where(strict, _dot_nt(kb, kb) * dec * beta, 0.0)
            ecol = jnp.exp(col)
            probs.append(dict(
                d=d, rs=rs, o_ref=o_ref, t=eye - a_mat, p=a_mat,
                rhs=jnp.concatenate([vp * beta, kp * (beta * ecol)], axis=1),
                attn=(_dot_nt(qp.astype(BF16), kb) * dec).astype(BF16),
                qg=qp * ecol, kdec=kp * jnp.exp(tot - col), etot=jnp.exp(tot)))
    for _ in range(5):
        for pb in probs:
            pbf = pb['p'].astype(BF16)
            pb['p'] = _dot(pbf, pbf)
        for pb in probs:
            pb['t'] = pb['t'] + _dot(pb['t'].astype(BF16), pb['p'].astype(BF16))
    for pb in probs:
        pb['uw'] = _dot3(pb['t'], pb['rhs'])
    for step in range(2):
        for pb in (probs[step], probs[2 + step]):
            d, rs, uw = pb['d'], pb['rs'], pb['uw']
            vnew, qs = [], []
            for h in heads:
                hr = slice(cl * h, cl * h + cl)
                wq = jnp.concatenate([uw[hr, GDN_DK:], pb['qg'][hr]], axis=0).astype(BF16)
                ws_qs = _dot(wq, s_scr[d, h].astype(BF16))
                vnew.append(uw[hr, :GDN_DK] - ws_qs[0:cl])
                qs.append(ws_qs[cl:])
            vn = jnp.concatenate(vnew, axis=0)
            out = jnp.concatenate(qs, axis=0) + _dot(pb['attn'], vn.astype(BF16))
            for h in heads:
                hr = slice(cl * h, cl * h + cl)
                pb['o_ref'][0, rs, GDN_DK * h:GDN_DK * h + GDN_DK] = out[hr]
                kd_t = jnp.concatenate([pb['kdec'][hr], zpad], axis=0).T.astype(BF16)
                vn_pad = jnp.concatenate([vnew[h], zpad], axis=0).astype(BF16)
                s_scr[d, h] = s_scr[d, h] * pb['etot'][cl * h:cl * h + 1, :] + _dot(kd_t, vn_pad)

    @pl.when(c == nc - 1)
    def _():
        sfin_ref[0] = s_scr[...]


def _gdn(conv_g, small, small_t, par, par_t, s0):
    b, l, _ = conv_g.shape
    rows = 2 * GDN_CHUNK
    nc = l // rows
    fwm = lambda j: (lambda b_, c: (b_, c, j))
    bwm = lambda j: (lambda b_, c: (b_, nc - 1 - c, j))
    fw, bw = fwm(0), bwm(0)
    fwt = lambda b_, c: (b_, 0, c)
    bwt = lambda b_, c: (b_, 0, nc - 1 - c)
    full2 = lambda b_, c: (0, 0)
    st = lambda b_, c: (b_, 0, 0, 0, 0)
    return pl.pallas_call(
        functools.partial(_gdn_kernel, nc=nc),
        grid=(b, nc),
        in_specs=[pl.BlockSpec((1, rows, 512), fwm(0)), pl.BlockSpec((1, rows, 512), fwm(1)),
                  pl.BlockSpec((1, rows, 512), fwm(2)),
                  pl.BlockSpec((1, rows, 512), bwm(0)), pl.BlockSpec((1, rows, 512), bwm(1)),
                  pl.BlockSpec((1, rows, 512), bwm(2)),
                  pl.BlockSpec((1, rows, 128), fw), pl.BlockSpec((1, rows, 128), bw),
                  pl.BlockSpec((1, 32, rows), fwt), pl.BlockSpec((1, 32, rows), bwt),
                  pl.BlockSpec((8, 128), full2), pl.BlockSpec((32, 128), full2),
                  pl.BlockSpec((1, 2, GDN_HEADS, 128, 128), st)],
        out_specs=[pl.BlockSpec((1, rows, 512), fw), pl.BlockSpec((1, rows, 512), bw),
                   pl.BlockSpec((1, 2, GDN_HEADS, 128, 128), st)],
        out_shape=[jax.ShapeDtypeStruct((b, l, 512), F32), jax.ShapeDtypeStruct((b, l, 512), F32),
                   jax.ShapeDtypeStruct((b, 2, GDN_HEADS, 128, 128), F32)],
        scratch_shapes=[pltpu.VMEM((2, GDN_HEADS, 128, 128), F32)],
        compiler_params=_cp(("parallel", "arbitrary")),
        name="gdn_scan",
    )(conv_g, conv_g, conv_g, conv_g, conv_g, conv_g, small, small, small_t, small_t, par, par_t, s0)


def _pool_body(x_ref, w_ref, sc_ref, o_ref, pa, pb, *, l, gw, two_d, win, rc):
    lo = win // 2
    hi = win - 1 - lo
    nrow = l // gw
    shift = int(math.log2(gw))
    pbw = 8 * gw
    z8 = jnp.zeros((8, 128), F32)
    pa[0:8, :] = z8
    pa[l + 8:l + 16, :] = z8
    pa[8:l + 8, :] = x_ref[0].astype(F32)
    if two_d:
        zb = jnp.zeros((pbw, 128), F32)
        pb[0:pbw, :] = zb
        pb[pbw + l:pbw + l + pbw, :] = zb

    def finish(r0, acc, pos):
        col = pos & (gw - 1)
        cnt = (jnp.minimum(col + hi, gw - 1) - jnp.maximum(col - lo, 0) + 1).astype(F32)
        if two_d:
            row = pos >> shift
            cnt = cnt * (jnp.minimum(row + hi, nrow - 1) - jnp.maximum(row - lo, 0) + 1).astype(F32)
        dlt = (acc / cnt - x_ref[0, r0:r0 + rc, :].astype(F32)).astype(BF16)
        o_ref[0, r0:r0 + rc, :] = _dot(dlt, w_ref[0]) * sc_ref[...]

    for r0 in range(0, l, rc):
        pos = lax.broadcasted_iota(jnp.int32, (rc, 128), 0) + r0
        col = pos & (gw - 1)
        acc = pa[8 + r0:8 + r0 + rc, :]
        for j in range(-lo, hi + 1):
            if j == 0:
                continue
            v = pa[8 + r0 + j:8 + r0 + j + rc, :]
            ok = (col + j >= 0) if j < 0 else (col + j < gw)
            acc = acc + jnp.where(ok, v, 0.0)
        if two_d:
            pb[pbw + r0:pbw + r0 + rc, :] = acc
        else:
            finish(r0, acc, pos)
    if two_d:
        for r0 in range(0, l, rc):
            pos = lax.broadcasted_iota(jnp.int32, (rc, 128), 0) + r0
            acc = pb[pbw + r0:pbw + r0 + rc, :]
            for j in range(-lo, hi + 1):
                if j != 0:
                    acc = acc + pb[pbw + r0 + gw * j:pbw + r0 + gw * j + rc, :]
            finish(r0, acc, pos)


def _pool_kernel(x_ref, w_ref, sc_ref, o_ref, pa, pb, **kw):
    g = pl.program_id(1)
    for gi, win in enumerate(POOL_WINDOWS):
        @pl.when(g == gi)
        def _(win=win):
            _pool_body(x_ref, w_ref, sc_ref, o_ref, pa, pb, win=win, **kw)


def _pool(proj, pool_w, pool_scale, two_d):
    b, l, _ = proj.shape
    gw = GRID_W if two_d else l
    assert gw & (gw - 1) == 0 and l % gw == 0
    rc = min(512, l)
    cb0 = OFF_POOL // 128
    pb_rows = l + 16 * gw if two_d else 8
    return pl.pallas_call(
        functools.partial(_pool_kernel, l=l, gw=gw, two_d=two_d, rc=rc),
        grid=(b, len(POOL_WINDOWS)),
        in_specs=[pl.BlockSpec((1, l, 128), lambda b_, g: (b_, 0, cb0 + g)),
                  pl.BlockSpec((1, 128, 128), lambda b_, g: (g, 0, 0)),
                  pl.BlockSpec((1, 128), lambda b_, g: (0, g))],
        out_specs=pl.BlockSpec((1, l, 128), lambda b_, g: (b_, 0, g)),
        out_shape=jax.ShapeDtypeStruct((b, l, BRANCH), F32),
        scratch_shapes=[pltpu.VMEM((l + 16, 128), F32), pltpu.VMEM((pb_rows, 128), F32)],
        compiler_params=_cp(("parallel", "parallel")),
        name="pool_branch",
    )(proj, pool_w, pool_scale)


def _four1_kernel(x_ref, cc_ref, ss_ref, xc_ref, xs_ref):
    x = x_ref[0].astype(F32)
    xc_ref[0] = _dot3(x, cc_ref[...]).astype(BF16)
    xs_ref[0] = _dot3(x, ss_ref[...]).astype(BF16)


def _four2_kernel(ct_ref, st_ref, xc_ref, xs_ref, o_ref):
    o_ref[0] = _dot(ct_ref[...], xc_ref[0]) + _dot(st_ref[...], xs_ref[0])


def _fourier(proj, tabs):
    cc, ss, cl_, sl_ = tabs
    b, l, _ = proj.shape
    tm = min(512, l)
    cb = OFF_FOUR // BRANCH
    xc, xs = pl.pallas_call(
        _four1_kernel,
        grid=(b, l // tm),
        in_specs=[pl.BlockSpec((1, tm, BRANCH), lambda b_, i: (b_, i, cb)),
                  pl.BlockSpec((BRANCH, BRANCH), lambda b_, i: (0, 0)),
                  pl.BlockSpec((BRANCH, BRANCH), lambda b_, i: (0, 0))],
        out_specs=[pl.BlockSpec((1, tm, BRANCH), lambda b_, i: (b_, i, 0))] * 2,
        out_shape=[jax.ShapeDtypeStruct((b, l, BRANCH), BF16)] * 2,
        compiler_params=_cp(("parallel", "parallel")),
        name="dft_channels",
    )(proj, cc, ss)
    tr = min(256, l)
    return pl.pallas_call(
        _four2_kernel,
        grid=(b, l // tr),
        in_specs=[pl.BlockSpec((tr, l), lambda b_, i: (i, 0)),
                  pl.BlockSpec((tr, l), lambda b_, i: (i, 0)),
                  pl.BlockSpec((1, l, BRANCH), lambda b_, i: (b_, 0, 0)),
                  pl.BlockSpec((1, l, BRANCH), lambda b_, i: (b_, 0, 0))],
        out_specs=pl.BlockSpec((1, tr, BRANCH), lambda b_, i: (b_, i, 0)),
        out_shape=jax.ShapeDtypeStruct((b, l, BRANCH), F32),
        compiler_params=_cp(("parallel", "parallel")),
        name="dft_positions",
    )(cl_, sl_, xc, xs)


def _dft_tables(l):
    c = BRANCH // 4
    k = jnp.arange(c, dtype=jnp.int32)
    ang = (2.0 * math.pi / c) * ((k[:, None] * k[None, :]) % c).astype(F32)
    scale = 1.0 / math.sqrt(l * c)
    eye4 = jnp.eye(4, dtype=F32)
    cc = jnp.kron(eye4, jnp.cos(ang) * scale)
    ss = jnp.kron(eye4, jnp.sin(ang) * scale)
    g = 64
    assert l % g == 0
    kk = jnp.arange(l, dtype=jnp.int32)[None, :]
    alpha = (2.0 * math.pi / l) * ((g * jnp.arange(l // g, dtype=jnp.int32)[:, None] * kk) % l).astype(F32)
    beta = (2.0 * math.pi / l) * ((jnp.arange(g, dtype=jnp.int32)[:, None] * kk) % l).astype(F32)
    ca, sa = jnp.cos(alpha)[:, None, :], jnp.sin(alpha)[:, None, :]
    cb, sb = jnp.cos(beta)[None, :, :], jnp.sin(beta)[None, :, :]
    cos_l = (ca * cb - sa * sb).reshape(l, l).astype(BF16)
    neg_sin_l = (-(sa * cb + ca * sb)).reshape(l, l).astype(BF16)
    return cc, ss, cos_l, neg_sin_l


def _merge_kernel(x_ref, g1_ref, lng_ref, lnb_ref, gates_ref, pool_ref, four_ref, yf_ref, yb_ref, sz_ref,
                  of_ref, ob_ref, gz_ref, snw_ref, gnw_ref, wbr_ref, wout_ref, o_ref, *, alpha):
    d = x_ref.shape[2]
    ssd_g = (yf_ref[0] + yb_ref[0]) * _silu(sz_ref[0].astype(F32))
    parts = []
    for g in range(2):
        blk = ssd_g[:, 256 * g:256 * g + 256]
        parts.append(blk * lax.rsqrt(jnp.mean(blk * blk, axis=-1, keepdims=True) + LN_EPS))
    ssd_out = jnp.concatenate(parts, axis=1) * snw_ref[...]
    o = of_ref[0] + ob_ref[0]
    parts = []
    for h in range(GDN_HEADS):
        blk = o[:, 128 * h:128 * h + 128]
        parts.append(blk * lax.rsqrt(jnp.mean(blk * blk, axis=-1, keepdims=True) + LN_EPS))
    gdn_out = jnp.concatenate(parts, axis=1) * gnw_ref[...] * _silu(gz_ref[0].astype(F32))
    merged = None
    for i, br in enumerate((pool_ref[0], four_ref[0], ssd_out, gdn_out)):
        term = _sigmoid(gates_ref[0, :, d * i:d * i + d].astype(F32)) * _dot(br.astype(BF16), wbr_ref[i])
        merged = term if merged is None else merged + term
    mix = _dot(merged.astype(BF16), wout_ref[...])
    y = alpha * x_ref[0] + g1_ref[0] * mix
    o_ref[0] = _ln(y) * lng_ref[...] + lnb_ref[...]


def _merge(x, gate1, ln_g, ln_b, proj, pool_o, four_o, yf, yb, of, ob, snw, gnw, wbr, wout, alpha):
    b, l, d = x.shape
    tm = min(256, l)
    row = lambda b_, i: (b_, i, 0)
    colb = lambda j: (lambda b_, i: (b_, i, j))
    vec = lambda b_, i: (0, 0)
    br = pl.BlockSpec((1, tm, BRANCH), row)
    return pl.pallas_call(
        functools.partial(_merge_kernel, alpha=alpha),
        grid=(b, l // tm),
        in_specs=[pl.BlockSpec((1, tm, d), row),
                  pl.BlockSpec((1, 1, d), _bmap(gate1.shape[0])),
                  pl.BlockSpec((1, d), vec), pl.BlockSpec((1, d), vec),
                  pl.BlockSpec((1, tm, 4 * d), colb(OFF_GATES // (4 * d))),
                  br, br, br, br,
                  pl.BlockSpec((1, tm, BRANCH), colb(OFF_SZ // BRANCH)),
                  br, br,
                  pl.BlockSpec((1, tm, BRANCH), colb(OFF_GZ // BRANCH)),
                  pl.BlockSpec((1, BRANCH), vec), pl.BlockSpec((1, BRANCH), vec),
                  pl.BlockSpec((4, BRANCH, d), lambda b_, i: (0, 0, 0)),
                  pl.BlockSpec((d, d), vec)],
        out_specs=pl.BlockSpec((1, tm, d), row),
        out_shape=jax.ShapeDtypeStruct((b, l, d), F32),
        compiler_params=_cp(("parallel", "parallel")),
        name="branch_merge",
    )(x, gate1, ln_g, ln_b, proj, pool_o, four_o, yf, yb, proj, of, ob, proj, snw, gnw, wbr, wout)


def _sort_network(n):
    pairs = []

    def merge(lo, hi, r):
        step = r * 2
        if step < hi - lo:
            merge(lo, hi, step)
            merge(lo + r, hi, step)
            pairs.extend((i, i + r) for i in range(lo + r, hi - r, step))
        else:
            pairs.append((lo, lo + r))

    def sort(lo, hi):
        if hi - lo >= 1:
            mid = lo + (hi - lo) // 2
            sort(lo, mid)
            sort(mid + 1, hi)
            merge(lo, hi, 1)

    sort(0, n - 1)
    return pairs


def _top_desc(x, n):
    nrow = x.shape[0] // 8
    lst = [x[8 * r:8 * r + 8, :] for r in range(nrow)]
    npad = 1 << (nrow - 1).bit_length()
    lst += [jnp.full(lst[0].shape, NEG_INF, F32)] * (npad - nrow)
    for i, j in _sort_network(npad):
        lst[i], lst[j] = jnp.maximum(lst[i], lst[j]), jnp.minimum(lst[i], lst[j])
    lst = lst[:nrow]
    sub = lax.broadcasted_iota(jnp.int32, lst[0].shape, 0).astype(F32)
    out = []
    for r in range(n):
        m = jnp.max(lst[0], axis=0, keepdims=True)
        out.append(m)
        left = n - r - 1
        if left:
            win = jnp.min(jnp.where(lst[0] == m, sub, 8.0), axis=0, keepdims=True)
            winner = sub == win
            for j in range(min(nrow, left)):
                nxt = lst[j + 1] if j + 1 < nrow else NEG_INF
                lst[j] = jnp.where(winner, nxt, lst[j])
    return out


def _peer_kernel(x_ref, sh_ref, sc_ref, g2_ref, lng_ref, lnb_ref, wqt_ref, keys_ref, u_ref, vt_ref, o_ref,
                 h_scr, qt_scr, th_scr, a1_scr, s2_scr, cand_scr, act_scr, p_scr, acc_scr,
                 *, n_i1, n_eb, alpha):
    eb = pl.program_id(2)
    t = x_ref.shape[1]
    ntc = t // 128
    k1 = PEER_TOPK + 1
    pairs = [(a, b) for a in range(k1) for b in range(k1) if (a + 1) * (b + 1) <= k1]

    @pl.when(eb == 0)
    def _():
        h = (_ln(x_ref[0]) * (1.0 + sc_ref[0]) + sh_ref[0]).T.astype(BF16)
        h_scr[...] = h
        qt_scr[...] = _dot(wqt_ref[...], h).astype(BF16)
        acc_scr[...] = jnp.zeros(acc_scr.shape, F32)
        p_scr[1] = jnp.zeros(p_scr.shape[1:], BF16)
        for r in range(len(pairs), cand_scr.shape[0]):
            cand_scr[r:r + 1, :] = jnp.full((1, 128), NEG_INF, F32)

        def head_body(hd, carry):
            for tc in range(ntc):
                ts = slice(128 * tc, 128 * tc + 128)
                sv, tops = [], []
                for s in range(2):
                    hs = 2 * hd + s
                    q_rows = qt_scr[pl.ds(pl.multiple_of(hs * 128, 128), 128), ts]
                    st = _dot(keys_ref[hs], q_rows)
                    sv.append(st)
                    tops.append(_top_desc(st, k1))
                v1, v2 = tops
                for r, (a, b) in enumerate(pairs):
                    cand_scr[r:r + 1, :] = v1[a] + v2[b]
                best = _top_desc(cand_scr[...], k1)
                z = None
                for kk in range(PEER_TOPK):
                    term = jnp.exp(best[kk] - best[0])
                    z = term if z is None else z + term
                thr = 0.5 * (best[PEER_TOPK - 1] + best[PEER_TOPK])
                th_scr[hd, tc] = (thr - sv[0]) * LOG2E
                a1_scr[hd, tc] = (sv[0] - best[0]) * LOG2E - jnp.log(z) * LOG2E
                s2_scr[hd, tc] = sv[1] * LOG2E
            return carry

        lax.fori_loop(0, PEER_HEADS, head_body, 0)

    def activations(slot, part, parts):
        n = u_ref.shape[0] // parts
        rs = slice(n * part, n * part + n)
        act = _gelu(_dot(u_ref[rs, :], h_scr[...]))
        for tc in range(ntc):
            act_scr[slot, tc, rs, :] = act[:, 128 * tc:128 * tc + 128]

    def gated(slot, r):
        i1 = (eb - 1) * n_i1 + r
        rows = slice(PEER_NKEYS * r, PEER_NKEYS * r + PEER_NKEYS)
        for tc in range(ntc):
            w = None
            for hd in range(PEER_HEADS):
                th = th_scr[hd, tc, pl.ds(i1, 1), :]
                a1 = a1_scr[hd, tc, pl.ds(i1, 1), :]
                s2 = s2_scr[hd, tc]
                term = jnp.where(s2 > th, jnp.exp2(s2 + a1), 0.0)
                w = term if w is None else w + term
            p_scr[slot, rows, 128 * tc:128 * tc + 128] = (w * act_scr[slot, tc, rows, :]).astype(BF16)

    def accumulate(slot, part, parts):
        n = vt_ref.shape[0] // parts
        rs = slice(n * part, n * part + n)
        acc_scr[rs, :] += _dot(vt_ref[rs, :], p_scr[slot])

    def interleaved(act_slot, acc_slot, gate_slot):
        parts = n_i1 // 2
        for q in range(parts):
            if act_slot is not None:
                activations(act_slot, q, parts)
            gated(gate_slot, 2 * q)
            accumulate(acc_slot, q, parts)
            gated(gate_slot, 2 * q + 1)

    @pl.when(eb == 0)
    def _():
        activations(0, 0, 1)

    for parity in range(2):
        @pl.when((eb >= 1) & (eb < n_eb) & (eb % 2 == parity))
        def _(parity=parity):
            interleaved(parity, parity, 1 - parity)

    @pl.when(eb == n_eb)
    def _():
        interleaved(None, n_eb % 2, (n_eb - 1) % 2)

    @pl.when(eb == n_eb + 1)
    def _():
        ffn = acc_scr[...] + _dot(vt_ref[...], p_scr[(n_eb - 1) % 2])
        y = alpha * x_ref[0] + g2_ref[0] * ffn.T
        o_ref[0] = _ln(y) * lng_ref[...] + lnb_ref[...]


def _peer_ffn(x, shift, scale, gate2, ln_g, ln_b, wqt, keys, u_tab, v_tab_t, alpha):
    b, l, d = x.shape
    t = min(512, l)
    n_exp = u_tab.shape[0]
    n_i1 = 8
    be = n_i1 * PEER_NKEYS
    n_eb = n_exp // be
    nh2 = 2 * PEER_HEADS
    k1 = PEER_TOPK + 1
    npairs = sum(1 for a in range(k1) for b_ in range(k1) if (a + 1) * (b_ + 1) <= k1)
    ncand = -(-npairs // 8) * 8
    vec = lambda b_, i, e: (0, 0)
    per_head = pltpu.VMEM((PEER_HEADS, t // 128, PEER_NKEYS, 128), F32)
    return pl.pallas_call(
        functools.partial(_peer_kernel, n_i1=n_i1, n_eb=n_eb, alpha=alpha),
        grid=(b, l // t, n_eb + 2),
        in_specs=[pl.BlockSpec((1, t, d), lambda b_, i, e: (b_, i, 0)),
                  pl.BlockSpec((1, 1, d), _bmap(shift.shape[0])),
                  pl.BlockSpec((1, 1, d), _bmap(scale.shape[0])),
                  pl.BlockSpec((1, 1, d), _bmap(gate2.shape[0])),
                  pl.BlockSpec((1, d), vec), pl.BlockSpec((1, d), vec),
                  pl.BlockSpec((nh2 * 128, d), vec),
                  pl.BlockSpec((nh2, PEER_NKEYS, 128), lambda b_, i, e: (0, 0, 0)),
                  pl.BlockSpec((be, d), lambda b_, i, e: (jnp.minimum(e, n_eb - 1), 0)),
                  pl.BlockSpec((d, be), lambda b_, i, e: (0, jnp.clip(e - 2, 0, n_eb - 1)))],
        out_specs=pl.BlockSpec((1, t, d), lambda b_, i, e: (b_, i, 0)),
        out_shape=jax.ShapeDtypeStruct((b, l, d), F32),
        scratch_shapes=[pltpu.VMEM((d, t), BF16), pltpu.VMEM((nh2 * 128, t), BF16),
                        per_head, per_head, per_head,
                        pltpu.VMEM((ncand, 128), F32), pltpu.VMEM((2, t // 128, be, 128), F32),
                        pltpu.VMEM((2, be, t), BF16),
                        pltpu.VMEM((d, t), F32)],
        compiler_params=_cp(("parallel", "parallel", "arbitrary")),
        name="peer_ffn",
    )(x, shift, scale, gate2, ln_g, ln_b, wqt, keys, u_tab, v_tab_t)


def _prep_layer_weights(w_in, pool_scale, ssd_conv_w, ssd_conv_b, ssd_dt_bias, ssd_a_log, ssd_d, ssd_norm_w,
                        gdn_conv_w, gdn_dt_bias, gdn_a_log, gdn_norm_w, peer_wq, peer_keys, peer_u, peer_v):
    nl, d, _ = w_in.shape
    seg = lambda a, b_: w_in[:, :, a:b_]
    main = jnp.concatenate([seg(4640, 8736), seg(0, 512), seg(512, 1024), seg(1024, 2048), seg(2576, 4112),
                            seg(2048, 2560), seg(4112, 4624)], axis=-1).astype(BF16)
    small = jnp.concatenate([seg(2560, 2576), seg(4624, 4632), seg(4632, 4640)], axis=-1)
    w_sm = jnp.pad(small, ((0, 0), (0, 0), (0, 96))).astype(BF16)
    w_smt = jnp.swapaxes(small, 1, 2).astype(BF16)
    conv_s = jnp.pad(ssd_conv_w, ((0, 0), (0, 8 - CONV_K), (0, 0)))
    conv_g = jnp.pad(gdn_conv_w, ((0, 0), (0, 8 - CONV_K), (0, 0)))
    zeros = lambda n: jnp.zeros((nl, n), F32)
    bias_row = jnp.concatenate([ssd_dt_bias.reshape(nl, 16), zeros(112)], axis=1)
    alog_row = jnp.concatenate([ssd_a_log.reshape(nl, 16), zeros(112)], axis=1)
    galog_row = jnp.concatenate([zeros(24), gdn_a_log.reshape(nl, 8), zeros(96)], axis=1)
    gbias_row = jnp.concatenate([zeros(24), gdn_dt_bias.reshape(nl, 8), zeros(96)], axis=1)
    par = jnp.stack([bias_row, alog_row, galog_row, gbias_row] + [zeros(128)] * 4, axis=1)
    par_t = jnp.pad(jnp.swapaxes(par[:, 0:4, 0:32], 1, 2), ((0, 0), (0, 0), (0, 124)))
    dx = jnp.repeat(ssd_d, SSD_HEADDIM, axis=1).reshape(nl, 1, BRANCH)
    return dict(
        main=main, w_sm=w_sm, w_smt=w_smt, conv_s=conv_s, conv_sb=ssd_conv_b.reshape(nl, 1, -1),
        conv_g=conv_g, conv_gb=jnp.zeros((nl, 1, 3 * BRANCH), F32), par=par, par_t=par_t, dx=dx,
        pool_scale=pool_scale.reshape(nl, 1, BRANCH), snw=ssd_norm_w.reshape(nl, 1, BRANCH),
        gnw=jnp.tile(gdn_norm_w, (1, GDN_HEADS)).reshape(nl, 1, BRANCH),
        wqt=jnp.swapaxes(peer_wq, 1, 2).astype(BF16),
        keys=peer_keys.reshape(nl, 2 * PEER_HEADS, PEER_NKEYS, -1).astype(BF16),
        u=peer_u.astype(BF16),
        vt=jnp.swapaxes(peer_v.astype(BF16), 1, 2))


def _expand_table():
    lane = jnp.arange(128)[:, None]
    col = jnp.arange(BRANCH)[None, :] // SSD_HEADDIM
    return jnp.stack([(lane == col), (lane == col + SSD_HEADS)]).astype(F32)


def _scans(x, shift, scale, lw, l_idx, expand, states):
    proj = _lnmod_mm(_fold_rows(x, shift.shape[0] == 1, 1024), shift, scale, lw['main'][l_idx], 512)
    proj = proj.reshape(x.shape[0], x.shape[1], -1)
    small, small_t = _small_proj(x, shift, scale, lw['w_sm'][l_idx], lw['w_smt'][l_idx])
    conv_s = _conv_silu(proj, OFF_XBC, 1024, lw['conv_s'][l_idx], lw['conv_sb'][l_idx])
    conv_g = _conv_silu(proj, OFF_QKV, 1536, lw['conv_g'][l_idx], lw['conv_gb'][l_idx])
    yf, yb, s_ssd = _ssd(conv_s, small, small_t, lw['par'][l_idx], lw['par_t'][l_idx], lw['dx'][l_idx], expand,
                         states[0])
    of, ob, s_gdn = _gdn(conv_g, small, small_t, lw['par'][l_idx], lw['par_t'][l_idx], states[1])
    return proj, (yf, yb, of, ob), (s_ssd, s_gdn)


def _mixer(x, mods, lw, l_idx, expand, tabs, states, pool_w, w_branch, w_out, ln_g, ln_b, two_d, alpha):
    proj, (yf, yb, of, ob), new_states = _scans(x, mods[0], mods[1], lw, l_idx, expand, states)
    pool_o = _pool(proj, pool_w, lw['pool_scale'][l_idx], two_d)
    four_o = _fourier(proj, tabs)
    x = _merge(x, mods[2], ln_g, ln_b, proj, pool_o, four_o, yf, yb, of, ob, lw['snw'][l_idx], lw['gnw'][l_idx],
               w_branch, w_out, alpha)
    return x, new_states


def _fold_rows(x, shared_mod, rows):
    b, l, d = x.shape
    if shared_mod and l < rows and rows % l == 0 and b % (rows // l) == 0:
        return x.reshape(b * l // rows, rows, d)
    return x


def _peer(x, mods, lw, l_idx, ln_g, ln_b, alpha):
    xr = _fold_rows(x, mods[3].shape[0] == 1, 512)
    y = _peer_ffn(xr, mods[3], mods[4], mods[5], ln_g, ln_b, lw['wqt'][l_idx], lw['keys'][l_idx], lw['u'][l_idx],
                  lw['vt'][l_idx], alpha)
    return y.reshape(x.shape)


def kernel(x, c, ctx, c_ctx, w_mod, b_mod, w_in, pool_w, pool_scale, ssd_conv_w, ssd_conv_b, ssd_dt_bias, ssd_a_log, ssd_d, ssd_norm_w, gdn_conv_w, gdn_dt_bias, gdn_a_log, gdn_norm_w, w_branch, w_out, ln1_g, ln1_b, peer_wq, peer_keys, peer_u, peer_v, ln2_g, ln2_b):
    bsz, seq, d = x.shape
    nl = w_in.shape[0]
    alpha = (2 * nl) ** 0.25
    lw = _prep_layer_weights(w_in, pool_scale, ssd_conv_w, ssd_conv_b, ssd_dt_bias, ssd_a_log, ssd_d, ssd_norm_w,
                             gdn_conv_w, gdn_dt_bias, gdn_a_log, gdn_norm_w, peer_wq, peer_keys, peer_u, peer_v)
    pool_wb = pool_w.astype(BF16)
    w_branch_b = w_branch.astype(BF16)
    w_out_b = w_out.astype(BF16)
    expand = _expand_table()
    tabs_x = _dft_tables(seq)
    tabs_c = _dft_tables(ctx.shape[1])
    cs = jnp.concatenate([c, c_ctx[None], jnp.zeros((8 - bsz - 1, d), F32)], axis=0)
    mod_all = _mod_all(cs, w_mod.astype(BF16), b_mod).reshape(nl, 8, 6, 1, d)
    zero_states = (jnp.zeros((bsz, 2, SSD_STATE, BRANCH), F32), jnp.zeros((bsz, 2, GDN_HEADS, GDN_DK, GDN_DK), F32))
    vec = lambda a, l_idx: a[l_idx].reshape(1, d)
    for l_idx in range(nl):
        mods_x = [mod_all[l_idx, :bsz, i] for i in range(6)]
        mods_c = [mod_all[l_idx, bsz:bsz + 1, i] for i in range(6)]
        g1, b1, g2, b2 = vec(ln1_g, l_idx), vec(ln1_b, l_idx), vec(ln2_g, l_idx), vec(ln2_b, l_idx)
        if l_idx == nl - 1:
            _, _, states = _scans(ctx, mods_c[0], mods_c[1], lw, l_idx, expand, zero_states)
        else:
            ctx, states = _mixer(ctx, mods_c, lw, l_idx, expand, tabs_c, zero_states, pool_wb[l_idx],
                                 w_branch_b[l_idx], w_out_b[l_idx], g1, b1, False, alpha)
            ctx = _peer(ctx, mods_c, lw, l_idx, g2, b2, alpha)
        x, _ = _mixer(x, mods_x, lw, l_idx, expand, tabs_x, states, pool_wb[l_idx], w_branch_b[l_idx],
                      w_out_b[l_idx], g1, b1, True, alpha)
        x = _peer(x, mods_x, lw, l_idx, g2, b2, alpha)
    return x
```
